```python
import numpy as np
import jax
import jax.numpy as jnp
from jax import lax

D_MODEL = 1024
BATCH = 8
SEQ = 8192
DEPTH = 2
DEC_BATCH = 8
DEC_SEQ = 32
PAST_LEN = 4096

CHUNK = 64
Q_BLOCK = 128
N_EVEN = (DEPTH + 1) // 2
N_ODD = DEPTH // 2
RET_H = 4
RET_D = 128
RET_W = RET_H * RET_D
ROPE_BASE = 10000.0
FOX_H = 8
FOX_D = 64
FOX_W = FOX_H * FOX_D
AB_IN = 4 * RET_W + 3 * FOX_W + FOX_H
AB_OUT = RET_W + FOX_W
S5_GROUP = 16
S5_G = D_MODEL // S5_GROUP
S5_P = 64
DT_MIN = 1e-3
DT_MAX = 1e-1
MEM_LEN = 256
X_H = 4
X_D = D_MODEL // X_H
D_FF = 4 * D_MODEL
EPS = 1e-6
NEG_INF = -1e30
F32 = jnp.float32

kernel_name = 'hybrid_retention_fox_s5_stream_step'


def _rmsnorm(x, w):
    xf = x.astype(F32)
    y = xf * lax.rsqrt(jnp.mean(xf * xf, axis=-1, keepdims=True) + EPS)
    return y * w.astype(F32)


def _rotary(x, pos):
    half = x.shape[-1] // 2
    inv = ROPE_BASE ** (-jnp.arange(half, dtype=F32) / half)
    ang = pos.astype(F32)[:, None] * inv[None, :]
    cos = jnp.cos(ang)[None, :, None, :]
    sin = jnp.sin(ang)[None, :, None, :]
    xf = x.astype(F32)
    x1, x2 = xf[..., :half], xf[..., half:]
    return jnp.concatenate([x1 * cos - x2 * sin, x2 * cos + x1 * sin], axis=-1)


def _head_groupnorm(o, w):
    mu = jnp.mean(o, axis=-1, keepdims=True)
    var = jnp.mean(jnp.square(o - mu), axis=-1, keepdims=True)
    return (o - mu) * lax.rsqrt(var + EPS) * w.astype(F32)


def _retention(q, k, v, s0, chunk):
    B, T, H, d = q.shape
    n = T // chunk
    qc = q.reshape(B, n, chunk, H, d)
    kc = k.reshape(B, n, chunk, H, d)
    vc = v.reshape(B, n, chunk, H, d)
    lg = jnp.log(1.0 - 2.0 ** (-5.0 - jnp.arange(H, dtype=F32)))
    idx = jnp.arange(chunk, dtype=F32)
    dec_in = jnp.exp(lg[:, None, None] * jnp.abs(idx[:, None] - idx[None, :]))
    w_q = jnp.exp(lg[None, :] * (idx[:, None] + 1.0))
    w_k = jnp.exp(lg[None, :] * (chunk - 1.0 - idx[:, None]))
    scores = jnp.einsum('bnihd,bnjhd->bnhij', qc, kc) * dec_in
    intra = jnp.einsum('bnhij,bnjhe->bnihe', scores, vc)
    kv = jnp.einsum('bnjhd,bnjhe->nbhde', kc * w_k[:, :, None], vc)
    decay_chunk = jnp.exp(lg * chunk)[None, :, None, None]

    def step(s, kv_c):
        return s * decay_chunk + kv_c, s

    s_fin, s_before = lax.scan(step, s0, kv)
    inter = jnp.einsum('bnihd,nbhde->bnihe', qc * w_q[:, :, None], s_before)
    return (intra + inter).reshape(B, T, H, d), s_fin


def _fox_attend(q, k, v, cq, ck, qpos, kpos):
    s = jnp.einsum('bqhd,bkhd->bhqk', q, k).astype(F32) * (FOX_D ** -0.5)
    s = s + jnp.swapaxes(cq, 1, 2)[..., :, None] - jnp.swapaxes(ck, 1, 2)[..., None, :]
    s = jnp.where(kpos[None, :] <= qpos[:, None], s, NEG_INF)
    p = jax.nn.softmax(s, axis=-1)
    return jnp.einsum('bhqk,bkhd->bqhd', p.astype(v.dtype), v)


def _fox_prompt(q, k, v, logf):
    B, T, H, Dh = q.shape
    c = jnp.cumsum(logf, axis=1)
    kpos = jnp.arange(T)

    def block(i):
        start = i * Q_BLOCK
        qb = lax.dynamic_slice_in_dim(q, start, Q_BLOCK, axis=1)
        cb = lax.dynamic_slice_in_dim(c, start, Q_BLOCK, axis=1)
        return _fox_attend(qb, k, v, cb, c, start + jnp.arange(Q_BLOCK), kpos)

    out = lax.map(block, jnp.arange(T // Q_BLOCK))
    return jnp.moveaxis(out, 0, 1).reshape(B, T, H, Dh)


def _fox_sample(q, k, v, logf, k_past, v_past, logf_past):
    L = q.shape[1]
    P = k_past.shape[1]
    k_all = jnp.concatenate([k_past.astype(k.dtype), k], axis=1)
    v_all = jnp.concatenate([v_past.astype(v.dtype), v], axis=1)
    c = jnp.cumsum(jnp.concatenate([logf_past.astype(F32), logf], axis=1), axis=1)
    return _fox_attend(q, k_all, v_all, c[:, P:], c, P + jnp.arange(L), jnp.arange(P + L))


def _ab_mixer(h, pos, p, e, st):
    B, T, _ = h.shape
    z = h @ p['w_in_ab'][e]
    splits = [RET_W, 2 * RET_W, 3 * RET_W, 4 * RET_W,
              4 * RET_W + FOX_W, 4 * RET_W + 2 * FOX_W, 4 * RET_W + 3 * FOX_W]
    rq, rk, rv, rg, fq, fk, fv, fl = jnp.split(z, splits, axis=-1)
    rq = _rotary(rq.reshape(B, T, RET_H, RET_D), pos)
    rk = _rotary(rk.reshape(B, T, RET_H, RET_D), pos) * (RET_D ** -0.5)
    rv = rv.reshape(B, T, RET_H, RET_D).astype(F32)
    if st is None:
        s0 = jnp.zeros((B, RET_H, RET_D, RET_D), F32)
        blk = CHUNK
    else:
        s0 = st['ret'][e].astype(F32)
        blk = T
    ro, s_new = _retention(rq, rk, rv, s0, blk)
    ro = _head_groupnorm(ro, p['ret_gn_w'][e]).reshape(B, T, RET_W) * jax.nn.silu(rg.astype(F32))
    fq = fq.reshape(B, T, FOX_H, FOX_D)
    fk = fk.reshape(B, T, FOX_H, FOX_D)
    fv = fv.reshape(B, T, FOX_H, FOX_D)
    logf = jax.nn.log_sigmoid(fl.astype(F32) + p['b_fox_f'][e].astype(F32))
    if st is None:
        fo = _fox_prompt(fq, fk, fv, logf)
    else:
        fo = _fox_sample(fq, fk, fv, logf, st['fox_k'][e], st['fox_v'][e], st['fox_logf'][e])
    mixed = jnp.concatenate([ro.astype(h.dtype), fo.reshape(B, T, FOX_W).astype(h.dtype)], axis=-1)
    return mixed @ p['w_out_ab'][e], (s_new, fk, fv, logf)


def _s5_discretize(lam_re, lam_im, log_dt, b_re, b_im):
    lam_re = lam_re.astype(F32)
    lam_im = lam_im.astype(F32)
    dt = jnp.exp(log_dt.astype(F32))[:, None]
    mag = jnp.exp(lam_re * dt)
    a_re = mag * jnp.cos(lam_im * dt)
    a_im = mag * jnp.sin(lam_im * dt)
    den = lam_re * lam_re + lam_im * lam_im
    num_re = a_re - 1.0
    f_re = (num_re * lam_re + a_im * lam_im) / den
    f_im = (a_im * lam_re - num_re * lam_im) / den
    b_re = b_re.astype(F32)
    b_im = b_im.astype(F32)
    bb_re = f_re[..., None] * b_re - f_im[..., None] * b_im
    bb_im = f_re[..., None] * b_im + f_im[..., None] * b_re
    return a_re, a_im, bb_re, bb_im


def _s5_combine(e1, e2):
    a1r, a1i, b1r, b1i = e1
    a2r, a2i, b2r, b2i = e2
    return (a2r * a1r - a2i * a1i, a2r * a1i + a2i * a1r,
            a2r * b1r - a2i * b1i + b2r, a2r * b1i + a2i * b1r + b2i)


def _s5_mixer(h, p, o, st):
    B, T, _ = h.shape
    u = (h @ p['w_in_c'][o]).astype(F32).reshape(B, T, S5_G, S5_GROUP)
    a_re, a_im, bb_re, bb_im = _s5_discretize(p['s5_lambda_re'][o], p['s5_lambda_im'][o],
                                              p['s5_log_dt'][o], p['s5_b_re'][o], p['s5_b_im'][o])
    bu_re = jnp.einsum('gpc,btgc->btgp', bb_re, u)
    bu_im = jnp.einsum('gpc,btgc->btgp', bb_im, u)
    if st is not None:
        x0_re = st['s5_re'][o].astype(F32)
        x0_im = st['s5_im'][o].astype(F32)
        bu_re = bu_re.at[:, 0].add(a_re * x0_re - a_im * x0_im)
        bu_im = bu_im.at[:, 0].add(a_re * x0_im + a_im * x0_re)
    A_re = jnp.broadcast_to(a_re, bu_re.shape)
    A_im = jnp.broadcast_to(a_im, bu_im.shape)
    _, _, xr, xi = lax.associative_scan(_s5_combine, (A_re, A_im, bu_re, bu_im), axis=1)
    c_re = p['s5_c_re'][o].astype(F32)
    c_im = p['s5_c_im'][o].astype(F32)
    y = (jnp.einsum('gcp,btgp->btgc', c_re, xr) - jnp.einsum('gcp,btgp->btgc', c_im, xi)
         + p['s5_d'][o].astype(F32) * u)
    y = y.reshape(B, T, D_MODEL).astype(h.dtype)
    g = y @ p['w_glu'][o]
    out = g[..., :D_MODEL] * jax.nn.sigmoid(g[..., D_MODEL:])
    return out, (xr[:, -1], xi[:, -1])


def _memory_kv(mem, norm_w, w_k, w_v):
    B, M, _ = mem.shape
    m = _rmsnorm(mem, norm_w).astype(mem.dtype)
    return (m @ w_k).reshape(B, M, X_H, X_D), (m @ w_v).reshape(B, M, X_H, X_D)


def _cross_attn(h, mk, mv, w_q, w_o):
    B, T, _ = h.shape
    q = (h @ w_q).reshape(B, T, X_H, X_D)
    s = jnp.einsum('bthd,bmhd->bhtm', q, mk.astype(q.dtype)).astype(F32) * (X_D ** -0.5)
    pr = jax.nn.softmax(s, axis=-1)
    o = jnp.einsum('bhtm,bmhd->bthd', pr.astype(h.dtype), mv.astype(h.dtype))
    return o.reshape(B, T, D_MODEL) @ w_o


def _sq_relu_mlp(h, w_up, w_down):
    return jnp.square(jax.nn.relu(h @ w_up)) @ w_down


def _run_group(x, pos, p, mem=None, st=None):
    dt = x.dtype
    new = {'ret': [], 'fox_k': [], 'fox_v': [], 'fox_logf': [], 's5_re': [], 's5_im': [],
           'mem_k': [], 'mem_v': []}
    for layer in range(DEPTH):
        g = p['norm_w'][layer]
        h = _rmsnorm(x, g[0]).astype(dt)
        if layer % 2 == 0:
            mix, (s_ret, fk, fv, flf) = _ab_mixer(h, pos, p, layer // 2, st)
            new['ret'].append(s_ret)
            new['fox_k'].append(fk)
            new['fox_v'].append(fv)
            new['fox_logf'].append(flf)
        else:
            mix, (sr, si) = _s5_mixer(h, p, layer // 2, st)
            new['s5_re'].append(sr)
            new['s5_im'].append(si)
        x = x + _rmsnorm(mix, g[1]).astype(dt)
        h = _rmsnorm(x, g[2]).astype(dt)
        if st is None:
            mk, mv = _memory_kv(mem, p['mem_norm_w'][layer], p['w_xk'][layer], p['w_xv'][layer])
            new['mem_k'].append(mk)
            new['mem_v'].append(mv)
        else:
            mk, mv = st['mem_k'][layer], st['mem_v'][layer]
        x = x + _rmsnorm(_cross_attn(h, mk, mv, p['w_xq'][layer], p['w_xo'][layer]), g[3]).astype(dt)
        h = _rmsnorm(x, g[4]).astype(dt)
        x = x + _rmsnorm(_sq_relu_mlp(h, p['w_up'][layer], p['w_down'][layer]), g[5]).astype(dt)
    return x, new


def setup_inputs(seed: int = 0) -> dict:
    key = jax.random.key(seed)
    ks = jax.random.split(key, 40)

    def nrm(i, shape, scale=1.0):
        return jax.random.normal(ks[i], shape, F32) * scale

    n = jnp.arange(S5_P, dtype=F32)
    return {
        'x_prompt': nrm(0, (BATCH, SEQ, D_MODEL)),
        'x_sample': nrm(1, (DEC_BATCH, DEC_SEQ, D_MODEL)),
        'cache_ret_state': nrm(2, (N_EVEN, DEC_BATCH, RET_H, RET_D, RET_D), 0.5),
        'cache_fox_k': nrm(3, (N_EVEN, DEC_BATCH, PAST_LEN, FOX_H, FOX_D)),
        'cache_fox_v': nrm(4, (N_EVEN, DEC_BATCH, PAST_LEN, FOX_H, FOX_D)),
        'cache_fox_logf': jax.nn.log_sigmoid(2.5 + nrm(5, (N_EVEN, DEC_BATCH, PAST_LEN, FOX_H))),
        'state_s5_re': nrm(6, (N_ODD, DEC_BATCH, S5_G, S5_P), 0.1),
        'state_s5_im': nrm(7, (N_ODD, DEC_BATCH, S5_G, S5_P), 0.1),
        'cache_mem_k': nrm(8, (DEPTH, DEC_BATCH, MEM_LEN, X_H, X_D)),
        'cache_mem_v': nrm(9, (DEPTH, DEC_BATCH, MEM_LEN, X_H, X_D)),
        'mem_prompt': nrm(10, (BATCH, MEM_LEN, D_MODEL)),
        'norm_w': 1.0 + nrm(11, (DEPTH, 6, D_MODEL), 0.05),
        'w_in_ab': nrm(12, (N_EVEN, D_MODEL, AB_IN), D_MODEL ** -0.5),
        'b_fox_f': jax.random.uniform(ks[13], (N_EVEN, FOX_H), F32, 1.0, 4.0),
        'ret_gn_w': 1.0 + nrm(14, (N_EVEN, RET_H, RET_D), 0.05),
        'w_out_ab': nrm(15, (N_EVEN, AB_OUT, D_MODEL), AB_OUT ** -0.5),
        'w_in_c': nrm(16, (N_ODD, D_MODEL, D_MODEL), D_MODEL ** -0.5),
        's5_lambda_re': -0.5 + nrm(17, (N_ODD, S5_G, S5_P), 0.01),
        's5_lambda_im': jnp.pi * n + nrm(18, (N_ODD, S5_G, S5_P), 0.01),
        's5_log_dt': jax.random.uniform(ks[19], (N_ODD, S5_G), F32, float(np.log(DT_MIN)), float(np.log(DT_MAX))),
        's5_b_re': nrm(20, (N_ODD, S5_G, S5_P, S5_GROUP), (2.0 * S5_GROUP) ** -0.5),
        's5_b_im': nrm(21, (N_ODD, S5_G, S5_P, S5_GROUP), (2.0 * S5_GROUP) ** -0.5),
        's5_c_re': nrm(22, (N_ODD, S5_G, S5_GROUP, S5_P), (2.0 * S5_P) ** -0.5),
        's5_c_im': nrm(23, (N_ODD, S5_G, S5_GROUP, S5_P), (2.0 * S5_P) ** -0.5),
        's5_d': nrm(24, (N_ODD, S5_G, S5_GROUP)),
        'w_glu': nrm(25, (N_ODD, D_MODEL, 2 * D_MODEL), D_MODEL ** -0.5),
        'mem_norm_w': 1.0 + nrm(26, (DEPTH, D_MODEL), 0.05),
        'w_xq': nrm(27, (DEPTH, D_MODEL, D_MODEL), D_MODEL ** -0.5),
        'w_xk': nrm(28, (DEPTH, D_MODEL, D_MODEL), D_MODEL ** -0.5),
        'w_xv': nrm(29, (DEPTH, D_MODEL, D_MODEL), D_MODEL ** -0.5),
        'w_xo': nrm(30, (DEPTH, D_MODEL, D_MODEL), D_MODEL ** -0.5),
        'w_up': nrm(31, (DEPTH, D_MODEL, D_FF), D_MODEL ** -0.5),
        'w_down': nrm(32, (DEPTH, D_FF, D_MODEL), D_FF ** -0.5),
    }


def reference(x_prompt, x_sample, cache_ret_state, cache_fox_k, cache_fox_v, cache_fox_logf,
              state_s5_re, state_s5_im, cache_mem_k, cache_mem_v, mem_prompt,
              norm_w, w_in_ab, b_fox_f, ret_gn_w, w_out_ab, w_in_c,
              s5_lambda_re, s5_lambda_im, s5_log_dt, s5_b_re, s5_b_im, s5_c_re, s5_c_im, s5_d, w_glu,
              mem_norm_w, w_xq, w_xk, w_xv, w_xo, w_up, w_down):
    p = {'norm_w': norm_w, 'w_in_ab': w_in_ab, 'b_fox_f': b_fox_f, 'ret_gn_w': ret_gn_w,
         'w_out_ab': w_out_ab, 'w_in_c': w_in_c, 's5_lambda_re': s5_lambda_re,
         's5_lambda_im': s5_lambda_im, 's5_log_dt': s5_log_dt, 's5_b_re': s5_b_re, 's5_b_im': s5_b_im,
         's5_c_re': s5_c_re, 's5_c_im': s5_c_im, 's5_d': s5_d, 'w_glu': w_glu,
         'mem_norm_w': mem_norm_w, 'w_xq': w_xq, 'w_xk': w_xk, 'w_xv': w_xv, 'w_xo': w_xo,
         'w_up': w_up, 'w_down': w_down}
    st = {'ret': cache_ret_state, 'fox_k': cache_fox_k, 'fox_v': cache_fox_v,
          'fox_logf': cache_fox_logf, 's5_re': state_s5_re, 's5_im': state_s5_im,
          'mem_k': cache_mem_k, 'mem_v': cache_mem_v}
    pos_prompt = jnp.arange(x_prompt.shape[1])
    pos_sample = cache_fox_k.shape[2] + jnp.arange(x_sample.shape[1])
    y_prompt, np_ = _run_group(x_prompt, pos_prompt, p, mem=mem_prompt)
    y_sample, ns_ = _run_group(x_sample, pos_sample, p, st=st)
    return (y_prompt, y_sample,
            jnp.stack(np_['ret']), jnp.stack(ns_['ret']),
            jnp.stack(np_['fox_k']), jnp.stack(np_['fox_v']), jnp.stack(np_['fox_logf']),
            jnp.stack(ns_['fox_k']), jnp.stack(ns_['fox_v']), jnp.stack(ns_['fox_logf']),
            jnp.stack(np_['s5_re']), jnp.stack(np_['s5_im']),
            jnp.stack(ns_['s5_re']), jnp.stack(ns_['s5_im']),
            jnp.stack(np_['mem_k']), jnp.stack(np_['mem_v']))
```

```python
import functools

import numpy as np
import jax
import jax.numpy as jnp
from jax import lax
from jax.experimental import pallas as pl
from jax.experimental.pallas import tpu as pltpu

F32 = jnp.float32
BF16 = jnp.bfloat16

D_MODEL = 1024
RET_H = 4
RET_D = 128
RET_W = RET_H * RET_D
CHUNK = 64
ROPE_BASE = 10000.0
FOX_H = 8
FOX_D = 64
FOX_W = FOX_H * FOX_D
S5_GROUP = 16
S5_G = D_MODEL // S5_GROUP
S5_P = 64
S5_STATE = S5_G * S5_P
X_H = 4
X_D = D_MODEL // X_H
D_FF = 4 * D_MODEL
EPS = 1e-6
NEG_INF = -1e30

LANES = 128
VMEM_LIMIT = 56 * 1024 * 1024

ROW_TILE = 512
RET_BLOCK = 256
FOX_TQ = 512
FOX_TK = 512
S5_TS = 64
S5_LANE_CHUNK = 512
CUMSUM_CHUNK = 256


def _dot(a, b):
    return jnp.dot(a, b, preferred_element_type=F32)


def _dot_nt(a, b):
    return lax.dot_general(a, b, (((1,), (1,)), ((), ())), preferred_element_type=F32)


def _dot_tn(a, b):
    return lax.dot_general(a, b, (((0,), (0,)), ((), ())), preferred_element_type=F32)


def _rms(x, w):
    return x * lax.rsqrt(jnp.mean(x * x, axis=-1, keepdims=True) + EPS) * w


def _const_spec(shape):
    nd = len(shape)
    return pl.BlockSpec(shape, lambda *_: (0,) * nd, pipeline_mode=pl.Buffered(1))


def _params(sem):
    return pltpu.CompilerParams(dimension_semantics=sem, vmem_limit_bytes=VMEM_LIMIT)


def _inproj_body(x_ref, g_ref, wr_ref, wf_ref, wl_ref, bl_ref, cos_ref, sin_ref,
                 rq_ref, rk_ref, rv_ref, rg_ref, fq_ref, fkb_ref, fvb_ref, fk_ref, fv_ref, lf_ref):
    h = _rms(x_ref[...], g_ref[...]).astype(BF16)
    cos = cos_ref[...]
    sin = sin_ref[...]

    def rotary_to(dst_ref, z, scale):
        for hh in range(RET_H):
            sl = slice(hh * RET_D, (hh + 1) * RET_D)
            zz = z[:, sl]
            r = zz * cos + pltpu.roll(zz, RET_D // 2, 1) * sin
            if scale is not None:
                r = r * scale
            dst_ref[:, sl] = r.astype(BF16)

    rotary_to(rq_ref, _dot(h, wr_ref[:, 0:RET_W]), None)
    rotary_to(rk_ref, _dot(h, wr_ref[:, RET_W:2 * RET_W]), RET_D ** -0.5)
    rv_ref[...] = _dot(h, wr_ref[:, 2 * RET_W:3 * RET_W]).astype(BF16)
    rg_ref[...] = _dot(h, wr_ref[:, 3 * RET_W:4 * RET_W])
    fq_ref[...] = (_dot(h, wf_ref[:, 0:FOX_W]) * (FOX_D ** -0.5)).astype(BF16)
    fk = _dot(h, wf_ref[:, FOX_W:2 * FOX_W])
    fk_ref[...] = fk
    fkb_ref[...] = fk.astype(BF16)
    fv = _dot(h, wf_ref[:, 2 * FOX_W:3 * FOX_W])
    fv_ref[...] = fv
    fvb_ref[...] = fv.astype(BF16)
    zl = _dot(h, wl_ref[...]) + bl_ref[...]
    lf = jnp.minimum(zl, 0.0) - jnp.log1p(jnp.exp(-jnp.abs(zl)))
    lf_ref[...] = lf[:, 0:FOX_H]


def _inproj(x, g, wr, wf, wl, bl, cos_t, sin_t, tm):
    n = x.shape[0]
    nt = cos_t.shape[0] // tm
    row = lambda w: pl.BlockSpec((tm, w), lambda i: (i, 0))
    tab = pl.BlockSpec((tm, LANES), lambda i: (i % nt, 0))
    shp = lambda w, dt: jax.ShapeDtypeStruct((n, w), dt)
    return pl.pallas_call(
        _inproj_body,
        grid=(n // tm,),
        in_specs=[row(D_MODEL), _const_spec(g.shape), _const_spec(wr.shape), _const_spec(wf.shape),
                  _const_spec(wl.shape), _const_spec(bl.shape), tab, tab],
        out_specs=[row(RET_W), row(RET_W), row(RET_W), row(RET_W), row(FOX_W), row(FOX_W), row(FOX_W),
                   row(FOX_W), row(FOX_W), row(FOX_H)],
        out_shape=[shp(RET_W, BF16), shp(RET_W, BF16), shp(RET_W, BF16), shp(RET_W, F32),
                   shp(FOX_W, BF16), shp(FOX_W, BF16), shp(FOX_W, BF16), shp(FOX_W, F32),
                   shp(FOX_W, F32), shp(FOX_H, F32)],
        compiler_params=_params(("parallel",)),
        name="inproj_ab",
    )(x, g, wr, wf, wl, bl, cos_t, sin_t)


def _cumsum_body(x_ref, u_ref, o_ref):
    rows, total = x_ref.shape
    u = u_ref[...]

    def chunk(i, carry):
        off = pl.multiple_of(i * CUMSUM_CHUNK, CUMSUM_CHUNK)
        xc = x_ref[:, pl.ds(off, CUMSUM_CHUNK)]
        hi = xc.astype(BF16)
        r1 = xc - hi.astype(F32)
        mid = r1.astype(BF16)
        lo = (r1 - mid.astype(F32)).astype(BF16)
        y = _dot(hi, u) + _dot(mid, u) + _dot(lo, u) + carry
        o_ref[:, pl.ds(off, CUMSUM_CHUNK)] = y
        return y[:, CUMSUM_CHUNK - 1:CUMSUM_CHUNK]

    lax.fori_loop(0, total // CUMSUM_CHUNK, chunk, jnp.zeros((rows, 1), F32))


def _cumsum_lanes(x):
    idx = np.arange(CUMSUM_CHUNK)
    u = jnp.asarray(idx[:, None] <= idx[None, :], BF16)
    return pl.pallas_call(
        _cumsum_body,
        out_shape=jax.ShapeDtypeStruct(x.shape, F32),
        compiler_params=pltpu.CompilerParams(vmem_limit_bytes=VMEM_LIMIT),
        name="logf_cumsum",
    )(x, u)


def _ret_body(q_ref, k_ref, v_ref, rg_ref, dm_ref, wq_ref, wk_ref, dec_ref, gn_ref, s0_ref,
              ro_ref, s_ref):
    @pl.when(pl.program_id(1) == 0)
    def _():
        s_ref[...] = s0_ref[...]

    for hh in range(RET_H):
        sl = slice(hh * RET_D, (hh + 1) * RET_D)
        q = q_ref[:, sl]
        k = k_ref[:, sl]
        v = v_ref[:, sl]
        state = s_ref[0, hh]
        scores = _dot_nt(q, k) * dm_ref[hh]
        o = _dot(scores.astype(BF16), v) + _dot(q, state.astype(BF16)) * wq_ref[hh]
        kw = (k.astype(F32) * wk_ref[hh]).astype(BF16)
        s_ref[0, hh] = state * dec_ref[hh] + _dot_tn(kw, v)
        mu = jnp.mean(o, axis=-1, keepdims=True)
        oc = o - mu
        var = jnp.mean(oc * oc, axis=-1, keepdims=True)
        g = rg_ref[:, sl]
        ro_ref[:, sl] = (oc * lax.rsqrt(var + EPS) * gn_ref[hh] * (g * jax.nn.sigmoid(g))).astype(BF16)


def _retention(rq, rk, rv, rg, gn_w, s0, batch, seq, chunk):
    tb = min(RET_BLOCK, seq)
    nb = seq // tb
    lg = np.log(1.0 - 2.0 ** (-5.0 - np.arange(RET_H, dtype=np.float64)))
    idx = np.arange(tb, dtype=np.float64)
    visible = (idx[None, :] // chunk) <= (idx[:, None] // chunk)
    dm = np.exp(lg[:, None, None] * np.abs(idx[:, None] - idx[None, :])) * visible[None]
    wq = np.broadcast_to(np.exp(lg[:, None] * (idx[None, :] + 1.0))[:, :, None], (RET_H, tb, RET_D))
    wk = np.broadcast_to(np.exp(lg[:, None] * (tb - 1.0 - idx[None, :]))[:, :, None], (RET_H, tb, RET_D))
    dec = np.broadcast_to(np.exp(lg * tb)[:, None, None], (RET_H, 1, RET_D))
    consts = [jnp.asarray(a, F32) for a in (dm, wq, wk, dec)]
    gn = gn_w.reshape(RET_H, 1, RET_D)
    row = pl.BlockSpec((tb, RET_W), lambda b, t: (b * nb + t, 0))
    st = pl.BlockSpec((1, RET_H, RET_D, RET_D), lambda b, t: (b, 0, 0, 0))
    return pl.pallas_call(
        _ret_body,
        grid=(batch, nb),
        in_specs=[row, row, row, row] + [_const_spec(c.shape) for c in consts] + [_const_spec(gn.shape), st],
        out_specs=[row, st],
        out_shape=[jax.ShapeDtypeStruct((batch * seq, RET_W), BF16),
                   jax.ShapeDtypeStruct((batch, RET_H, RET_D, RET_D), F32)],
        compiler_params=_params(("parallel", "arbitrary")),
        name="retention",
    )(rq, rk, rv, rg, *consts, gn, s0)


def _fox_body(q_ref, k_ref, v_ref, c_ref, o_ref, m_scr, l_scr, acc_scr, *, tq, tk, q_off):
    q_start = q_off + pl.program_id(2) * tq
    n_full = q_start // tk
    q = q_ref[...]
    lane = lax.broadcasted_iota(jnp.int32, q.shape, 1)
    results = []
    for hh in range(2):
        qm = jnp.where((lane >= FOX_D) == bool(hh), q, jnp.zeros_like(q))
        m_scr[...] = jnp.full(m_scr.shape, -jnp.inf, F32)
        l_scr[...] = jnp.zeros(l_scr.shape, F32)
        acc_scr[...] = jnp.zeros(acc_scr.shape, F32)

        def tile(kj, masked):
            off = pl.multiple_of(kj * tk, tk)
            s = _dot_nt(qm, k_ref[0, pl.ds(off, tk), :]) - c_ref[0, 0, hh:hh + 1, pl.ds(off, tk)]
            if masked:
                qpos = q_start + lax.broadcasted_iota(jnp.int32, s.shape, 0)
                kpos = off + lax.broadcasted_iota(jnp.int32, s.shape, 1)
                s = jnp.where(kpos <= qpos, s, NEG_INF)
            m_prev = m_scr[...]
            m_new = jnp.maximum(m_prev, jnp.max(s, axis=1, keepdims=True))
            alpha = jnp.exp(m_prev - m_new)
            p = jnp.exp(s - m_new)
            l_scr[...] = alpha * l_scr[...] + jnp.sum(p, axis=1, keepdims=True)
            acc_scr[...] = alpha * acc_scr[...] + _dot(p.astype(BF16), v_ref[0, pl.ds(off, tk), :])
            m_scr[...] = m_new

        def full_tile(kj, carry):
            tile(kj, False)
            return carry

        lax.fori_loop(0, n_full, full_tile, 0)
        tile(n_full, True)
        results.append(acc_scr[...] / l_scr[...])
    o_ref[...] = jnp.where(lane < FOX_D, results[0], results[1]).astype(BF16)


def _fox(fq, k_all, v_all, c_all, batch, lq, q_off, tq, tk):
    lk = k_all.shape[1]
    nq = lq // tq
    body = functools.partial(_fox_body, tq=tq, tk=tk, q_off=q_off)
    return pl.pallas_call(
        body,
        grid=(batch, FOX_H // 2, nq),
        in_specs=[pl.BlockSpec((tq, LANES), lambda b, h, i: (b * nq + i, h)),
                  pl.BlockSpec((1, lk, LANES), lambda b, h, i: (b, 0, h)),
                  pl.BlockSpec((1, lk, LANES), lambda b, h, i: (b, 0, h)),
                  pl.BlockSpec((1, 1, 2, lk), lambda b, h, i: (b, h, 0, 0))],
        out_specs=pl.BlockSpec((tq, LANES), lambda b, h, i: (b * nq + i, h)),
        out_shape=jax.ShapeDtypeStruct((batch * lq, FOX_W), BF16),
        scratch_shapes=[pltpu.VMEM((tq, 1), F32), pltpu.VMEM((tq, 1), F32), pltpu.VMEM((tq, LANES), F32)],
        compiler_params=_params(("parallel", "parallel", "arbitrary")),
        name="fox_attention",
    )(fq, k_all, v_all, c_all)


def _normproj_body(x_ref, g_ref, w_ref, o_ref, ob_ref):
    y = _dot(_rms(x_ref[...], g_ref[...]).astype(BF16), w_ref[...])
    o_ref[...] = y
    ob_ref[...] = y.astype(BF16)


def _normproj(x, g, w, tm):
    n, wout = x.shape[0], w.shape[1]
    return pl.pallas_call(
        _normproj_body,
        grid=(n // tm,),
        in_specs=[pl.BlockSpec((tm, D_MODEL), lambda i: (i, 0)), _const_spec(g.shape), _const_spec(w.shape)],
        out_specs=[pl.BlockSpec((tm, wout), lambda i: (i, 0))] * 2,
        out_shape=[jax.ShapeDtypeStruct((n, wout), F32), jax.ShapeDtypeStruct((n, wout), BF16)],
        compiler_params=_params(("parallel",)),
        name="memory_kv",
    )(x, g, w)


def _post_body(*refs, has_mix):
    if has_mix:
        (x_ref, ro_ref, fo_ref, wo_ref, nw_ref, wxq_ref, wxo_ref, mk_ref, mv_ref, wup_ref, wdn_ref,
         o_ref) = refs
    else:
        x_ref, nw_ref, wxq_ref, wxo_ref, mk_ref, mv_ref, wup_ref, wdn_ref, o_ref = refs
    x = x_ref[...]
    if has_mix:
        mix = _dot(ro_ref[...], wo_ref[0:RET_W, :]) + _dot(fo_ref[...], wo_ref[RET_W:RET_W + FOX_W, :])
        x = x + _rms(mix, nw_ref[1:2, :])

    h = _rms(x, nw_ref[2:3, :]).astype(BF16)
    q = _dot(h, wxq_ref[...])
    att = None
    for hh in range(X_H):
        sl = slice(hh * X_D, (hh + 1) * X_D)
        s = _dot_nt(q[:, sl].astype(BF16), mk_ref[:, sl]) * (X_D ** -0.5)
        p = jnp.exp(s - jnp.max(s, axis=-1, keepdims=True))
        o = _dot(p.astype(BF16), mv_ref[:, sl]) / jnp.sum(p, axis=-1, keepdims=True)
        part = _dot(o.astype(BF16), wxo_ref[sl, :])
        att = part if att is None else att + part
    x = x + _rms(att, nw_ref[3:4, :])

    h = _rms(x, nw_ref[4:5, :]).astype(BF16)
    acc = None
    for c in range(D_FF // D_MODEL):
        sl = slice(c * D_MODEL, (c + 1) * D_MODEL)
        r = jnp.maximum(_dot(h, wup_ref[:, sl]), 0.0)
        part = _dot((r * r).astype(BF16), wdn_ref[sl, :])
        acc = part if acc is None else acc + part
    o_ref[...] = x + _rms(acc, nw_ref[5:6, :])


def _post(x, mix_in, nw, wxq, wxo, mk, mv, wup, wdn, seq, tm):
    n = x.shape[0]
    mem_len = mk.shape[0] // (n // seq)
    per_batch = seq // tm
    row = lambda w: pl.BlockSpec((tm, w), lambda i: (i, 0))
    mem = pl.BlockSpec((mem_len, D_MODEL), lambda i: (i // per_batch, 0))
    args = [x]
    specs = [row(D_MODEL)]
    if mix_in is not None:
        ro, fo, wo = mix_in
        args += [ro, fo, wo]
        specs += [row(RET_W), row(FOX_W), _const_spec(wo.shape)]
    args += [nw, wxq, wxo, mk, mv, wup, wdn]
    specs += [_const_spec(nw.shape), _const_spec(wxq.shape), _const_spec(wxo.shape), mem, mem,
              _const_spec(wup.shape), _const_spec(wdn.shape)]
    return pl.pallas_call(
        functools.partial(_post_body, has_mix=mix_in is not None),
        grid=(n // tm,),
        in_specs=specs,
        out_specs=row(D_MODEL),
        out_shape=jax.ShapeDtypeStruct((n, D_MODEL), F32),
        compiler_params=_params(("parallel",)),
        name="xattn_mlp_mix" if mix_in is not None else "xattn_mlp",
    )(*args)


def _s5prep_body(lre_ref, lim_ref, ldt_ref, lre_rep_ref, lim_rep_ref, bre_ref, bim_ref,
                 are_ref, aim_ref, bbre_ref, bbim_ref):
    dt = jnp.exp(ldt_ref[...])

    def zoh(lre, lim):
        mag = jnp.exp(lre * dt)
        a_re = mag * jnp.cos(lim * dt)
        a_im = mag * jnp.sin(lim * dt)
        den = lre * lre + lim * lim
        num_re = a_re - 1.0
        f_re = (num_re * lre + a_im * lim) / den
        f_im = (a_im * lre - num_re * lim) / den
        return a_re, a_im, f_re, f_im

    a_re, a_im, _, _ = zoh(lre_ref[...], lim_ref[...])
    are_ref[...] = a_re
    aim_ref[...] = a_im
    _, _, f_re, f_im = zoh(lre_rep_ref[...], lim_rep_ref[...])
    b_re = bre_ref[...]
    b_im = bim_ref[...]
    bbre_ref[...] = f_re * b_re - f_im * b_im
    bbim_ref[...] = f_re * b_im + f_im * b_re


def _s5_discretize(lam_re, lam_im, log_dt, b_re, b_im):
    rep = lambda a: jnp.repeat(a, S5_GROUP, axis=1)
    wide = jax.ShapeDtypeStruct((S5_G, S5_P * S5_GROUP), F32)
    small = jax.ShapeDtypeStruct((S5_G, S5_P), F32)
    a_re, a_im, bb_re, bb_im = pl.pallas_call(
        _s5prep_body, out_shape=[small, small, wide, wide], name="s5_discretize",
    )(lam_re, lam_im, log_dt.reshape(S5_G, 1), rep(lam_re), rep(lam_im),
      b_re.reshape(S5_G, -1), b_im.reshape(S5_G, -1))
    a = jnp.stack([a_re.reshape(-1), a_im.reshape(-1)])
    gpc = S5_LANE_CHUNK // S5_P
    nch = S5_G // gpc
    eye = jnp.eye(gpc, dtype=F32)

    def in_blocks(bb):
        t = bb.reshape(nch, gpc, S5_P, S5_GROUP).transpose(0, 1, 3, 2)
        return (t[:, :, :, None, :] * eye[None, :, None, :, None]).reshape(nch, gpc * S5_GROUP, gpc * S5_P)

    bmat = jnp.concatenate([in_blocks(bb_re), in_blocks(bb_im)], axis=-1).astype(BF16)
    return a, bmat


def _s5_out_matrices(c_re, c_im):
    gpc = S5_LANE_CHUNK // S5_P
    nch = S5_G // gpc
    eye = jnp.eye(gpc, dtype=F32)

    def out_blocks(c):
        t = c.reshape(nch, gpc, S5_GROUP, S5_P).transpose(0, 1, 3, 2)
        return (t[:, :, :, None, :] * eye[None, :, None, :, None]).reshape(nch, gpc * S5_P, gpc * S5_GROUP)

    return jnp.concatenate([out_blocks(c_re), -out_blocks(c_im)], axis=1).astype(BF16)


def _s5_body(x_ref, nw_ref, win_ref, bmat_ref, a_ref, cmat_ref, d_ref, wglu_ref, st0_ref,
             o_ref, st_ref, u_scr, bu_scr, y_scr, *, ts):
    nb = x_ref.shape[0]
    rows = nb * ts
    lc = S5_LANE_CHUNK
    gw = (lc // S5_P) * S5_GROUP

    @pl.when(pl.program_id(0) == 0)
    def _():
        st_ref[...] = st0_ref[...]

    x = x_ref[...].reshape(rows, D_MODEL)
    u_scr[...] = _dot(_rms(x, nw_ref[0:1, :]).astype(BF16), win_ref[...])
    nv = lc // LANES
    for j in range(S5_STATE // lc):
        uj = u_scr[:, j * gw:(j + 1) * gw]
        bu = _dot(uj.astype(BF16), bmat_ref[j])
        for c in range(2 * nv):
            bu_scr[c] = bu[:, c * LANES:(c + 1) * LANES]
        a_re = [jnp.broadcast_to(a_ref[0:1, j * lc + c * LANES:j * lc + (c + 1) * LANES], (nb, LANES))
                for c in range(nv)]
        a_im = [jnp.broadcast_to(a_ref[1:2, j * lc + c * LANES:j * lc + (c + 1) * LANES], (nb, LANES))
                for c in range(nv)]

        def step(t, carry):
            idx = pl.ds(t, nb, stride=ts)
            new = []
            for c in range(nv):
                s_re, s_im = carry[c], carry[nv + c]
                new.append(a_re[c] * s_re - a_im[c] * s_im + bu_scr[c, idx, :])
            for c in range(nv):
                s_re, s_im = carry[c], carry[nv + c]
                new.append(a_re[c] * s_im + a_im[c] * s_re + bu_scr[nv + c, idx, :])
            for c in range(2 * nv):
                bu_scr[c, idx, :] = new[c]
            return tuple(new)

        init = tuple(st_ref[:, j * lc + c * LANES:j * lc + (c + 1) * LANES] for c in range(nv)) + tuple(
            st_ref[:, S5_STATE + j * lc + c * LANES:S5_STATE + j * lc + (c + 1) * LANES] for c in range(nv))
        fin = lax.fori_loop(0, ts, step, init)
        for c in range(nv):
            st_ref[:, j * lc + c * LANES:j * lc + (c + 1) * LANES] = fin[c]
            st_ref[:, S5_STATE + j * lc + c * LANES:S5_STATE + j * lc + (c + 1) * LANES] = fin[nv + c]
        xs = jnp.concatenate([bu_scr[c] for c in range(2 * nv)], axis=1).astype(BF16)
        y_scr[:, j * gw:(j + 1) * gw] = _dot(xs, cmat_ref[j]) + d_ref[0:1, j * gw:(j + 1) * gw] * uj
    g = _dot(y_scr[...].astype(BF16), wglu_ref[...])
    out = g[:, 0:D_MODEL] * jax.nn.sigmoid(g[:, D_MODEL:2 * D_MODEL])
    o_ref[...] = (x + _rms(out, nw_ref[1:2, :])).reshape(nb, ts, D_MODEL)


def _s5_layer(x3, nw, win, bmat, a, cmat, d, wglu, st0, ts):
    nb, seq, _ = x3.shape
    rows = nb * ts
    blk = pl.BlockSpec((nb, ts, D_MODEL), lambda t: (0, t, 0))
    consts = [nw, win, bmat, a, cmat, d, wglu, st0]
    return pl.pallas_call(
        functools.partial(_s5_body, ts=ts),
        grid=(seq // ts,),
        in_specs=[blk] + [_const_spec(c.shape) for c in consts],
        out_specs=[blk, pl.BlockSpec(st0.shape, lambda t: (0, 0))],
        out_shape=[jax.ShapeDtypeStruct(x3.shape, F32), jax.ShapeDtypeStruct(st0.shape, F32)],
        scratch_shapes=[pltpu.VMEM((rows, D_MODEL), F32),
                        pltpu.VMEM((2 * S5_LANE_CHUNK // LANES, rows, LANES), F32),
                        pltpu.VMEM((rows, D_MODEL), F32)],
        compiler_params=_params(("arbitrary",)),
        name="s5_layer",
    )(x3, *consts)


def _rotary_tables(pos, rows):
    half = RET_D // 2
    inv = ROPE_BASE ** (-jnp.arange(half, dtype=F32) / half)
    ang = pos.astype(F32)[:, None] * inv[None, :]
    cos = jnp.cos(ang)
    sin = jnp.sin(ang)
    cos_t = jnp.concatenate([cos, cos], axis=1)
    sin_t = jnp.concatenate([-sin, sin], axis=1)
    reps = max(1, rows // pos.shape[0])
    return jnp.tile(cos_t, (reps, 1)), jnp.tile(sin_t, (reps, 1))


def _prepare_weights(norm_w, w_in_ab, b_fox_f, w_out_ab, w_in_c, s5_lambda_re, s5_lambda_im, s5_log_dt,
                     s5_b_re, s5_b_im, s5_c_re, s5_c_im, s5_d, w_glu, w_xq, w_xo, w_up, w_down):
    w = {}
    wab = w_in_ab[0].astype(BF16)
    w["wr"] = wab[:, 0:4 * RET_W]
    w["wf"] = wab[:, 4 * RET_W:4 * RET_W + 3 * FOX_W]
    w["wl"] = jnp.pad(wab[:, 4 * RET_W + 3 * FOX_W:], ((0, 0), (0, LANES - FOX_H)))
    w["bl"] = jnp.pad(b_fox_f[0].astype(F32), (0, LANES - FOX_H)).reshape(1, LANES)
    w["wo"] = w_out_ab[0].astype(BF16)
    w["nw"] = norm_w.astype(F32)
    w["win_c"] = w_in_c[0].astype(BF16)
    w["a"], w["bmat"] = _s5_discretize(s5_lambda_re[0], s5_lambda_im[0], s5_log_dt[0], s5_b_re[0], s5_b_im[0])
    w["cmat"] = _s5_out_matrices(s5_c_re[0].astype(F32), s5_c_im[0].astype(F32))
    w["d"] = s5_d[0].astype(F32).reshape(1, D_MODEL)
    w["wglu"] = w_glu[0].astype(BF16)
    for name, arr in (("wxq", w_xq), ("wxo", w_xo), ("wup", w_up), ("wdn", w_down)):
        w[name] = arr.astype(BF16)
    return w


def _run_group(x3, pos, w, gn_w, mem_kv, ret_s0, fox_past, s5_st0):
    batch, seq, _ = x3.shape
    n = batch * seq
    tm = min(ROW_TILE, n)
    tm_b = min(ROW_TILE, seq)
    x = x3.reshape(n, D_MODEL)

    cos_t, sin_t = _rotary_tables(pos, tm)
    rq, rk, rv, rg, fq, fkb, fvb, fk, fv, lf = _inproj(
        x, w["nw"][0, 0:1], w["wr"], w["wf"], w["wl"], w["bl"], cos_t, sin_t, tm)

    lf3 = lf.reshape(batch, seq, FOX_H)
    k_all = fkb.reshape(batch, seq, FOX_W)
    v_all = fvb.reshape(batch, seq, FOX_W)
    q_off = 0
    if fox_past is not None:
        k_past, v_past, lf_past = fox_past
        q_off = k_past.shape[1]
        lf_all = jnp.concatenate([lf_past.astype(F32), lf3], axis=1)
        k_all = jnp.concatenate([k_past.reshape(batch, q_off, FOX_W).astype(BF16), k_all], axis=1)
        v_all = jnp.concatenate([v_past.reshape(batch, q_off, FOX_W).astype(BF16), v_all], axis=1)
    else:
        lf_all = lf3
    tq = min(FOX_TQ, seq)
    lk = lf_all.shape[1]
    lk_pad = -(-lk // FOX_TK) * FOX_TK
    pad = lk_pad - lk
    lf_rows = jnp.pad(jnp.transpose(lf_all, (0, 2, 1)).reshape(batch * FOX_H, lk), ((0, 0), (0, pad)))
    c_all = _cumsum_lanes(lf_rows).reshape(batch, FOX_H // 2, 2, lk_pad)
    if pad:
        k_all = jnp.pad(k_all, ((0, 0), (0, pad), (0, 0)))
        v_all = jnp.pad(v_all, ((0, 0), (0, pad), (0, 0)))
    fo = _fox(fq, k_all, v_all, c_all, batch, seq, q_off, tq, FOX_TK)

    chunk = CHUNK if fox_past is None else seq
    ro, ret_state = _retention(rq, rk, rv, rg, gn_w, ret_s0, batch, seq, chunk)

    x = _post(x, (ro, fo, w["wo"]), w["nw"][0], w["wxq"][0], w["wxo"][0], mem_kv[0][0], mem_kv[0][1],
              w["wup"][0], w["wdn"][0], seq, tm_b)

    ts = min(S5_TS, seq)
    x3n, s5_state = _s5_layer(x.reshape(batch, seq, D_MODEL), w["nw"][1], w["win_c"], w["bmat"], w["a"],
                              w["cmat"], w["d"], w["wglu"], s5_st0, ts)
    x = _post(x3n.reshape(n, D_MODEL), None, w["nw"][1], w["wxq"][1], w["wxo"][1], mem_kv[1][0], mem_kv[1][1],
              w["wup"][1], w["wdn"][1], seq, tm_b)

    new = dict(ret=ret_state[None], fox_k=fk.reshape(1, batch, seq, FOX_H, FOX_D),
               fox_v=fv.reshape(1, batch, seq, FOX_H, FOX_D), fox_logf=lf3[None],
               s5_re=s5_state[:, 0:S5_STATE].reshape(1, batch, S5_G, S5_P),
               s5_im=s5_state[:, S5_STATE:].reshape(1, batch, S5_G, S5_P))
    return x.reshape(batch, seq, D_MODEL), new


def kernel(x_prompt, x_sample, cache_ret_state, cache_fox_k, cache_fox_v, cache_fox_logf, state_s5_re, state_s5_im, cache_mem_k, cache_mem_v, mem_prompt, norm_w, w_in_ab, b_fox_f, ret_gn_w, w_out_ab, w_in_c, s5_lambda_re, s5_lambda_im, s5_log_dt, s5_b_re, s5_b_im, s5_c_re, s5_c_im, s5_d, w_glu, mem_norm_w, w_xq, w_xk, w_xv, w_xo, w_up, w_down):
    w = _prepare_weights(norm_w, w_in_ab, b_fox_f, w_out_ab, w_in_c, s5_lambda_re, s5_lambda_im, s5_log_dt,
                         s5_b_re, s5_b_im, s5_c_re, s5_c_im, s5_d, w_glu, w_xq, w_xo, w_up, w_down)
    batch, seq, _ = x_prompt.shape
    dec_batch, dec_seq, _ = x_sample.shape
    mem_len = mem_prompt.shape[1]
    depth = norm_w.shape[0]
    gn_w = ret_gn_w[0].astype(F32)

    mem_rows = mem_prompt.reshape(batch * mem_len, D_MODEL)
    mem_kv_prompt = []
    mem_k_out = []
    mem_v_out = []
    for layer in range(depth):
        wkv = jnp.concatenate([w_xk[layer], w_xv[layer]], axis=1).astype(BF16)
        kv, kvb = _normproj(mem_rows, mem_norm_w[layer].astype(F32).reshape(1, D_MODEL), wkv,
                            min(ROW_TILE, batch * mem_len))
        mem_kv_prompt.append((kvb[:, 0:D_MODEL], kvb[:, D_MODEL:]))
        mem_k_out.append(kv[:, 0:D_MODEL].reshape(batch, mem_len, X_H, X_D))
        mem_v_out.append(kv[:, D_MODEL:].reshape(batch, mem_len, X_H, X_D))

    pos_prompt = jnp.arange(seq)
    y_prompt, np_ = _run_group(
        x_prompt, pos_prompt, w, gn_w, mem_kv_prompt,
        jnp.zeros((batch, RET_H, RET_D, RET_D), F32), None, jnp.zeros((batch, 2 * S5_STATE), F32))

    past = cache_fox_k.shape[2]
    pos_sample = past + jnp.arange(dec_seq)
    mem_kv_sample = [(cache_mem_k[layer].reshape(dec_batch * mem_len, D_MODEL).astype(BF16),
                      cache_mem_v[layer].reshape(dec_batch * mem_len, D_MODEL).astype(BF16))
                     for layer in range(depth)]
    s5_st0 = jnp.concatenate([state_s5_re[0].reshape(dec_batch, S5_STATE),
                              state_s5_im[0].reshape(dec_batch, S5_STATE)], axis=1).astype(F32)
    y_sample, ns_ = _run_group(
        x_sample, pos_sample, w, gn_w, mem_kv_sample, cache_ret_state[0].astype(F32),
        (cache_fox_k[0], cache_fox_v[0], cache_fox_logf[0]), s5_st0)

    return (y_prompt, y_sample,
            np_["ret"], ns_["ret"],
            np_["fox_k"], np_["fox_v"], np_["fox_logf"],
            ns_["fox_k"], ns_["fox_v"], ns_["fox_logf"],
            np_["s5_re"], np_["s5_im"], ns_["s5_re"], ns_["s5_im"],
            jnp.stack(mem_k_out), jnp.stack(mem_v_out))
```

```python
import functools

import numpy as np
import jax
import jax.numpy as jnp
from jax import lax
from jax.experimental import pallas as pl
from jax.experimental.pallas import tpu as pltpu

F32 = jnp.float32
BF16 = jnp.bfloat16

D_MODEL = 1024
RET_H = 4
RET_D = 128
RET_W = RET_H * RET_D
CHUNK = 64
ROPE_BASE = 10000.0
FOX_H = 8
FOX_D = 64
FOX_W = FOX_H * FOX_D
S5_GROUP = 16
S5_G = D_MODEL // S5_GROUP
S5_P = 64
S5_STATE = S5_G * S5_P
X_H = 4
X_D = D_MODEL // X_H
D_FF = 4 * D_MODEL
EPS = 1e-6
NEG_INF = -1e30
LOG2E = float(np.log2(np.e))

LANES = 128
VMEM_LIMIT = 56 * 1024 * 1024

ROW_TILE = 512
RET_BLOCK = 256
FOX_TQ = 512
FOX_TK = 512
S5_TS = 64
S5_LANE_CHUNK = 512
CUMSUM_CHUNK = 256


def _dot(a, b):
    return jnp.dot(a, b, preferred_element_type=F32)


def _dot_nt(a, b):
    return lax.dot_general(a, b, (((1,), (1,)), ((), ())), preferred_element_type=F32)


def _dot_tn(a, b):
    return lax.dot_general(a, b, (((0,), (0,)), ((), ())), preferred_element_type=F32)


def _rms(x, w):
    return x * lax.rsqrt(jnp.mean(x * x, axis=-1, keepdims=True) + EPS) * w


def _const_spec(shape):
    nd = len(shape)
    return pl.BlockSpec(shape, lambda *_: (0,) * nd, pipeline_mode=pl.Buffered(1))


def _params(sem):
    return pltpu.CompilerParams(dimension_semantics=sem, vmem_limit_bytes=VMEM_LIMIT)


def _inproj_body(x_ref, g_ref, wr_ref, wf_ref, wl_ref, bl_ref, cos_ref, sin_ref,
                 rq_ref, rk_ref, rv_ref, rg_ref, fq_ref, fkb_ref, fvp_ref, fk_ref, fv_ref, lf_ref):
    h = _rms(x_ref[...], g_ref[...]).astype(BF16)
    cos = cos_ref[...]
    sin = sin_ref[...]

    def rotary_to(dst_ref, z, scale):
        for hh in range(RET_H):
            sl = slice(hh * RET_D, (hh + 1) * RET_D)
            zz = z[:, sl]
            r = zz * cos + pltpu.roll(zz, RET_D // 2, 1) * sin
            if scale is not None:
                r = r * scale
            dst_ref[:, sl] = r.astype(BF16)

    rotary_to(rq_ref, _dot(h, wr_ref[:, 0:RET_W]), None)
    rotary_to(rk_ref, _dot(h, wr_ref[:, RET_W:2 * RET_W]), RET_D ** -0.5)
    rv_ref[...] = _dot(h, wr_ref[:, 2 * RET_W:3 * RET_W]).astype(BF16)
    rg_ref[...] = _dot(h, wr_ref[:, 3 * RET_W:4 * RET_W])
    fq_ref[...] = (_dot(h, wf_ref[:, 0:FOX_W]) * (LOG2E * FOX_D ** -0.5)).astype(BF16)
    fk = _dot(h, wf_ref[:, FOX_W:2 * FOX_W])
    fk_ref[...] = fk
    fkb_ref[...] = fk.astype(BF16)
    fv = _dot(h, wf_ref[:, 2 * FOX_W:3 * FOX_W])
    fv_ref[...] = fv
    lane = lax.broadcasted_iota(jnp.int32, (fv.shape[0], LANES), 1)
    for pr in range(FOX_H // 2):
        vp = fv[:, pr * LANES:(pr + 1) * LANES]
        fvp_ref[:, (2 * pr) * LANES:(2 * pr + 1) * LANES] = jnp.where(lane < FOX_D, vp, 1.0).astype(BF16)
        fvp_ref[:, (2 * pr + 1) * LANES:(2 * pr + 2) * LANES] = jnp.where(lane >= FOX_D, vp, 1.0).astype(BF16)
    zl = _dot(h, wl_ref[...]) + bl_ref[...]
    lf = jnp.minimum(zl, 0.0) - jnp.log1p(jnp.exp(-jnp.abs(zl)))
    lf_ref[...] = lf[:, 0:FOX_H]


def _inproj(x, g, wr, wf, wl, bl, cos_t, sin_t, tm):
    n = x.shape[0]
    nt = cos_t.shape[0] // tm
    row = lambda w: pl.BlockSpec((tm, w), lambda i: (i, 0))
    tab = pl.BlockSpec((tm, LANES), lambda i: (i % nt, 0))
    shp = lambda w, dt: jax.ShapeDtypeStruct((n, w), dt)
    return pl.pallas_call(
        _inproj_body,
        grid=(n // tm,),
        in_specs=[row(D_MODEL), _const_spec(g.shape), _const_spec(wr.shape), _const_spec(wf.shape),
                  _const_spec(wl.shape), _const_spec(bl.shape), tab, tab],
        out_specs=[row(RET_W), row(RET_W), row(RET_W), row(RET_W), row(FOX_W), row(FOX_W),
                   row(FOX_H * LANES), row(FOX_W), row(FOX_W), row(FOX_H)],
        out_shape=[shp(RET_W, BF16), shp(RET_W, BF16), shp(RET_W, BF16), shp(RET_W, F32),
                   shp(FOX_W, BF16), shp(FOX_W, BF16), shp(FOX_H * LANES, BF16), shp(FOX_W, F32),
                   shp(FOX_W, F32), shp(FOX_H, F32)],
        compiler_params=_params(("parallel",)),
        name="inproj_ab",
    )(x, g, wr, wf, wl, bl, cos_t, sin_t)


def _cumsum_body(x_ref, u_ref, o_ref):
    rows, total = x_ref.shape
    u = u_ref[...]

    def chunk(i, carry):
        off = pl.multiple_of(i * CUMSUM_CHUNK, CUMSUM_CHUNK)
        xc = x_ref[:, pl.ds(off, CUMSUM_CHUNK)]
        hi = xc.astype(BF16)
        r1 = xc - hi.astype(F32)
        mid = r1.astype(BF16)
        lo = (r1 - mid.astype(F32)).astype(BF16)
        y = _dot(hi, u) + _dot(mid, u) + _dot(lo, u) + carry
        o_ref[:, pl.ds(off, CUMSUM_CHUNK)] = y * LOG2E
        return y[:, CUMSUM_CHUNK - 1:CUMSUM_CHUNK]

    lax.fori_loop(0, total // CUMSUM_CHUNK, chunk, jnp.zeros((rows, 1), F32))


def _cumsum_lanes(x):
    idx = np.arange(CUMSUM_CHUNK)
    u = jnp.asarray(idx[:, None] <= idx[None, :], BF16)
    return pl.pallas_call(
        _cumsum_body,
        out_shape=jax.ShapeDtypeStruct(x.shape, F32),
        compiler_params=pltpu.CompilerParams(vmem_limit_bytes=VMEM_LIMIT),
        name="logf_cumsum",
    )(x, u)


def _ret_body(q_ref, k_ref, v_ref, rg_ref, dm_ref, wq_ref, wk_ref, dec_ref, gn_ref, s0_ref,
              ro_ref, s_ref):
    @pl.when(pl.program_id(1) == 0)
    def _():
        s_ref[...] = s0_ref[...]

    for hh in range(RET_H):
        sl = slice(hh * RET_D, (hh + 1) * RET_D)
        q = q_ref[:, sl]
        k = k_ref[:, sl]
        v = v_ref[:, sl]
        state = s_ref[0, hh]
        scores = _dot_nt(q, k) * dm_ref[hh]
        o = _dot(scores.astype(BF16), v) + _dot(q, state.astype(BF16)) * wq_ref[hh]
        kw = (k.astype(F32) * wk_ref[hh]).astype(BF16)
        s_ref[0, hh] = state * dec_ref[hh] + _dot_tn(kw, v)
        mu = jnp.mean(o, axis=-1, keepdims=True)
        oc = o - mu
        var = jnp.mean(oc * oc, axis=-1, keepdims=True)
        g = rg_ref[:, sl]
        ro_ref[:, sl] = (oc * lax.rsqrt(var + EPS) * gn_ref[hh] * (g * jax.nn.sigmoid(g))).astype(BF16)


def _retention(rq, rk, rv, rg, gn_w, s0, batch, seq, chunk):
    tb = min(RET_BLOCK, seq)
    nb = seq // tb
    lg = np.log(1.0 - 2.0 ** (-5.0 - np.arange(RET_H, dtype=np.float64)))
    idx = np.arange(tb, dtype=np.float64)
    visible = (idx[None, :] // chunk) <= (idx[:, None] // chunk)
    dm = np.exp(lg[:, None, None] * np.abs(idx[:, None] - idx[None, :])) * visible[None]
    wq = np.broadcast_to(np.exp(lg[:, None] * (idx[None, :] + 1.0))[:, :, None], (RET_H, tb, RET_D))
    wk = np.broadcast_to(np.exp(lg[:, None] * (tb - 1.0 - idx[None, :]))[:, :, None], (RET_H, tb, RET_D))
    dec = np.broadcast_to(np.exp(lg * tb)[:, None, None], (RET_H, 1, RET_D))
    consts = [jnp.asarray(a, F32) for a in (dm, wq, wk, dec)]
    gn = gn_w.reshape(RET_H, 1, RET_D)
    row = pl.BlockSpec((tb, RET_W), lambda b, t: (b * nb + t, 0))
    st = pl.BlockSpec((1, RET_H, RET_D, RET_D), lambda b, t: (b, 0, 0, 0))
    return pl.pallas_call(
        _ret_body,
        grid=(batch, nb),
        in_specs=[row, row, row, row] + [_const_spec(c.shape) for c in consts] + [_const_spec(gn.shape), st],
        out_specs=[row, st],
        out_shape=[jax.ShapeDtypeStruct((batch * seq, RET_W), BF16),
                   jax.ShapeDtypeStruct((batch, RET_H, RET_D, RET_D), F32)],
        compiler_params=_params(("parallel", "arbitrary")),
        name="retention",
    )(rq, rk, rv, rg, *consts, gn, s0)


def _fox_body(q_ref, k_ref, v_ref, c_ref, o_ref, qs_scr, m_scr, acc_scr, *, tq, tk, q_off):
    q_start = q_off + pl.program_id(2) * tq
    n_full = q_start // tk
    q = q_ref[...]
    lane = lax.broadcasted_iota(jnp.int32, q.shape, 1)
    zero = jnp.zeros_like(q)
    qs_scr[0:tq, :] = jnp.where(lane < FOX_D, q, zero)
    qs_scr[tq:2 * tq, :] = jnp.where(lane >= FOX_D, q, zero)
    m_scr[...] = jnp.full(m_scr.shape, -jnp.inf, F32)
    acc_scr[...] = jnp.zeros(acc_scr.shape, F32)

    def tile(kj, masked):
        off = pl.multiple_of(kj * tk, tk)
        s = _dot_nt(qs_scr[...], k_ref[0, pl.ds(off, tk), :])
        for hh in range(2):
            rows = slice(hh * tq, (hh + 1) * tq)
            sh = s[rows, :] - c_ref[0, 0, hh:hh + 1, pl.ds(off, tk)]
            if masked:
                qpos = q_start + lax.broadcasted_iota(jnp.int32, sh.shape, 0)
                kpos = off + lax.broadcasted_iota(jnp.int32, sh.shape, 1)
                sh = jnp.where(kpos <= qpos, sh, NEG_INF)
            chunks = [sh[:, c * LANES:(c + 1) * LANES] for c in range(tk // LANES)]
            m_prev = m_scr[rows, :]
            m_new = jnp.maximum(m_prev, jnp.max(functools.reduce(jnp.maximum, chunks), axis=1, keepdims=True))
            p = jnp.concatenate([jnp.exp2(ch - m_new) for ch in chunks], axis=1).astype(BF16)
            pv = _dot(p, v_ref[0, pl.ds(off, tk), hh * LANES:(hh + 1) * LANES])
            acc_scr[rows, :] = jnp.exp2(m_prev - m_new) * acc_scr[rows, :] + pv
            m_scr[rows, :] = m_new

    def full_tile(kj, carry):
        tile(kj, False)
        return carry

    lax.fori_loop(0, n_full, full_tile, 0)
    tile(n_full, True)
    acc_a = acc_scr[0:tq, :]
    acc_b = acc_scr[tq:2 * tq, :]
    half = LANES // 2
    o_ref[...] = jnp.where(lane < FOX_D, acc_a / pltpu.roll(acc_a, half, 1),
                           acc_b / pltpu.roll(acc_b, half, 1)).astype(BF16)


def _fox(fq, k_all, v_all, c_all, batch, lq, q_off, tq, tk):
    lk = k_all.shape[1]
    nq = lq // tq
    body = functools.partial(_fox_body, tq=tq, tk=tk, q_off=q_off)
    return pl.pallas_call(
        body,
        grid=(batch, FOX_H // 2, nq),
        in_specs=[pl.BlockSpec((tq, LANES), lambda b, h, i: (b * nq + i, h)),
                  pl.BlockSpec((1, lk, LANES), lambda b, h, i: (b, 0, h)),
                  pl.BlockSpec((1, lk, 2 * LANES), lambda b, h, i: (b, 0, h)),
                  pl.BlockSpec((1, 1, 2, lk), lambda b, h, i: (b, h, 0, 0))],
        out_specs=pl.BlockSpec((tq, LANES), lambda b, h, i: (b * nq + i, h)),
        out_shape=jax.ShapeDtypeStruct((batch * lq, FOX_W), BF16),
        scratch_shapes=[pltpu.VMEM((2 * tq, LANES), BF16), pltpu.VMEM((2 * tq, LANES), F32),
                        pltpu.VMEM((2 * tq, LANES), F32)],
        compiler_params=_params(("parallel", "parallel", "arbitrary")),
        name="fox_attention",
    )(fq, k_all, v_all, c_all)


def _normproj_body(x_ref, g_ref, w_ref, o_ref, ob_ref):
    y = _dot(_rms(x_ref[...], g_ref[...]).astype(BF16), w_ref[...])
    o_ref[...] = y
    ob_ref[...] = y.astype(BF16)


def _normproj(x, g, w, tm):
    n, wout = x.shape[0], w.shape[1]
    return pl.pallas_call(
        _normproj_body,
        grid=(n // tm,),
        in_specs=[pl.BlockSpec((tm, D_MODEL), lambda i: (i, 0)), _const_spec(g.shape), _const_spec(w.shape)],
        out_specs=[pl.BlockSpec((tm, wout), lambda i: (i, 0))] * 2,
        out_shape=[jax.ShapeDtypeStruct((n, wout), F32), jax.ShapeDtypeStruct((n, wout), BF16)],
        compiler_params=_params(("parallel",)),
        name="memory_kv",
    )(x, g, w)


def _post_body(*refs, has_mix):
    if has_mix:
        (x_ref, ro_ref, fo_ref, wo_ref, nw_ref, wxq_ref, wxo_ref, mk_ref, mv_ref, wup_ref, wdn_ref,
         o_ref) = refs
    else:
        x_ref, nw_ref, wxq_ref, wxo_ref, mk_ref, mv_ref, wup_ref, wdn_ref, o_ref = refs
    x = x_ref[...]
    if has_mix:
        mix = _dot(ro_ref[...], wo_ref[0:RET_W, :]) + _dot(fo_ref[...], wo_ref[RET_W:RET_W + FOX_W, :])
        x = x + _rms(mix, nw_ref[1:2, :])

    h = _rms(x, nw_ref[2:3, :]).astype(BF16)
    q = _dot(h, wxq_ref[...])
    att = None
    for hh in range(X_H):
        sl = slice(hh * X_D, (hh + 1) * X_D)
        s = _dot_nt(q[:, sl].astype(BF16), mk_ref[:, sl]) * (X_D ** -0.5)
        p = jnp.exp(s - jnp.max(s, axis=-1, keepdims=True))
        o = _dot(p.astype(BF16), mv_ref[:, sl]) / jnp.sum(p, axis=-1, keepdims=True)
        part = _dot(o.astype(BF16), wxo_ref[sl, :])
        att = part if att is None else att + part
    x = x + _rms(att, nw_ref[3:4, :])

    h = _rms(x, nw_ref[4:5, :]).astype(BF16)
    acc = None
    for c in range(D_FF // D_MODEL):
        sl = slice(c * D_MODEL, (c + 1) * D_MODEL)
        r = jnp.maximum(_dot(h, wup_ref[:, sl]), 0.0)
        part = _dot((r * r).astype(BF16), wdn_ref[sl, :])
        acc = part if acc is None else acc + part
    o_ref[...] = x + _rms(acc, nw_ref[5:6, :])


def _post(x, mix_in, nw, wxq, wxo, mk, mv, wup, wdn, seq, tm):
    n = x.shape[0]
    mem_len = mk.shape[0] // (n // seq)
    per_batch = seq // tm
    row = lambda w: pl.BlockSpec((tm, w), lambda i: (i, 0))
    mem = pl.BlockSpec((mem_len, D_MODEL), lambda i: (i // per_batch, 0))
    args = [x]
    specs = [row(D_MODEL)]
    if mix_in is not None:
        ro, fo, wo = mix_in
        args += [ro, fo, wo]
        specs += [row(RET_W), row(FOX_W), _const_spec(wo.shape)]
    args += [nw, wxq, wxo, mk, mv, wup, wdn]
    specs += [_const_spec(nw.shape), _const_spec(wxq.shape), _const_spec(wxo.shape), mem, mem,
              _const_spec(wup.shape), _const_spec(wdn.shape)]
    return pl.pallas_call(
        functools.partial(_post_body, has_mix=mix_in is not None),
        grid=(n // tm,),
        in_specs=specs,
        out_specs=row(D_MODEL),
        out_shape=jax.ShapeDtypeStruct((n, D_MODEL), F32),
        compiler_params=_params(("parallel",)),
        name="xattn_mlp_mix" if mix_in is not None else "xattn_mlp",
    )(*args)


def _s5prep_body(lre_ref, lim_ref, ldt_ref, lre_rep_ref, lim_rep_ref, bre_ref, bim_ref,
                 are_ref, aim_ref, bbre_ref, bbim_ref):
    dt = jnp.exp(ldt_ref[...])

    def zoh(lre, lim):
        mag = jnp.exp(lre * dt)
        a_re = mag * jnp.cos(lim * dt)
        a_im = mag * jnp.sin(lim * dt)
        den = lre * lre + lim * lim
        num_re = a_re - 1.0
        f_re = (num_re * lre + a_im * lim) / den
        f_im = (a_im * lre - num_re * lim) / den
        return a_re, a_im, f_re, f_im

    a_re, a_im, _, _ = zoh(lre_ref[...], lim_ref[...])
    are_ref[...] = a_re
    aim_ref[...] = a_im
    _, _, f_re, f_im = zoh(lre_rep_ref[...], lim_rep_ref[...])
    b_re = bre_ref[...]
    b_im = bim_ref[...]
    bbre_ref[...] = f_re * b_re - f_im * b_im
    bbim_ref[...] = f_re * b_im + f_im * b_re


def _s5_discretize(lam_re, lam_im, log_dt, b_re, b_im):
    rep = lambda a: jnp.repeat(a, S5_GROUP, axis=1)
    wide = jax.ShapeDtypeStruct((S5_G, S5_P * S5_GROUP), F32)
    small = jax.ShapeDtypeStruct((S5_G, S5_P), F32)
    a_re, a_im, bb_re, bb_im = pl.pallas_call(
        _s5prep_body, out_shape=[small, small, wide, wide], name="s5_discretize",
    )(lam_re, lam_im, log_dt.reshape(S5_G, 1), rep(lam_re), rep(lam_im),
      b_re.reshape(S5_G, -1), b_im.reshape(S5_G, -1))
    a = jnp.stack([a_re.reshape(-1), a_im.reshape(-1)])
    gpc = S5_LANE_CHUNK // S5_P
    nch = S5_G // gpc
    eye = jnp.eye(gpc, dtype=F32)

    def in_blocks(bb):
        t = bb.reshape(nch, gpc, S5_P, S5_GROUP).transpose(0, 1, 3, 2)
        return (t[:, :, :, None, :] * eye[None, :, None, :, None]).reshape(nch, gpc * S5_GROUP, gpc * S5_P)

    bmat = jnp.concatenate([in_blocks(bb_re), in_blocks(bb_im)], axis=-1).astype(BF16)
    return a, bmat


def _s5_out_matrices(c_re, c_im):
    gpc = S5_LANE_CHUNK // S5_P
    nch = S5_G // gpc
    eye = jnp.eye(gpc, dtype=F32)

    def out_blocks(c):
        t = c.reshape(nch, gpc, S5_GROUP, S5_P).transpose(0, 1, 3, 2)
        return (t[:, :, :, None, :] * eye[None, :, None, :, None]).reshape(nch, gpc * S5_P, gpc * S5_GROUP)

    return jnp.concatenate([out_blocks(c_re), -out_blocks(c_im)], axis=1).astype(BF16)


def _s5_body(x_ref, nw_ref, perm_ref, win_ref, bmat_ref, a_ref, cmat_ref, d_ref, wglu_ref, st0_ref,
             o_ref, st_ref, u_scr, bu_scr, y_scr, *, ts):
    nb = x_ref.shape[0]
    rows = nb * ts
    lc = S5_LANE_CHUNK
    gw = (lc // S5_P) * S5_GROUP

    @pl.when(pl.program_id(0) == 0)
    def _():
        st_ref[...] = st0_ref[...]

    x = x_ref[...].reshape(rows, D_MODEL)
    h = _dot(perm_ref[0], _rms(x, nw_ref[0:1, :]).astype(BF16)).astype(BF16)
    u_scr[...] = _dot(h, win_ref[...])
    nv = lc // LANES
    for j in range(S5_STATE // lc):
        uj = u_scr[:, j * gw:(j + 1) * gw]
        bu = _dot(uj.astype(BF16), bmat_ref[j])
        for c in range(2 * nv):
            bu_scr[c] = bu[:, c * LANES:(c + 1) * LANES]
        a_re = [jnp.broadcast_to(a_ref[0:1, j * lc + c * LANES:j * lc + (c + 1) * LANES], (nb, LANES))
                for c in range(nv)]
        a_im = [jnp.broadcast_to(a_ref[1:2, j * lc + c * LANES:j * lc + (c + 1) * LANES], (nb, LANES))
                for c in range(nv)]

        def step(t, carry):
            idx = pl.ds(pl.multiple_of(t * nb, nb), nb)
            new = []
            for c in range(nv):
                s_re, s_im = carry[c], carry[nv + c]
                new.append(a_re[c] * s_re - a_im[c] * s_im + bu_scr[c, idx, :])
            for c in range(nv):
                s_re, s_im = carry[c], carry[nv + c]
                new.append(a_re[c] * s_im + a_im[c] * s_re + bu_scr[nv + c, idx, :])
            for c in range(2 * nv):
                bu_scr[c, idx, :] = new[c]
            return tuple(new)

        init = tuple(st_ref[:, j * lc + c * LANES:j * lc + (c + 1) * LANES] for c in range(nv)) + tuple(
            st_ref[:, S5_STATE + j * lc + c * LANES:S5_STATE + j * lc + (c + 1) * LANES] for c in range(nv))
        fin = lax.fori_loop(0, ts, step, init, unroll=8)
        for c in range(nv):
            st_ref[:, j * lc + c * LANES:j * lc + (c + 1) * LANES] = fin[c]
            st_ref[:, S5_STATE + j * lc + c * LANES:S5_STATE + j * lc + (c + 1) * LANES] = fin[nv + c]
        xs = jnp.concatenate([bu_scr[c] for c in range(2 * nv)], axis=1).astype(BF16)
        y_scr[:, j * gw:(j + 1) * gw] = _dot(xs, cmat_ref[j]) + d_ref[0:1, j * gw:(j + 1) * gw] * uj
    y = _dot(perm_ref[1], y_scr[...].astype(BF16)).astype(BF16)
    g = _dot(y, wglu_ref[...])
    out = g[:, 0:D_MODEL] * jax.nn.sigmoid(g[:, D_MODEL:2 * D_MODEL])
    o_ref[...] = (x + _rms(out, nw_ref[1:2, :])).reshape(nb, ts, D_MODEL)


def _s5_layer(x3, nw, win, bmat, a, cmat, d, wglu, st0, ts):
    nb, seq, _ = x3.shape
    rows = nb * ts
    blk = pl.BlockSpec((nb, ts, D_MODEL), lambda t: (0, t, 0))
    r = np.arange(rows)
    fwd = np.zeros((rows, rows), np.float32)
    fwd[(r % ts) * nb + r // ts, r] = 1.0
    perm = jnp.asarray(np.stack([fwd, fwd.T]), BF16)
    consts = [nw, perm, win, bmat, a, cmat, d, wglu, st0]
    return pl.pallas_call(
        functools.partial(_s5_body, ts=ts),
        grid=(seq // ts,),
        in_specs=[blk] + [_const_spec(c.shape) for c in consts],
        out_specs=[blk, pl.BlockSpec(st0.shape, lambda t: (0, 0))],
        out_shape=[jax.ShapeDtypeStruct(x3.shape, F32), jax.ShapeDtypeStruct(st0.shape, F32)],
        scratch_shapes=[pltpu.VMEM((rows, D_MODEL), F32),
                        pltpu.VMEM((2 * S5_LANE_CHUNK // LANES, rows, LANES), F32),
                        pltpu.VMEM((rows, D_MODEL), F32)],
        compiler_params=_params(("arbitrary",)),
        name="s5_layer",
    )(x3, *consts)


def _rotary_tables(pos, rows):
    half = RET_D // 2
    inv = ROPE_BASE ** (-jnp.arange(half, dtype=F32) / half)
    ang = pos.astype(F32)[:, None] * inv[None, :]
    cos = jnp.cos(ang)
    sin = jnp.sin(ang)
    cos_t = jnp.concatenate([cos, cos], axis=1)
    sin_t = jnp.concatenate([-sin, sin], axis=1)
    reps = max(1, rows // pos.shape[0])
    return jnp.tile(cos_t, (reps, 1)), jnp.tile(sin_t, (reps, 1))


def _prepare_weights(norm_w, w_in_ab, b_fox_f, w_out_ab, w_in_c, s5_lambda_re, s5_lambda_im, s5_log_dt,
                     s5_b_re, s5_b_im, s5_c_re, s5_c_im, s5_d, w_glu, w_xq, w_xo, w_up, w_down):
    w = {}
    wab = w_in_ab[0].astype(BF16)
    w["wr"] = wab[:, 0:4 * RET_W]
    w["wf"] = wab[:, 4 * RET_W:4 * RET_W + 3 * FOX_W]
    w["wl"] = jnp.pad(wab[:, 4 * RET_W + 3 * FOX_W:], ((0, 0), (0, LANES - FOX_H)))
    w["bl"] = jnp.pad(b_fox_f[0].astype(F32), (0, LANES - FOX_H)).reshape(1, LANES)
    w["wo"] = w_out_ab[0].astype(BF16)
    w["nw"] = norm_w.astype(F32)
    w["win_c"] = w_in_c[0].astype(BF16)
    w["a"], w["bmat"] = _s5_discretize(s5_lambda_re[0], s5_lambda_im[0], s5_log_dt[0], s5_b_re[0], s5_b_im[0])
    w["cmat"] = _s5_out_matrices(s5_c_re[0].astype(F32), s5_c_im[0].astype(F32))
    w["d"] = s5_d[0].astype(F32).reshape(1, D_MODEL)
    w["wglu"] = w_glu[0].astype(BF16)
    for name, arr in (("wxq", w_xq), ("wxo", w_xo), ("wup", w_up), ("wdn", w_down)):
        w[name] = arr.astype(BF16)
    return w


def _run_group(x3, pos, w, gn_w, mem_kv, ret_s0, fox_past, s5_st0):
    batch, seq, _ = x3.shape
    n = batch * seq
    tm = min(ROW_TILE, n)
    tm_b = min(ROW_TILE, seq)
    x = x3.reshape(n, D_MODEL)

    cos_t, sin_t = _rotary_tables(pos, tm)
    rq, rk, rv, rg, fq, fkb, fvp, fk, fv, lf = _inproj(
        x, w["nw"][0, 0:1], w["wr"], w["wf"], w["wl"], w["bl"], cos_t, sin_t, tm)

    lf3 = lf.reshape(batch, seq, FOX_H)
    k_all = fkb.reshape(batch, seq, FOX_W)
    v_all = fvp.reshape(batch, seq, FOX_H * LANES)
    q_off = 0
    if fox_past is not None:
        k_past, v_past, lf_past = fox_past
        q_off = k_past.shape[1]
        lf_all = jnp.concatenate([lf_past.astype(F32), lf3], axis=1)
        k_all = jnp.concatenate([k_past.reshape(batch, q_off, FOX_W).astype(BF16), k_all], axis=1)
        vp = v_past.reshape(batch, q_off, FOX_H // 2, 2, FOX_D).astype(BF16)
        ones = jnp.ones_like(vp[:, :, :, 0])
        vp = jnp.stack([jnp.concatenate([vp[:, :, :, 0], ones], axis=-1),
                        jnp.concatenate([ones, vp[:, :, :, 1]], axis=-1)], axis=3)
        v_all = jnp.concatenate([vp.reshape(batch, q_off, FOX_H * LANES), v_all], axis=1)
    else:
        lf_all = lf3
    tq = min(FOX_TQ, seq)
    lk = lf_all.shape[1]
    lk_pad = -(-lk // FOX_TK) * FOX_TK
    pad = lk_pad - lk
    lf_rows = jnp.pad(jnp.transpose(lf_all, (0, 2, 1)).reshape(batch * FOX_H, lk), ((0, 0), (0, pad)))
    c_all = _cumsum_lanes(lf_rows).reshape(batch, FOX_H // 2, 2, lk_pad)
    if pad:
        k_all = jnp.pad(k_all, ((0, 0), (0, pad), (0, 0)))
        v_all = jnp.pad(v_all, ((0, 0), (0, pad), (0, 0)))
    fo = _fox(fq, k_all, v_all, c_all, batch, seq, q_off, tq, FOX_TK)

    chunk = CHUNK if fox_past is None else seq
    ro, ret_state = _retention(rq, rk, rv, rg, gn_w, ret_s0, batch, seq, chunk)

    x = _post(x, (ro, fo, w["wo"]), w["nw"][0], w["wxq"][0], w["wxo"][0], mem_kv[0][0], mem_kv[0][1],
              w["wup"][0], w["wdn"][0], seq, tm_b)

    ts = min(S5_TS, seq)
    x3n, s5_state = _s5_layer(x.reshape(batch, seq, D_MODEL), w["nw"][1], w["win_c"], w["bmat"], w["a"],
                              w["cmat"], w["d"], w["wglu"], s5_st0, ts)
    x = _post(x3n.reshape(n, D_MODEL), None, w["nw"][1], w["wxq"][1], w["wxo"][1], mem_kv[1][0], mem_kv[1][1],
              w["wup"][1], w["wdn"][1], seq, tm_b)

    new = dict(ret=ret_state[None], fox_k=fk.reshape(1, batch, seq, FOX_H, FOX_D),
               fox_v=fv.reshape(1, batch, seq, FOX_H, FOX_D), fox_logf=lf3[None],
               s5_re=s5_state[:, 0:S5_STATE].reshape(1, batch, S5_G, S5_P),
               s5_im=s5_state[:, S5_STATE:].reshape(1, batch, S5_G, S5_P))
    return x.reshape(batch, seq, D_MODEL), new


def kernel(x_prompt, x_sample, cache_ret_state, cache_fox_k, cache_fox_v, cache_fox_logf, state_s5_re, state_s5_im, cache_mem_k, cache_mem_v, mem_prompt, norm_w, w_in_ab, b_fox_f, ret_gn_w, w_out_ab, w_in_c, s5_lambda_re, s5_lambda_im, s5_log_dt, s5_b_re, s5_b_im, s5_c_re, s5_c_im, s5_d, w_glu, mem_norm_w, w_xq, w_xk, w_xv, w_xo, w_up, w_down):
    w = _prepare_weights(norm_w, w_in_ab, b_fox_f, w_out_ab, w_in_c, s5_lambda_re, s5_lambda_im, s5_log_dt,
                         s5_b_re, s5_b_im, s5_c_re, s5_c_im, s5_d, w_glu, w_xq, w_xo, w_up, w_down)
    batch, seq, _ = x_prompt.shape
    dec_batch, dec_seq, _ = x_sample.shape
    mem_len = mem_prompt.shape[1]
    depth = norm_w.shape[0]
    gn_w = ret_gn_w[0].astype(F32)

    mem_rows = mem_prompt.reshape(batch * mem_len, D_MODEL)
    mem_kv_prompt = []
    mem_k_out = []
    mem_v_out = []
    for layer in range(depth):
        wkv = jnp.concatenate([w_xk[layer], w_xv[layer]], axis=1).astype(BF16)
        kv, kvb = _normproj(mem_rows, mem_norm_w[layer].astype(F32).reshape(1, D_MODEL), wkv,
                            min(ROW_TILE, batch * mem_len))
        mem_kv_prompt.append((kvb[:, 0:D_MODEL], kvb[:, D_MODEL:]))
        mem_k_out.append(kv[:, 0:D_MODEL].reshape(batch, mem_len, X_H, X_D))
        mem_v_out.append(kv[:, D_MODEL:].reshape(batch, mem_len, X_H, X_D))

    pos_prompt = jnp.arange(seq)
    y_prompt, np_ = _run_group(
        x_prompt, pos_prompt, w, gn_w, mem_kv_prompt,
        jnp.zeros((batch, RET_H, RET_D, RET_D), F32), None, jnp.zeros((batch, 2 * S5_STATE), F32))

    past = cache_fox_k.shape[2]
    pos_sample = past + jnp.arange(dec_seq)
    mem_kv_sample = [(cache_mem_k[layer].reshape(dec_batch * mem_len, D_MODEL).astype(BF16),
                      cache_mem_v[layer].reshape(dec_batch * mem_len, D_MODEL).astype(BF16))
                     for layer in range(depth)]
    s5_st0 = jnp.concatenate([state_s5_re[0].reshape(dec_batch, S5_STATE),
                              state_s5_im[0].reshape(dec_batch, S5_STATE)], axis=1).astype(F32)
    y_sample, ns_ = _run_group(
        x_sample, pos_sample, w, gn_w, mem_kv_sample, cache_ret_state[0].astype(F32),
        (cache_fox_k[0], cache_fox_v[0], cache_fox_logf[0]), s5_st0)

    return (y_prompt, y_sample,
            np_["ret"], ns_["ret"],
            np_["fox_k"], np_["fox_v"], np_["fox_logf"],
            ns_["fox_k"], ns_["fox_v"], ns_["fox_logf"],
            np_["s5_re"], np_["s5_im"], ns_["s5_re"], ns_["s5_im"],
            jnp.stack(mem_k_out), jnp.stack(mem_v_out))
```

```python
import functools

import numpy as np
import jax
import jax.numpy as jnp
from jax import lax
from jax.experimental import pallas as pl
from jax.experimental.pallas import tpu as pltpu

F32 = jnp.float32
BF16 = jnp.bfloat16

D_MODEL = 1024
RET_H = 4
RET_D = 128
RET_W = RET_H * RET_D
CHUNK = 64
ROPE_BASE = 10000.0
FOX_H = 8
FOX_D = 64
FOX_W = FOX_H * FOX_D
S5_GROUP = 16
S5_G = D_MODEL // S5_GROUP
S5_P = 64
S5_STATE = S5_G * S5_P
X_H = 4
X_D = D_MODEL // X_H
D_FF = 4 * D_MODEL
EPS = 1e-6
NEG_INF = -1e30
LOG2E = float(np.log2(np.e))

LANES = 128
VMEM_LIMIT = 56 * 1024 * 1024

ROW_TILE = 512
RET_BLOCK = 256
FOX_TILE = 512
S5_TS = 64
S5_LANE_CHUNK = 512
CUMSUM_CHUNK = 256


def _dot(a, b):
    return jnp.dot(a, b, preferred_element_type=F32)


def _dot_nt(a, b):
    return lax.dot_general(a, b, (((1,), (1,)), ((), ())), preferred_element_type=F32)


def _dot_tn(a, b):
    return lax.dot_general(a, b, (((0,), (0,)), ((), ())), preferred_element_type=F32)


def _rms(x, w):
    return x * lax.rsqrt(jnp.mean(x * x, axis=-1, keepdims=True) + EPS) * w


def _const_spec(shape):
    nd = len(shape)
    return pl.BlockSpec(shape, lambda *_: (0,) * nd, pipeline_mode=pl.Buffered(1))


def _params(sem):
    return pltpu.CompilerParams(dimension_semantics=sem, vmem_limit_bytes=VMEM_LIMIT)


def _inproj_body(x_ref, g_ref, wr_ref, wf_ref, wl_ref, bl_ref, cos_ref, sin_ref,
                 rq_ref, rk_ref, rv_ref, rg_ref, fq_ref, fkb_ref, fvp_ref, fk_ref, fv_ref, lf_ref, *, head_major):
    def put_kv(dst_ref, z):
        if head_major:
            dst_ref[0] = z.T.reshape(FOX_H, FOX_D, z.shape[0])
        else:
            dst_ref[...] = z

    h = _rms(x_ref[...], g_ref[...]).astype(BF16)
    cos = cos_ref[...]
    sin = sin_ref[...]

    def rotary_to(dst_ref, z, scale):
        for hh in range(RET_H):
            sl = slice(hh * RET_D, (hh + 1) * RET_D)
            zz = z[:, sl]
            r = zz * cos + pltpu.roll(zz, RET_D // 2, 1) * sin
            if scale is not None:
                r = r * scale
            dst_ref[:, sl] = r.astype(BF16)

    rotary_to(rq_ref, _dot(h, wr_ref[:, 0:RET_W]), None)
    rotary_to(rk_ref, _dot(h, wr_ref[:, RET_W:2 * RET_W]), RET_D ** -0.5)
    rv_ref[...] = _dot(h, wr_ref[:, 2 * RET_W:3 * RET_W]).astype(BF16)
    rg_ref[...] = _dot(h, wr_ref[:, 3 * RET_W:4 * RET_W])
    fq_ref[...] = (_dot(h, wf_ref[:, 0:FOX_W]) * (LOG2E * FOX_D ** -0.5)).astype(BF16)
    fk = _dot(h, wf_ref[:, FOX_W:2 * FOX_W])
    put_kv(fk_ref, fk)
    fkb_ref[...] = fk.astype(BF16)
    fv = _dot(h, wf_ref[:, 2 * FOX_W:3 * FOX_W])
    put_kv(fv_ref, fv)
    lane = lax.broadcasted_iota(jnp.int32, (fv.shape[0], LANES), 1)
    for pr in range(FOX_H // 2):
        vp = fv[:, pr * LANES:(pr + 1) * LANES]
        fvp_ref[:, (2 * pr) * LANES:(2 * pr + 1) * LANES] = jnp.where(lane < FOX_D, vp, 1.0).astype(BF16)
        fvp_ref[:, (2 * pr + 1) * LANES:(2 * pr + 2) * LANES] = jnp.where(lane >= FOX_D, vp, 1.0).astype(BF16)
    zl = _dot(h, wl_ref[...]) + bl_ref[...]
    lf = jnp.minimum(zl, 0.0) - jnp.log1p(jnp.exp(-jnp.abs(zl)))
    if head_major:
        lf_ref[0] = lf.T[0:FOX_H, :]
    else:
        lf_ref[...] = lf[:, 0:FOX_H]


def _inproj(x, g, wr, wf, wl, bl, cos_t, sin_t, tm, seq):
    n = x.shape[0]
    nt = cos_t.shape[0] // tm
    head_major = tm <= seq
    row = lambda w: pl.BlockSpec((tm, w), lambda i: (i, 0))
    tab = pl.BlockSpec((tm, LANES), lambda i: (i % nt, 0))
    shp = lambda w, dt: jax.ShapeDtypeStruct((n, w), dt)
    if head_major:
        per_batch = seq // tm
        kv_spec = pl.BlockSpec((1, FOX_H, FOX_D, tm), lambda i: (i // per_batch, 0, 0, i % per_batch))
        kv_shape = jax.ShapeDtypeStruct((n // seq, FOX_H, FOX_D, seq), F32)
        lf_spec = pl.BlockSpec((1, FOX_H, tm), lambda i: (i // per_batch, 0, i % per_batch))
        lf_shape = jax.ShapeDtypeStruct((n // seq, FOX_H, seq), F32)
    else:
        kv_spec, kv_shape = row(FOX_W), shp(FOX_W, F32)
        lf_spec, lf_shape = row(FOX_H), shp(FOX_H, F32)
    return pl.pallas_call(
        functools.partial(_inproj_body, head_major=head_major),
        grid=(n // tm,),
        in_specs=[row(D_MODEL), _const_spec(g.shape), _const_spec(wr.shape), _const_spec(wf.shape),
                  _const_spec(wl.shape), _const_spec(bl.shape), tab, tab],
        out_specs=[row(RET_W), row(RET_W), row(RET_W), row(RET_W), row(FOX_W), row(FOX_W),
                   row(FOX_H * LANES), kv_spec, kv_spec, lf_spec],
        out_shape=[shp(RET_W, BF16), shp(RET_W, BF16), shp(RET_W, BF16), shp(RET_W, F32),
                   shp(FOX_W, BF16), shp(FOX_W, BF16), shp(FOX_H * LANES, BF16), kv_shape, kv_shape,
                   lf_shape],
        compiler_params=_params(("parallel",)),
        name="inproj_ab",
    )(x, g, wr, wf, wl, bl, cos_t, sin_t)


def _cumsum_body(x_ref, u_ref, o_ref):
    rows, total = x_ref.shape
    u = u_ref[...]

    def chunk(i, carry):
        off = pl.multiple_of(i * CUMSUM_CHUNK, CUMSUM_CHUNK)
        xc = x_ref[:, pl.ds(off, CUMSUM_CHUNK)]
        hi = xc.astype(BF16)
        r1 = xc - hi.astype(F32)
        mid = r1.astype(BF16)
        lo = (r1 - mid.astype(F32)).astype(BF16)
        y = _dot(hi, u) + _dot(mid, u) + _dot(lo, u) + carry
        o_ref[:, pl.ds(off, CUMSUM_CHUNK)] = y * LOG2E
        return y[:, CUMSUM_CHUNK - 1:CUMSUM_CHUNK]

    lax.fori_loop(0, total // CUMSUM_CHUNK, chunk, jnp.zeros((rows, 1), F32))


def _cumsum_lanes(x):
    idx = np.arange(CUMSUM_CHUNK)
    u = jnp.asarray(idx[:, None] <= idx[None, :], BF16)
    return pl.pallas_call(
        _cumsum_body,
        out_shape=jax.ShapeDtypeStruct(x.shape, F32),
        compiler_params=pltpu.CompilerParams(vmem_limit_bytes=VMEM_LIMIT),
        name="logf_cumsum",
    )(x, u)


def _ret_body(q_ref, k_ref, v_ref, rg_ref, dm_ref, wq_ref, wk_ref, dec_ref, gn_ref, s0_ref,
              ro_ref, s_ref):
    @pl.when(pl.program_id(1) == 0)
    def _():
        s_ref[...] = s0_ref[...]

    for hh in range(RET_H):
        sl = slice(hh * RET_D, (hh + 1) * RET_D)
        q = q_ref[:, sl]
        k = k_ref[:, sl]
        v = v_ref[:, sl]
        state = s_ref[0, hh]
        scores = _dot_nt(q, k) * dm_ref[hh]
        o = _dot(scores.astype(BF16), v) + _dot(q, state.astype(BF16)) * wq_ref[hh]
        kw = (k.astype(F32) * wk_ref[hh]).astype(BF16)
        s_ref[0, hh] = state * dec_ref[hh] + _dot_tn(kw, v)
        mu = jnp.mean(o, axis=-1, keepdims=True)
        oc = o - mu
        var = jnp.mean(oc * oc, axis=-1, keepdims=True)
        g = rg_ref[:, sl]
        ro_ref[:, sl] = (oc * lax.rsqrt(var + EPS) * gn_ref[hh] * (g * jax.nn.sigmoid(g))).astype(BF16)


def _retention(rq, rk, rv, rg, gn_w, s0, batch, seq, chunk):
    tb = min(RET_BLOCK, seq)
    nb = seq // tb
    lg = np.log(1.0 - 2.0 ** (-5.0 - np.arange(RET_H, dtype=np.float64)))
    idx = np.arange(tb, dtype=np.float64)
    visible = (idx[None, :] // chunk) <= (idx[:, None] // chunk)
    dm = np.exp(lg[:, None, None] * np.abs(idx[:, None] - idx[None, :])) * visible[None]
    wq = np.broadcast_to(np.exp(lg[:, None] * (idx[None, :] + 1.0))[:, :, None], (RET_H, tb, RET_D))
    wk = np.broadcast_to(np.exp(lg[:, None] * (tb - 1.0 - idx[None, :]))[:, :, None], (RET_H, tb, RET_D))
    dec = np.broadcast_to(np.exp(lg * tb)[:, None, None], (RET_H, 1, RET_D))
    consts = [jnp.asarray(a, F32) for a in (dm, wq, wk, dec)]
    gn = gn_w.reshape(RET_H, 1, RET_D)
    row = pl.BlockSpec((tb, RET_W), lambda b, t: (b * nb + t, 0))
    st = pl.BlockSpec((1, RET_H, RET_D, RET_D), lambda b, t: (b, 0, 0, 0))
    return pl.pallas_call(
        _ret_body,
        grid=(batch, nb),
        in_specs=[row, row, row, row] + [_const_spec(c.shape) for c in consts] + [_const_spec(gn.shape), st],
        out_specs=[row, st],
        out_shape=[jax.ShapeDtypeStruct((batch * seq, RET_W), BF16),
                   jax.ShapeDtypeStruct((batch, RET_H, RET_D, RET_D), F32)],
        compiler_params=_params(("parallel", "arbitrary")),
        name="retention",
    )(rq, rk, rv, rg, *consts, gn, s0)


def _fox_body(q_ref, k_ref, v_ref, c_ref, mask_ref, o_ref, qs_scr, m_scr, acc_scr, *, tq, tk):
    n_full = pl.program_id(2)
    q = q_ref[...]
    lane = lax.broadcasted_iota(jnp.int32, q.shape, 1)
    zero = jnp.zeros_like(q)
    qs_scr[0:tq, :] = jnp.where(lane < FOX_D, q, zero)
    qs_scr[tq:2 * tq, :] = jnp.where(lane >= FOX_D, q, zero)
    m_scr[...] = jnp.full(m_scr.shape, -jnp.inf, F32)
    acc_scr[...] = jnp.zeros(acc_scr.shape, F32)

    def tile(kj, masked):
        off = pl.multiple_of(kj * tk, tk)
        s = _dot_nt(qs_scr[...], k_ref[0, pl.ds(off, tk), :])
        for hh in range(2):
            rows = slice(hh * tq, (hh + 1) * tq)
            sh = s[rows, :] - c_ref[0, 0, hh:hh + 1, pl.ds(off, tk)]
            if masked:
                sh = sh + mask_ref[...]
            chunks = [sh[:, c * LANES:(c + 1) * LANES] for c in range(tk // LANES)]
            m_prev = m_scr[rows, :]
            m_new = jnp.maximum(m_prev, jnp.max(functools.reduce(jnp.maximum, chunks), axis=1, keepdims=True))
            p = jnp.concatenate([jnp.exp2(ch - m_new) for ch in chunks], axis=1).astype(BF16)
            pv = _dot(p, v_ref[0, pl.ds(off, tk), hh * LANES:(hh + 1) * LANES])
            acc_scr[rows, :] = jnp.exp2(m_prev - m_new) * acc_scr[rows, :] + pv
            m_scr[rows, :] = m_new

    def tile_pair(i, carry):
        tile(2 * i, False)
        tile(2 * i + 1, False)
        return carry

    lax.fori_loop(0, n_full // 2, tile_pair, 0)

    @pl.when(n_full % 2 == 1)
    def _():
        tile(n_full - 1, False)

    tile(n_full, True)
    acc_a = acc_scr[0:tq, :]
    acc_b = acc_scr[tq:2 * tq, :]
    half = LANES // 2
    o_ref[...] = jnp.where(lane < FOX_D, acc_a / pltpu.roll(acc_a, half, 1),
                           acc_b / pltpu.roll(acc_b, half, 1)).astype(BF16)


def _fox(fq, k_all, v_all, c_all, batch, seq, tile_len):
    tq = tk = tile_len
    nq = seq // tq
    idx = np.arange(tile_len)
    mask = jnp.asarray(np.where(idx[None, :] <= idx[:, None], 0.0, NEG_INF), F32)
    body = functools.partial(_fox_body, tq=tq, tk=tk)
    return pl.pallas_call(
        body,
        grid=(batch, FOX_H // 2, nq),
        in_specs=[pl.BlockSpec((tq, LANES), lambda b, h, i: (b * nq + i, h)),
                  pl.BlockSpec((1, seq, LANES), lambda b, h, i: (b, 0, h)),
                  pl.BlockSpec((1, seq, 2 * LANES), lambda b, h, i: (b, 0, h)),
                  pl.BlockSpec((1, 1, 2, seq), lambda b, h, i: (b, h, 0, 0)),
                  _const_spec(mask.shape)],
        out_specs=pl.BlockSpec((tq, LANES), lambda b, h, i: (b * nq + i, h)),
        out_shape=jax.ShapeDtypeStruct((batch * seq, FOX_W), BF16),
        scratch_shapes=[pltpu.VMEM((2 * tq, LANES), BF16), pltpu.VMEM((2 * tq, LANES), F32),
                        pltpu.VMEM((2 * tq, LANES), F32)],
        compiler_params=_params(("parallel", "parallel", "arbitrary")),
        name="fox_attention",
    )(fq, k_all, v_all, c_all, mask)


def _fox_cached_body(q_ref, kp_ref, vp_ref, kn_ref, vn_ref, c_ref, o_ref, *, past, lq):
    q = q_ref[...]
    lane = lax.broadcasted_iota(jnp.int32, q.shape, 1)
    zero = jnp.zeros_like(q)
    qs = jnp.concatenate([jnp.where(lane < FOX_D, q, zero), jnp.where(lane >= FOX_D, q, zero)], axis=0)
    head_rows = lambda a, b, width: jnp.concatenate(
        [jnp.broadcast_to(a, (lq, width)), jnp.broadcast_to(b, (lq, width))], axis=0)
    s_past = _dot(qs, kp_ref[0].astype(BF16)) - head_rows(c_ref[0, 0, 0:1, 0:past], c_ref[0, 0, 1:2, 0:past], past)
    s_new = _dot_nt(qs, kn_ref[...]) - head_rows(c_ref[0, 0, 0:1, past:past + lq],
                                                 c_ref[0, 0, 1:2, past:past + lq], lq)
    qi = lax.broadcasted_iota(jnp.int32, s_new.shape, 0)
    ki = lax.broadcasted_iota(jnp.int32, s_new.shape, 1)
    s_new = jnp.where(ki <= jnp.where(qi >= lq, qi - lq, qi), s_new, NEG_INF)
    m = jnp.maximum(jnp.max(s_past, axis=1, keepdims=True), jnp.max(s_new, axis=1, keepdims=True))
    p_past = jnp.exp2(s_past - m)
    p_new = jnp.exp2(s_new - m)
    denom = jnp.sum(p_past, axis=1, keepdims=True) + jnp.sum(p_new, axis=1, keepdims=True)
    pb_new = p_new.astype(BF16)
    acc = _dot_nt(p_past.astype(BF16), vp_ref[0].astype(BF16))
    acc_a = (acc[0:lq] + _dot(pb_new[0:lq], vn_ref[:, 0:LANES])) / denom[0:lq]
    acc_b = (acc[lq:2 * lq] + _dot(pb_new[lq:2 * lq], vn_ref[:, LANES:2 * LANES])) / denom[lq:2 * lq]
    o_ref[...] = jnp.where(lane < FOX_D, acc_a, acc_b).astype(BF16)


def _fox_cached(fq, k_past, v_past, k_new, v_new, c_all, batch, lq):
    past = k_past.shape[2]
    new_rows = lambda w: pl.BlockSpec((lq, w), lambda b, h: (b, h))
    cache = pl.BlockSpec((1, LANES, past), lambda b, h: (b, h, 0))
    return pl.pallas_call(
        functools.partial(_fox_cached_body, past=past, lq=lq),
        grid=(batch, FOX_H // 2),
        in_specs=[new_rows(LANES), cache, cache, new_rows(LANES), new_rows(2 * LANES),
                  pl.BlockSpec((1, 1, 2, c_all.shape[-1]), lambda b, h: (b, h, 0, 0))],
        out_specs=new_rows(LANES),
        out_shape=jax.ShapeDtypeStruct((batch * lq, FOX_W), BF16),
        compiler_params=_params(("parallel", "parallel")),
        name="fox_attention_cached",
    )(fq, k_past, v_past, k_new, v_new, c_all)


def _normproj_body(x_ref, g_ref, w_ref, o_ref, ob_ref):
    y = _dot(_rms(x_ref[...], g_ref[...]).astype(BF16), w_ref[...])
    o_ref[...] = y
    ob_ref[...] = y.astype(BF16)


def _normproj(x, g, w, tm):
    n, wout = x.shape[0], w.shape[1]
    return pl.pallas_call(
        _normproj_body,
        grid=(n // tm,),
        in_specs=[pl.BlockSpec((tm, D_MODEL), lambda i: (i, 0)), _const_spec(g.shape), _const_spec(w.shape)],
        out_specs=[pl.BlockSpec((tm, wout), lambda i: (i, 0))] * 2,
        out_shape=[jax.ShapeDtypeStruct((n, wout), F32), jax.ShapeDtypeStruct((n, wout), BF16)],
        compiler_params=_params(("parallel",)),
        name="memory_kv",
    )(x, g, w)


def _post_body(*refs, has_mix):
    if has_mix:
        (x_ref, ro_ref, fo_ref, wo_ref, nw_ref, wxq_ref, wxo_ref, mk_ref, mv_ref, wup_ref, wdn_ref,
         o_ref) = refs
    else:
        x_ref, nw_ref, wxq_ref, wxo_ref, mk_ref, mv_ref, wup_ref, wdn_ref, o_ref = refs
    x = x_ref[...]
    if has_mix:
        mix = _dot(ro_ref[...], wo_ref[0:RET_W, :]) + _dot(fo_ref[...], wo_ref[RET_W:RET_W + FOX_W, :])
        x = x + _rms(mix, nw_ref[1:2, :])

    h = _rms(x, nw_ref[2:3, :]).astype(BF16)
    q = _dot(h, wxq_ref[...])
    att = None
    for hh in range(X_H):
        sl = slice(hh * X_D, (hh + 1) * X_D)
        s = _dot_nt(q[:, sl].astype(BF16), mk_ref[:, sl]) * (X_D ** -0.5)
        p = jnp.exp(s - jnp.max(s, axis=-1, keepdims=True))
        o = _dot(p.astype(BF16), mv_ref[:, sl]) / jnp.sum(p, axis=-1, keepdims=True)
        part = _dot(o.astype(BF16), wxo_ref[sl, :])
        att = part if att is None else att + part
    x = x + _rms(att, nw_ref[3:4, :])

    h = _rms(x, nw_ref[4:5, :]).astype(BF16)
    acc = None
    for c in range(D_FF // D_MODEL):
        sl = slice(c * D_MODEL, (c + 1) * D_MODEL)
        r = jnp.maximum(_dot(h, wup_ref[:, sl]), 0.0)
        part = _dot((r * r).astype(BF16), wdn_ref[sl, :])
        acc = part if acc is None else acc + part
    o_ref[...] = x + _rms(acc, nw_ref[5:6, :])


def _post(x, mix_in, nw, wxq, wxo, mk, mv, wup, wdn, seq, tm):
    n = x.shape[0]
    mem_len = mk.shape[0] // (n // seq)
    per_batch = seq // tm
    row = lambda w: pl.BlockSpec((tm, w), lambda i: (i, 0))
    mem = pl.BlockSpec((mem_len, D_MODEL), lambda i: (i // per_batch, 0))
    args = [x]
    specs = [row(D_MODEL)]
    if mix_in is not None:
        ro, fo, wo = mix_in
        args += [ro, fo, wo]
        specs += [row(RET_W), row(FOX_W), _const_spec(wo.shape)]
    args += [nw, wxq, wxo, mk, mv, wup, wdn]
    specs += [_const_spec(nw.shape), _const_spec(wxq.shape), _const_spec(wxo.shape), mem, mem,
              _const_spec(wup.shape), _const_spec(wdn.shape)]
    return pl.pallas_call(
        functools.partial(_post_body, has_mix=mix_in is not None),
        grid=(n // tm,),
        in_specs=specs,
        out_specs=row(D_MODEL),
        out_shape=jax.ShapeDtypeStruct((n, D_MODEL), F32),
        compiler_params=_params(("parallel",)),
        name="xattn_mlp_mix" if mix_in is not None else "xattn_mlp",
    )(*args)


def _s5prep_body(lre_ref, lim_ref, ldt_ref, lre_rep_ref, lim_rep_ref, bre_ref, bim_ref,
                 are_ref, aim_ref, bbre_ref, bbim_ref):
    dt = jnp.exp(ldt_ref[...])

    def zoh(lre, lim):
        mag = jnp.exp(lre * dt)
        a_re = mag * jnp.cos(lim * dt)
        a_im = mag * jnp.sin(lim * dt)
        den = lre * lre + lim * lim
        num_re = a_re - 1.0
        f_re = (num_re * lre + a_im * lim) / den
        f_im = (a_im * lre - num_re * lim) / den
        return a_re, a_im, f_re, f_im

    a_re, a_im, _, _ = zoh(lre_ref[...], lim_ref[...])
    are_ref[...] = a_re
    aim_ref[...] = a_im
    _, _, f_re, f_im = zoh(lre_rep_ref[...], lim_rep_ref[...])
    b_re = bre_ref[...]
    b_im = bim_ref[...]
    bbre_ref[...] = f_re * b_re - f_im * b_im
    bbim_ref[...] = f_re * b_im + f_im * b_re


def _s5_discretize(lam_re, lam_im, log_dt, b_re, b_im):
    rep = lambda a: jnp.repeat(a, S5_GROUP, axis=1)
    wide = jax.ShapeDtypeStruct((S5_G, S5_P * S5_GROUP), F32)
    small = jax.ShapeDtypeStruct((S5_G, S5_P), F32)
    a_re, a_im, bb_re, bb_im = pl.pallas_call(
        _s5prep_body, out_shape=[small, small, wide, wide], name="s5_discretize",
    )(lam_re, lam_im, log_dt.reshape(S5_G, 1), rep(lam_re), rep(lam_im),
      b_re.reshape(S5_G, -1), b_im.reshape(S5_G, -1))
    a = jnp.stack([a_re.reshape(-1), a_im.reshape(-1)])
    gpc = S5_LANE_CHUNK // S5_P
    nch = S5_G // gpc
    eye = jnp.eye(gpc, dtype=F32)

    def in_blocks(bb):
        t = bb.reshape(nch, gpc, S5_P, S5_GROUP).transpose(0, 1, 3, 2)
        return (t[:, :, :, None, :] * eye[None, :, None, :, None]).reshape(nch, gpc * S5_GROUP, gpc * S5_P)

    bmat = jnp.concatenate([in_blocks(bb_re), in_blocks(bb_im)], axis=-1).astype(BF16)
    return a, bmat


def _s5_out_matrices(c_re, c_im):
    gpc = S5_LANE_CHUNK // S5_P
    nch = S5_G // gpc
    eye = jnp.eye(gpc, dtype=F32)

    def out_blocks(c):
        t = c.reshape(nch, gpc, S5_GROUP, S5_P).transpose(0, 1, 3, 2)
        return (t[:, :, :, None, :] * eye[None, :, None, :, None]).reshape(nch, gpc * S5_P, gpc * S5_GROUP)

    return jnp.concatenate([out_blocks(c_re), -out_blocks(c_im)], axis=1).astype(BF16)


def _s5_body(x_ref, nw_ref, perm_ref, win_ref, bmat_ref, a_ref, cmat_ref, d_ref, wglu_ref, st0_ref,
             o_ref, st_ref, u_scr, bu_scr, y_scr, *, ts):
    nb = x_ref.shape[0]
    rows = nb * ts
    lc = S5_LANE_CHUNK
    gw = (lc // S5_P) * S5_GROUP

    @pl.when(pl.program_id(0) == 0)
    def _():
        st_ref[...] = st0_ref[...]

    x = x_ref[...].reshape(rows, D_MODEL)
    h = _dot(perm_ref[0], _rms(x, nw_ref[0:1, :]).astype(BF16)).astype(BF16)
    u_scr[...] = _dot(h, win_ref[...])
    nv = lc // LANES
    for j in range(S5_STATE // lc):
        uj = u_scr[:, j * gw:(j + 1) * gw]
        bu = _dot(uj.astype(BF16), bmat_ref[j])
        for c in range(2 * nv):
            bu_scr[c] = bu[:, c * LANES:(c + 1) * LANES]
        a_re = [jnp.broadcast_to(a_ref[0:1, j * lc + c * LANES:j * lc + (c + 1) * LANES], (nb, LANES))
                for c in range(nv)]
        a_im = [jnp.broadcast_to(a_ref[1:2, j * lc + c * LANES:j * lc + (c + 1) * LANES], (nb, LANES))
                for c in range(nv)]

        def step(t, carry):
            idx = pl.ds(pl.multiple_of(t * nb, nb), nb)
            new = []
            for c in range(nv):
                s_re, s_im = carry[c], carry[nv + c]
                new.append(a_re[c] * s_re - a_im[c] * s_im + bu_scr[c, idx, :])
            for c in range(nv):
                s_re, s_im = carry[c], carry[nv + c]
                new.append(a_re[c] * s_im + a_im[c] * s_re + bu_scr[nv + c, idx, :])
            for c in range(2 * nv):
                bu_scr[c, idx, :] = new[c]
            return tuple(new)

        init = tuple(st_ref[:, j * lc + c * LANES:j * lc + (c + 1) * LANES] for c in range(nv)) + tuple(
            st_ref[:, S5_STATE + j * lc + c * LANES:S5_STATE + j * lc + (c + 1) * LANES] for c in range(nv))
        fin = lax.fori_loop(0, ts, step, init, unroll=8)
        for c in range(nv):
            st_ref[:, j * lc + c * LANES:j * lc + (c + 1) * LANES] = fin[c]
            st_ref[:, S5_STATE + j * lc + c * LANES:S5_STATE + j * lc + (c + 1) * LANES] = fin[nv + c]
        xs = jnp.concatenate([bu_scr[c] for c in range(2 * nv)], axis=1).astype(BF16)
        y_scr[:, j * gw:(j + 1) * gw] = _dot(xs, cmat_ref[j]) + d_ref[0:1, j * gw:(j + 1) * gw] * uj
    y = _dot(perm_ref[1], y_scr[...].astype(BF16)).astype(BF16)
    g = _dot(y, wglu_ref[...])
    out = g[:, 0:D_MODEL] * jax.nn.sigmoid(g[:, D_MODEL:2 * D_MODEL])
    o_ref[...] = (x + _rms(out, nw_ref[1:2, :])).reshape(nb, ts, D_MODEL)


def _s5_layer(x3, nw, win, bmat, a, cmat, d, wglu, st0, ts):
    nb, seq, _ = x3.shape
    rows = nb * ts
    blk = pl.BlockSpec((nb, ts, D_MODEL), lambda t: (0, t, 0))
    r = np.arange(rows)
    fwd = np.zeros((rows, rows), np.float32)
    fwd[(r % ts) * nb + r // ts, r] = 1.0
    perm = jnp.asarray(np.stack([fwd, fwd.T]), BF16)
    consts = [nw, perm, win, bmat, a, cmat, d, wglu, st0]
    return pl.pallas_call(
        functools.partial(_s5_body, ts=ts),
        grid=(seq // ts,),
        in_specs=[blk] + [_const_spec(c.shape) for c in consts],
        out_specs=[blk, pl.BlockSpec(st0.shape, lambda t: (0, 0))],
        out_shape=[jax.ShapeDtypeStruct(x3.shape, F32), jax.ShapeDtypeStruct(st0.shape, F32)],
        scratch_shapes=[pltpu.VMEM((rows, D_MODEL), F32),
                        pltpu.VMEM((2 * S5_LANE_CHUNK // LANES, rows, LANES), F32),
                        pltpu.VMEM((rows, D_MODEL), F32)],
        compiler_params=_params(("arbitrary",)),
        name="s5_layer",
    )(x3, *consts)


def _rotary_tables(pos, rows):
    half = RET_D // 2
    inv = ROPE_BASE ** (-jnp.arange(half, dtype=F32) / half)
    ang = pos.astype(F32)[:, None] * inv[None, :]
    cos = jnp.cos(ang)
    sin = jnp.sin(ang)
    cos_t = jnp.concatenate([cos, cos], axis=1)
    sin_t = jnp.concatenate([-sin, sin], axis=1)
    reps = max(1, rows // pos.shape[0])
    return jnp.tile(cos_t, (reps, 1)), jnp.tile(sin_t, (reps, 1))


def _prepare_weights(norm_w, w_in_ab, b_fox_f, w_out_ab, w_in_c, s5_lambda_re, s5_lambda_im, s5_log_dt,
                     s5_b_re, s5_b_im, s5_c_re, s5_c_im, s5_d, w_glu, w_xq, w_xo, w_up, w_down):
    w = {}
    wab = w_in_ab[0].astype(BF16)
    w["wr"] = wab[:, 0:4 * RET_W]
    w["wf"] = wab[:, 4 * RET_W:4 * RET_W + 3 * FOX_W]
    w["wl"] = jnp.pad(wab[:, 4 * RET_W + 3 * FOX_W:], ((0, 0), (0, LANES - FOX_H)))
    w["bl"] = jnp.pad(b_fox_f[0].astype(F32), (0, LANES - FOX_H)).reshape(1, LANES)
    w["wo"] = w_out_ab[0].astype(BF16)
    w["nw"] = norm_w.astype(F32)
    w["win_c"] = w_in_c[0].astype(BF16)
    w["a"], w["bmat"] = _s5_discretize(s5_lambda_re[0], s5_lambda_im[0], s5_log_dt[0], s5_b_re[0], s5_b_im[0])
    w["cmat"] = _s5_out_matrices(s5_c_re[0].astype(F32), s5_c_im[0].astype(F32))
    w["d"] = s5_d[0].astype(F32).reshape(1, D_MODEL)
    w["wglu"] = w_glu[0].astype(BF16)
    for name, arr in (("wxq", w_xq), ("wxo", w_xo), ("wup", w_up), ("wdn", w_down)):
        w[name] = arr.astype(BF16)
    return w


def _run_group(x3, pos, w, gn_w, mem_kv, ret_s0, fox_past, s5_st0):
    batch, seq, _ = x3.shape
    n = batch * seq
    tm = min(ROW_TILE, n)
    tm_b = min(ROW_TILE, seq)
    x = x3.reshape(n, D_MODEL)

    cos_t, sin_t = _rotary_tables(pos, tm)
    rq, rk, rv, rg, fq, fkb, fvp, fk, fv, lf = _inproj(
        x, w["nw"][0, 0:1], w["wr"], w["wf"], w["wl"], w["bl"], cos_t, sin_t, tm, seq)
    if tm <= seq:
        lf_t = lf
        fk = jnp.transpose(fk, (0, 3, 1, 2))
        fv = jnp.transpose(fv, (0, 3, 1, 2))
        lf3 = jnp.transpose(lf, (0, 2, 1))
    else:
        lf3 = lf.reshape(batch, seq, FOX_H)
        lf_t = jnp.transpose(lf3, (0, 2, 1))
    if fox_past is not None:
        lf_t = jnp.concatenate([jnp.transpose(fox_past[2].astype(F32), (0, 2, 1)), lf_t], axis=2)
    lk = lf_t.shape[2]
    pad = -lk % CUMSUM_CHUNK
    lf_rows = jnp.pad(lf_t.reshape(batch * FOX_H, lk), ((0, 0), (0, pad)))
    c_all = _cumsum_lanes(lf_rows).reshape(batch, FOX_H // 2, 2, lk + pad)
    if fox_past is None:
        fo = _fox(fq, fkb.reshape(batch, seq, FOX_W), fvp.reshape(batch, seq, FOX_H * LANES), c_all,
                  batch, seq, FOX_TILE)
    else:
        past = fox_past[0].shape[1]
        cache_t = lambda a: jnp.transpose(a, (0, 2, 3, 1)).reshape(batch, FOX_W, past)
        fo = _fox_cached(fq, cache_t(fox_past[0]), cache_t(fox_past[1]), fkb, fvp, c_all, batch, seq)

    chunk = CHUNK if fox_past is None else seq
    ro, ret_state = _retention(rq, rk, rv, rg, gn_w, ret_s0, batch, seq, chunk)

    x = _post(x, (ro, fo, w["wo"]), w["nw"][0], w["wxq"][0], w["wxo"][0], mem_kv[0][0], mem_kv[0][1],
              w["wup"][0], w["wdn"][0], seq, tm_b)

    ts = min(S5_TS, seq)
    x3n, s5_state = _s5_layer(x.reshape(batch, seq, D_MODEL), w["nw"][1], w["win_c"], w["bmat"], w["a"],
                              w["cmat"], w["d"], w["wglu"], s5_st0, ts)
    x = _post(x3n.reshape(n, D_MODEL), None, w["nw"][1], w["wxq"][1], w["wxo"][1], mem_kv[1][0], mem_kv[1][1],
              w["wup"][1], w["wdn"][1], seq, tm_b)

    new = dict(ret=ret_state[None], fox_k=fk.reshape(1, batch, seq, FOX_H, FOX_D),
               fox_v=fv.reshape(1, batch, seq, FOX_H, FOX_D), fox_logf=lf3[None],
               s5_re=s5_state[:, 0:S5_STATE].reshape(1, batch, S5_G, S5_P),
               s5_im=s5_state[:, S5_STATE:].reshape(1, batch, S5_G, S5_P))
    return x.reshape(batch, seq, D_MODEL), new


def kernel(x_prompt, x_sample, cache_ret_state, cache_fox_k, cache_fox_v, cache_fox_logf, state_s5_re, state_s5_im, cache_mem_k, cache_mem_v, mem_prompt, norm_w, w_in_ab, b_fox_f, ret_gn_w, w_out_ab, w_in_c, s5_lambda_re, s5_lambda_im, s5_log_dt, s5_b_re, s5_b_im, s5_c_re, s5_c_im, s5_d, w_glu, mem_norm_w, w_xq, w_xk, w_xv, w_xo, w_up, w_down):
    w = _prepare_weights(norm_w, w_in_ab, b_fox_f, w_out_ab, w_in_c, s5_lambda_re, s5_lambda_im, s5_log_dt,
                         s5_b_re, s5_b_im, s5_c_re, s5_c_im, s5_d, w_glu, w_xq, w_xo, w_up, w_down)
    batch, seq, _ = x_prompt.shape
    dec_batch, dec_seq, _ = x_sample.shape
    mem_len = mem_prompt.shape[1]
    depth = norm_w.shape[0]
    gn_w = ret_gn_w[0].astype(F32)

    mem_rows = mem_prompt.reshape(batch * mem_len, D_MODEL)
    mem_kv_prompt = []
    mem_k_out = []
    mem_v_out = []
    for layer in range(depth):
        wkv = jnp.concatenate([w_xk[layer], w_xv[layer]], axis=1).astype(BF16)
        kv, kvb = _normproj(mem_rows, mem_norm_w[layer].astype(F32).reshape(1, D_MODEL), wkv,
                            min(ROW_TILE, batch * mem_len))
        mem_kv_prompt.append((kvb[:, 0:D_MODEL], kvb[:, D_MODEL:]))
        mem_k_out.append(kv[:, 0:D_MODEL].reshape(batch, mem_len, X_H, X_D))
        mem_v_out.append(kv[:, D_MODEL:].reshape(batch, mem_len, X_H, X_D))

    pos_prompt = jnp.arange(seq)
    y_prompt, np_ = _run_group(
        x_prompt, pos_prompt, w, gn_w, mem_kv_prompt,
        jnp.zeros((batch, RET_H, RET_D, RET_D), F32), None, jnp.zeros((batch, 2 * S5_STATE), F32))

    past = cache_fox_k.shape[2]
    pos_sample = past + jnp.arange(dec_seq)
    mem_kv_sample = [(cache_mem_k[layer].reshape(dec_batch * mem_len, D_MODEL).astype(BF16),
                      cache_mem_v[layer].reshape(dec_batch * mem_len, D_MODEL).astype(BF16))
                     for layer in range(depth)]
    s5_st0 = jnp.concatenate([state_s5_re[0].reshape(dec_batch, S5_STATE),
                              state_s5_im[0].reshape(dec_batch, S5_STATE)], axis=1).astype(F32)
    y_sample, ns_ = _run_group(
        x_sample, pos_sample, w, gn_w, mem_kv_sample, cache_ret_state[0].astype(F32),
        (cache_fox_k[0], cache_fox_v[0], cache_fox_logf[0]), s5_st0)

    return (y_prompt, y_sample,
            np_["ret"], ns_["ret"],
            np_["fox_k"], np_["fox_v"], np_["fox_logf"],
            ns_["fox_k"], ns_["fox_v"], ns_["fox_logf"],
            np_["s5_re"], np_["s5_im"], ns_["s5_re"], ns_["s5_im"],
            jnp.stack(mem_k_out), jnp.stack(mem_v_out))
```

```python
import functools

import numpy as np
import jax
import jax.numpy as jnp
from jax import lax
from jax.experimental import pallas as pl
from jax.experimental.pallas import tpu as pltpu

F32 = jnp.float32
BF16 = jnp.bfloat16

D_MODEL = 1024
RET_H = 4
RET_D = 128
RET_W = RET_H * RET_D
CHUNK = 64
ROPE_BASE = 10000.0
FOX_H = 8
FOX_D = 64
FOX_W = FOX_H * FOX_D
S5_GROUP = 16
S5_G = D_MODEL // S5_GROUP
S5_P = 64
S5_STATE = S5_G * S5_P
X_H = 4
X_D = D_MODEL // X_H
D_FF = 4 * D_MODEL
EPS = 1e-6
NEG_INF = -1e30
LOG2E = float(np.log2(np.e))

LANES = 128
VMEM_LIMIT = 56 * 1024 * 1024

ROW_TILE = 512
POST_PARTS = 2
RET_BLOCK = 256
FOX_TILE = 512
S5_TS = 64
S5_LANE_CHUNK = 512
CUMSUM_CHUNK = 256


def _dot(a, b):
    return jnp.dot(a, b, preferred_element_type=F32)


def _dot_nt(a, b):
    return lax.dot_general(a, b, (((1,), (1,)), ((), ())), preferred_element_type=F32)


def _dot_tn(a, b):
    return lax.dot_general(a, b, (((0,), (0,)), ((), ())), preferred_element_type=F32)


def _rms(x, w):
    return x * lax.rsqrt(jnp.mean(x * x, axis=-1, keepdims=True) + EPS) * w


def _const_spec(shape):
    nd = len(shape)
    return pl.BlockSpec(shape, lambda *_: (0,) * nd, pipeline_mode=pl.Buffered(1))


def _params(sem):
    return pltpu.CompilerParams(dimension_semantics=sem, vmem_limit_bytes=VMEM_LIMIT)


def _inproj_body(x_ref, g_ref, wr_ref, wf_ref, wl_ref, bl_ref, cos_ref, sin_ref,
                 rq_ref, rk_ref, rv_ref, rg_ref, fq_ref, fkb_ref, fvp_ref, fk_ref, fv_ref, lf_ref, *, head_major):
    def put_kv(dst_ref, z):
        if head_major:
            dst_ref[0] = z.T.reshape(FOX_H, FOX_D, z.shape[0])
        else:
            dst_ref[...] = z

    h = _rms(x_ref[...], g_ref[...]).astype(BF16)
    cos = cos_ref[...]
    sin = sin_ref[...]

    def rotary_to(dst_ref, z, scale):
        for hh in range(RET_H):
            sl = slice(hh * RET_D, (hh + 1) * RET_D)
            zz = z[:, sl]
            r = zz * cos + pltpu.roll(zz, RET_D // 2, 1) * sin
            if scale is not None:
                r = r * scale
            dst_ref[:, sl] = r.astype(BF16)

    rotary_to(rq_ref, _dot(h, wr_ref[:, 0:RET_W]), None)
    rotary_to(rk_ref, _dot(h, wr_ref[:, RET_W:2 * RET_W]), RET_D ** -0.5)
    rv_ref[...] = _dot(h, wr_ref[:, 2 * RET_W:3 * RET_W]).astype(BF16)
    rg_ref[...] = _dot(h, wr_ref[:, 3 * RET_W:4 * RET_W])
    fq_ref[...] = (_dot(h, wf_ref[:, 0:FOX_W]) * (LOG2E * FOX_D ** -0.5)).astype(BF16)
    fk = _dot(h, wf_ref[:, FOX_W:2 * FOX_W])
    put_kv(fk_ref, fk)
    fkb_ref[...] = fk.astype(BF16)
    fv = _dot(h, wf_ref[:, 2 * FOX_W:3 * FOX_W])
    put_kv(fv_ref, fv)
    lane = lax.broadcasted_iota(jnp.int32, (fv.shape[0], LANES), 1)
    for pr in range(FOX_H // 2):
        vp = fv[:, pr * LANES:(pr + 1) * LANES]
        fvp_ref[:, (2 * pr) * LANES:(2 * pr + 1) * LANES] = jnp.where(lane < FOX_D, vp, 1.0).astype(BF16)
        fvp_ref[:, (2 * pr + 1) * LANES:(2 * pr + 2) * LANES] = jnp.where(lane >= FOX_D, vp, 1.0).astype(BF16)
    zl = _dot(h, wl_ref[...]) + bl_ref[...]
    lf = jnp.minimum(zl, 0.0) - jnp.log1p(jnp.exp(-jnp.abs(zl)))
    if head_major:
        lf_ref[0] = lf.T[0:FOX_H, :]
    else:
        lf_ref[...] = lf[:, 0:FOX_H]


def _inproj(x, g, wr, wf, wl, bl, cos_t, sin_t, tm, seq):
    n = x.shape[0]
    nt = cos_t.shape[0] // tm
    head_major = tm <= seq
    row = lambda w: pl.BlockSpec((tm, w), lambda i: (i, 0))
    tab = pl.BlockSpec((tm, LANES), lambda i: (i % nt, 0))
    shp = lambda w, dt: jax.ShapeDtypeStruct((n, w), dt)
    if head_major:
        per_batch = seq // tm
        kv_spec = pl.BlockSpec((1, FOX_H, FOX_D, tm), lambda i: (i // per_batch, 0, 0, i % per_batch))
        kv_shape = jax.ShapeDtypeStruct((n // seq, FOX_H, FOX_D, seq), F32)
        lf_spec = pl.BlockSpec((1, FOX_H, tm), lambda i: (i // per_batch, 0, i % per_batch))
        lf_shape = jax.ShapeDtypeStruct((n // seq, FOX_H, seq), F32)
    else:
        kv_spec, kv_shape = row(FOX_W), shp(FOX_W, F32)
        lf_spec, lf_shape = row(FOX_H), shp(FOX_H, F32)
    return pl.pallas_call(
        functools.partial(_inproj_body, head_major=head_major),
        grid=(n // tm,),
        in_specs=[row(D_MODEL), _const_spec(g.shape), _const_spec(wr.shape), _const_spec(wf.shape),
                  _const_spec(wl.shape), _const_spec(bl.shape), tab, tab],
        out_specs=[row(RET_W), row(RET_W), row(RET_W), row(RET_W), row(FOX_W), row(FOX_W),
                   row(FOX_H * LANES), kv_spec, kv_spec, lf_spec],
        out_shape=[shp(RET_W, BF16), shp(RET_W, BF16), shp(RET_W, BF16), shp(RET_W, F32),
                   shp(FOX_W, BF16), shp(FOX_W, BF16), shp(FOX_H * LANES, BF16), kv_shape, kv_shape,
                   lf_shape],
        compiler_params=_params(("parallel",)),
        name="inproj_ab",
    )(x, g, wr, wf, wl, bl, cos_t, sin_t)


def _cumsum_body(x_ref, u_ref, o_ref):
    rows, total = x_ref.shape
    u = u_ref[...]

    def chunk(i, carry):
        off = pl.multiple_of(i * CUMSUM_CHUNK, CUMSUM_CHUNK)
        xc = x_ref[:, pl.ds(off, CUMSUM_CHUNK)]
        hi = xc.astype(BF16)
        r1 = xc - hi.astype(F32)
        mid = r1.astype(BF16)
        lo = (r1 - mid.astype(F32)).astype(BF16)
        y = _dot(hi, u) + _dot(mid, u) + _dot(lo, u) + carry
        o_ref[:, pl.ds(off, CUMSUM_CHUNK)] = y * LOG2E
        return y[:, CUMSUM_CHUNK - 1:CUMSUM_CHUNK]

    lax.fori_loop(0, total // CUMSUM_CHUNK, chunk, jnp.zeros((rows, 1), F32))


def _cumsum_lanes(x):
    idx = np.arange(CUMSUM_CHUNK)
    u = jnp.asarray(idx[:, None] <= idx[None, :], BF16)
    return pl.pallas_call(
        _cumsum_body,
        out_shape=jax.ShapeDtypeStruct(x.shape, F32),
        compiler_params=pltpu.CompilerParams(vmem_limit_bytes=VMEM_LIMIT),
        name="logf_cumsum",
    )(x, u)


def _ret_body(q_ref, k_ref, v_ref, rg_ref, dm_ref, wq_ref, wk_ref, dec_ref, gn_ref, s0_ref,
              ro_ref, s_ref):
    @pl.when(pl.program_id(1) == 0)
    def _():
        s_ref[...] = s0_ref[...]

    heads = range(RET_H)
    sls = [slice(hh * RET_D, (hh + 1) * RET_D) for hh in heads]
    qs = [q_ref[:, sl] for sl in sls]
    ks = [k_ref[:, sl] for sl in sls]
    vs = [v_ref[:, sl] for sl in sls]
    states = [s_ref[0, hh] for hh in heads]
    scores = [(_dot_nt(qs[hh], ks[hh]) * dm_ref[hh]).astype(BF16) for hh in heads]
    inter = [_dot(qs[hh], states[hh].astype(BF16)) * wq_ref[hh] for hh in heads]
    outs = [_dot(scores[hh], vs[hh]) + inter[hh] for hh in heads]
    kws = [(ks[hh].astype(F32) * wk_ref[hh]).astype(BF16) for hh in heads]
    for hh in heads:
        s_ref[0, hh] = states[hh] * dec_ref[hh] + _dot_tn(kws[hh], vs[hh])
    for hh in heads:
        o = outs[hh]
        mu = jnp.mean(o, axis=-1, keepdims=True)
        oc = o - mu
        var = jnp.mean(oc * oc, axis=-1, keepdims=True)
        g = rg_ref[:, sls[hh]]
        ro_ref[:, sls[hh]] = (oc * lax.rsqrt(var + EPS) * gn_ref[hh] * (g * jax.nn.sigmoid(g))).astype(BF16)


def _retention(rq, rk, rv, rg, gn_w, s0, batch, seq, chunk):
    tb = min(RET_BLOCK, seq)
    nb = seq // tb
    lg = np.log(1.0 - 2.0 ** (-5.0 - np.arange(RET_H, dtype=np.float64)))
    idx = np.arange(tb, dtype=np.float64)
    visible = (idx[None, :] // chunk) <= (idx[:, None] // chunk)
    dm = np.exp(lg[:, None, None] * np.abs(idx[:, None] - idx[None, :])) * visible[None]
    wq = np.broadcast_to(np.exp(lg[:, None] * (idx[None, :] + 1.0))[:, :, None], (RET_H, tb, RET_D))
    wk = np.broadcast_to(np.exp(lg[:, None] * (tb - 1.0 - idx[None, :]))[:, :, None], (RET_H, tb, RET_D))
    dec = np.broadcast_to(np.exp(lg * tb)[:, None, None], (RET_H, 1, RET_D))
    consts = [jnp.asarray(a, F32) for a in (dm, wq, wk, dec)]
    gn = gn_w.reshape(RET_H, 1, RET_D)
    row = pl.BlockSpec((tb, RET_W), lambda b, t: (b * nb + t, 0))
    st = pl.BlockSpec((1, RET_H, RET_D, RET_D), lambda b, t: (b, 0, 0, 0))
    return pl.pallas_call(
        _ret_body,
        grid=(batch, nb),
        in_specs=[row, row, row, row] + [_const_spec(c.shape) for c in consts] + [_const_spec(gn.shape), st],
        out_specs=[row, st],
        out_shape=[jax.ShapeDtypeStruct((batch * seq, RET_W), BF16),
                   jax.ShapeDtypeStruct((batch, RET_H, RET_D, RET_D), F32)],
        compiler_params=_params(("parallel", "arbitrary")),
        name="retention",
    )(rq, rk, rv, rg, *consts, gn, s0)


def _fox_body(q_ref, k_ref, v_ref, c_ref, mask_ref, o_ref, qs_scr, m_scr, acc_scr, *, tq, tk):
    n_full = pl.program_id(2)
    q = q_ref[...]
    lane = lax.broadcasted_iota(jnp.int32, q.shape, 1)
    zero = jnp.zeros_like(q)
    qs_scr[0:tq, :] = jnp.where(lane < FOX_D, q, zero)
    qs_scr[tq:2 * tq, :] = jnp.where(lane >= FOX_D, q, zero)
    m_scr[...] = jnp.full(m_scr.shape, -jnp.inf, F32)
    acc_scr[...] = jnp.zeros(acc_scr.shape, F32)

    def tile(kj, masked):
        off = pl.multiple_of(kj * tk, tk)
        s = _dot_nt(qs_scr[...], k_ref[0, pl.ds(off, tk), :])
        hrows = [slice(hh * tq, (hh + 1) * tq) for hh in range(2)]
        shs = [s[hrows[hh], :] - c_ref[0, 0, hh:hh + 1, pl.ds(off, tk)] for hh in range(2)]
        if masked:
            shs = [sh + mask_ref[...] for sh in shs]
        chunks = [[sh[:, c * LANES:(c + 1) * LANES] for c in range(tk // LANES)] for sh in shs]
        m_prevs = [m_scr[rows, :] for rows in hrows]
        m_news = [jnp.maximum(m_prevs[hh], jnp.max(functools.reduce(jnp.maximum, chunks[hh]), axis=1, keepdims=True))
                  for hh in range(2)]
        ps = [jnp.concatenate([jnp.exp2(ch - m_news[hh]) for ch in chunks[hh]], axis=1).astype(BF16)
              for hh in range(2)]
        pvs = [_dot(ps[hh], v_ref[0, pl.ds(off, tk), hh * LANES:(hh + 1) * LANES]) for hh in range(2)]
        for hh in range(2):
            acc_scr[hrows[hh], :] = jnp.exp2(m_prevs[hh] - m_news[hh]) * acc_scr[hrows[hh], :] + pvs[hh]
            m_scr[hrows[hh], :] = m_news[hh]

    def tile_pair(i, carry):
        tile(2 * i, False)
        tile(2 * i + 1, False)
        return carry

    lax.fori_loop(0, n_full // 2, tile_pair, 0)

    @pl.when(n_full % 2 == 1)
    def _():
        tile(n_full - 1, False)

    tile(n_full, True)
    acc_a = acc_scr[0:tq, :]
    acc_b = acc_scr[tq:2 * tq, :]
    half = LANES // 2
    o_ref[...] = jnp.where(lane < FOX_D, acc_a / pltpu.roll(acc_a, half, 1),
                           acc_b / pltpu.roll(acc_b, half, 1)).astype(BF16)


def _fox(fq, k_all, v_all, c_all, batch, seq, tile_len):
    tq = tk = tile_len
    nq = seq // tq
    idx = np.arange(tile_len)
    mask = jnp.asarray(np.where(idx[None, :] <= idx[:, None], 0.0, NEG_INF), F32)
    body = functools.partial(_fox_body, tq=tq, tk=tk)
    return pl.pallas_call(
        body,
        grid=(batch, FOX_H // 2, nq),
        in_specs=[pl.BlockSpec((tq, LANES), lambda b, h, i: (b * nq + i, h)),
                  pl.BlockSpec((1, seq, LANES), lambda b, h, i: (b, 0, h)),
                  pl.BlockSpec((1, seq, 2 * LANES), lambda b, h, i: (b, 0, h)),
                  pl.BlockSpec((1, 1, 2, seq), lambda b, h, i: (b, h, 0, 0)),
                  _const_spec(mask.shape)],
        out_specs=pl.BlockSpec((tq, LANES), lambda b, h, i: (b * nq + i, h)),
        out_shape=jax.ShapeDtypeStruct((batch * seq, FOX_W), BF16),
        scratch_shapes=[pltpu.VMEM((2 * tq, LANES), BF16), pltpu.VMEM((2 * tq, LANES), F32),
                        pltpu.VMEM((2 * tq, LANES), F32)],
        compiler_params=_params(("parallel", "parallel", "arbitrary")),
        name="fox_attention",
    )(fq, k_all, v_all, c_all, mask)


def _fox_cached_body(q_ref, kp_ref, vp_ref, kn_ref, vn_ref, c_ref, o_ref, *, past, lq):
    q = q_ref[...]
    lane = lax.broadcasted_iota(jnp.int32, q.shape, 1)
    zero = jnp.zeros_like(q)
    qs = jnp.concatenate([jnp.where(lane < FOX_D, q, zero), jnp.where(lane >= FOX_D, q, zero)], axis=0)
    head_rows = lambda a, b, width: jnp.concatenate(
        [jnp.broadcast_to(a, (lq, width)), jnp.broadcast_to(b, (lq, width))], axis=0)
    s_past = _dot(qs, kp_ref[0].astype(BF16)) - head_rows(c_ref[0, 0, 0:1, 0:past], c_ref[0, 0, 1:2, 0:past], past)
    s_new = _dot_nt(qs, kn_ref[...]) - head_rows(c_ref[0, 0, 0:1, past:past + lq],
                                                 c_ref[0, 0, 1:2, past:past + lq], lq)
    qi = lax.broadcasted_iota(jnp.int32, s_new.shape, 0)
    ki = lax.broadcasted_iota(jnp.int32, s_new.shape, 1)
    s_new = jnp.where(ki <= jnp.where(qi >= lq, qi - lq, qi), s_new, NEG_INF)
    m = jnp.maximum(jnp.max(s_past, axis=1, keepdims=True), jnp.max(s_new, axis=1, keepdims=True))
    p_past = jnp.exp2(s_past - m)
    p_new = jnp.exp2(s_new - m)
    denom = jnp.sum(p_past, axis=1, keepdims=True) + jnp.sum(p_new, axis=1, keepdims=True)
    pb_new = p_new.astype(BF16)
    acc = _dot_nt(p_past.astype(BF16), vp_ref[0].astype(BF16))
    acc_a = (acc[0:lq] + _dot(pb_new[0:lq], vn_ref[:, 0:LANES])) / denom[0:lq]
    acc_b = (acc[lq:2 * lq] + _dot(pb_new[lq:2 * lq], vn_ref[:, LANES:2 * LANES])) / denom[lq:2 * lq]
    o_ref[...] = jnp.where(lane < FOX_D, acc_a, acc_b).astype(BF16)


def _fox_cached(fq, k_past, v_past, k_new, v_new, c_all, batch, lq):
    past = k_past.shape[2]
    new_rows = lambda w: pl.BlockSpec((lq, w), lambda b, h: (b, h))
    cache = pl.BlockSpec((1, LANES, past), lambda b, h: (b, h, 0))
    return pl.pallas_call(
        functools.partial(_fox_cached_body, past=past, lq=lq),
        grid=(batch, FOX_H // 2),
        in_specs=[new_rows(LANES), cache, cache, new_rows(LANES), new_rows(2 * LANES),
                  pl.BlockSpec((1, 1, 2, c_all.shape[-1]), lambda b, h: (b, h, 0, 0))],
        out_specs=new_rows(LANES),
        out_shape=jax.ShapeDtypeStruct((batch * lq, FOX_W), BF16),
        compiler_params=_params(("parallel", "parallel")),
        name="fox_attention_cached",
    )(fq, k_past, v_past, k_new, v_new, c_all)


def _normproj_body(x_ref, g_ref, w_ref, o_ref, ob_ref):
    y = _dot(_rms(x_ref[...], g_ref[...]).astype(BF16), w_ref[...])
    o_ref[...] = y
    ob_ref[...] = y.astype(BF16)


def _normproj(x, g, w, tm):
    n, wout = x.shape[0], w.shape[1]
    return pl.pallas_call(
        _normproj_body,
        grid=(n // tm,),
        in_specs=[pl.BlockSpec((tm, D_MODEL), lambda i: (i, 0)), _const_spec(g.shape), _const_spec(w.shape)],
        out_specs=[pl.BlockSpec((tm, wout), lambda i: (i, 0))] * 2,
        out_shape=[jax.ShapeDtypeStruct((n, wout), F32), jax.ShapeDtypeStruct((n, wout), BF16)],
        compiler_params=_params(("parallel",)),
        name="memory_kv",
    )(x, g, w)


def _post_body(*refs, has_mix, parts, seq_rows, mem_len):
    if has_mix:
        (x_ref, ro_ref, fo_ref, wo_ref, nw_ref, wxq_ref, wxo_ref, mk_ref, mv_ref, wup_ref, wdn_ref,
         o_ref) = refs
    else:
        x_ref, nw_ref, wxq_ref, wxo_ref, mk_ref, mv_ref, wup_ref, wdn_ref, o_ref = refs
    rows = [slice(r0, r0 + nrows) for r0, nrows in parts]
    xs = [x_ref[rs, :] for rs in rows]
    if has_mix:
        mixes = [_dot(ro_ref[rs, :], wo_ref[0:RET_W, :]) + _dot(fo_ref[rs, :], wo_ref[RET_W:RET_W + FOX_W, :])
                 for rs in rows]
        xs = [x + _rms(mix, nw_ref[1:2, :]) for x, mix in zip(xs, mixes)]

    qs = [_dot(_rms(x, nw_ref[2:3, :]).astype(BF16), wxq_ref[...]) for x in xs]
    ctx = []
    for (r0, nrows), q in zip(parts, qs):
        groups = []
        for g0 in range(0, nrows, min(seq_rows, nrows)):
            gs = slice(g0, g0 + min(seq_rows, nrows))
            m0 = ((r0 + g0) // seq_rows) * mem_len
            heads = []
            for hh in range(X_H):
                sl = slice(hh * X_D, (hh + 1) * X_D)
                s = _dot_nt(q[gs, sl].astype(BF16), mk_ref[m0:m0 + mem_len, sl]) * (X_D ** -0.5)
                p = jnp.exp(s - jnp.max(s, axis=-1, keepdims=True))
                o = _dot(p.astype(BF16), mv_ref[m0:m0 + mem_len, sl]) / jnp.sum(p, axis=-1, keepdims=True)
                heads.append(o.astype(BF16))
            groups.append(jnp.concatenate(heads, axis=1))
        ctx.append(groups[0] if len(groups) == 1 else jnp.concatenate(groups, axis=0))
    atts = [_dot(c, wxo_ref[...]) for c in ctx]
    xs = [x + _rms(att, nw_ref[3:4, :]) for x, att in zip(xs, atts)]

    hs = [_rms(x, nw_ref[4:5, :]).astype(BF16) for x in xs]
    accs = [None] * len(parts)
    for c in range(D_FF // D_MODEL):
        sl = slice(c * D_MODEL, (c + 1) * D_MODEL)
        for i, h in enumerate(hs):
            r = jnp.maximum(_dot(h, wup_ref[:, sl]), 0.0)
            part = _dot((r * r).astype(BF16), wdn_ref[sl, :])
            accs[i] = part if accs[i] is None else accs[i] + part
    for rs, x, acc in zip(rows, xs, accs):
        o_ref[rs, :] = x + _rms(acc, nw_ref[5:6, :])


def _post(x, mix_in, nw, wxq, wxo, mk, mv, wup, wdn, seq, tm):
    n = x.shape[0]
    mem_len = mk.shape[0] // (n // seq)
    row = lambda w: pl.BlockSpec((tm, w), lambda i: (i, 0))
    if tm <= seq:
        per_batch = seq // tm
        mem = pl.BlockSpec((mem_len, D_MODEL), lambda i: (i // per_batch, 0))
        seq_rows = tm
        nparts = POST_PARTS if tm % (POST_PARTS * 16) == 0 else 1
        parts = tuple((p * (tm // nparts), tm // nparts) for p in range(nparts))
    else:
        mem = pl.BlockSpec(((tm // seq) * mem_len, D_MODEL), lambda i: (i, 0))
        seq_rows = seq
        parts = ((0, tm),)
    args = [x]
    specs = [row(D_MODEL)]
    if mix_in is not None:
        ro, fo, wo = mix_in
        args += [ro, fo, wo]
        specs += [row(RET_W), row(FOX_W), _const_spec(wo.shape)]
    args += [nw, wxq, wxo, mk, mv, wup, wdn]
    specs += [_const_spec(nw.shape), _const_spec(wxq.shape), _const_spec(wxo.shape), mem, mem,
              _const_spec(wup.shape), _const_spec(wdn.shape)]
    return pl.pallas_call(
        functools.partial(_post_body, has_mix=mix_in is not None, parts=parts, seq_rows=seq_rows, mem_len=mem_len),
        grid=(n // tm,),
        in_specs=specs,
        out_specs=row(D_MODEL),
        out_shape=jax.ShapeDtypeStruct((n, D_MODEL), F32),
        compiler_params=_params(("parallel",)),
        name="xattn_mlp_mix" if mix_in is not None else "xattn_mlp",
    )(*args)


def _s5prep_body(lre_ref, lim_ref, ldt_ref, lre_rep_ref, lim_rep_ref, bre_ref, bim_ref,
                 are_ref, aim_ref, bbre_ref, bbim_ref):
    dt = jnp.exp(ldt_ref[...])

    def zoh(lre, lim):
        mag = jnp.exp(lre * dt)
        a_re = mag * jnp.cos(lim * dt)
        a_im = mag * jnp.sin(lim * dt)
        den = lre * lre + lim * lim
        num_re = a_re - 1.0
        f_re = (num_re * lre + a_im * lim) / den
        f_im = (a_im * lre - num_re * lim) / den
        return a_re, a_im, f_re, f_im

    a_re, a_im, _, _ = zoh(lre_ref[...], lim_ref[...])
    are_ref[...] = a_re
    aim_ref[...] = a_im
    _, _, f_re, f_im = zoh(lre_rep_ref[...], lim_rep_ref[...])
    b_re = bre_ref[...]
    b_im = bim_ref[...]
    bbre_ref[...] = f_re * b_re - f_im * b_im
    bbim_ref[...] = f_re * b_im + f_im * b_re


def _s5_discretize(lam_re, lam_im, log_dt, b_re, b_im):
    rep = lambda a: jnp.repeat(a, S5_GROUP, axis=1)
    wide = jax.ShapeDtypeStruct((S5_G, S5_P * S5_GROUP), F32)
    small = jax.ShapeDtypeStruct((S5_G, S5_P), F32)
    a_re, a_im, bb_re, bb_im = pl.pallas_call(
        _s5prep_body, out_shape=[small, small, wide, wide], name="s5_discretize",
    )(lam_re, lam_im, log_dt.reshape(S5_G, 1), rep(lam_re), rep(lam_im),
      b_re.reshape(S5_G, -1), b_im.reshape(S5_G, -1))
    a = jnp.stack([a_re.reshape(-1), a_im.reshape(-1)])
    gpc = S5_LANE_CHUNK // S5_P
    nch = S5_G // gpc
    eye = jnp.eye(gpc, dtype=F32)

    def in_blocks(bb):
        t = bb.reshape(nch, gpc, S5_P, S5_GROUP).transpose(0, 1, 3, 2)
        return (t[:, :, :, None, :] * eye[None, :, None, :, None]).reshape(nch, gpc * S5_GROUP, gpc * S5_P)

    bmat = jnp.concatenate([in_blocks(bb_re), in_blocks(bb_im)], axis=-1).astype(BF16)
    return a, bmat


def _s5_out_matrices(c_re, c_im):
    gpc = S5_LANE_CHUNK // S5_P
    nch = S5_G // gpc
    eye = jnp.eye(gpc, dtype=F32)

    def out_blocks(c):
        t = c.reshape(nch, gpc, S5_GROUP, S5_P).transpose(0, 1, 3, 2)
        return (t[:, :, :, None, :] * eye[None, :, None, :, None]).reshape(nch, gpc * S5_P, gpc * S5_GROUP)

    return jnp.concatenate([out_blocks(c_re), -out_blocks(c_im)], axis=1).astype(BF16)


def _s5_body(x_ref, nw_ref, perm_ref, win_ref, bmat_ref, a_ref, cmat_ref, d_ref, wglu_ref, st0_ref,
             o_ref, st_ref, u_scr, bu_scr, y_scr, *, ts):
    nb = x_ref.shape[0]
    rows = nb * ts
    lc = S5_LANE_CHUNK
    gw = (lc // S5_P) * S5_GROUP

    @pl.when(pl.program_id(0) == 0)
    def _():
        st_ref[...] = st0_ref[...]

    x = x_ref[...].reshape(rows, D_MODEL)
    h = _dot(perm_ref[0], _rms(x, nw_ref[0:1, :]).astype(BF16)).astype(BF16)
    u_scr[...] = _dot(h, win_ref[...])
    nv = lc // LANES
    for j in range(S5_STATE // lc):
        uj = u_scr[:, j * gw:(j + 1) * gw]
        bu = _dot(uj.astype(BF16), bmat_ref[j])
        for c in range(2 * nv):
            bu_scr[c] = bu[:, c * LANES:(c + 1) * LANES]
        a_re = [jnp.broadcast_to(a_ref[0:1, j * lc + c * LANES:j * lc + (c + 1) * LANES], (nb, LANES))
                for c in range(nv)]
        a_im = [jnp.broadcast_to(a_ref[1:2, j * lc + c * LANES:j * lc + (c + 1) * LANES], (nb, LANES))
                for c in range(nv)]

        def step(t, carry):
            idx = pl.ds(pl.multiple_of(t * nb, nb), nb)
            new = []
            for c in range(nv):
                s_re, s_im = carry[c], carry[nv + c]
                new.append(a_re[c] * s_re - a_im[c] * s_im + bu_scr[c, idx, :])
            for c in range(nv):
                s_re, s_im = carry[c], carry[nv + c]
                new.append(a_re[c] * s_im + a_im[c] * s_re + bu_scr[nv + c, idx, :])
            for c in range(2 * nv):
                bu_scr[c, idx, :] = new[c]
            return tuple(new)

        init = tuple(st_ref[:, j * lc + c * LANES:j * lc + (c + 1) * LANES] for c in range(nv)) + tuple(
            st_ref[:, S5_STATE + j * lc + c * LANES:S5_STATE + j * lc + (c + 1) * LANES] for c in range(nv))
        fin = lax.fori_loop(0, ts, step, init, unroll=True)
        for c in range(nv):
            st_ref[:, j * lc + c * LANES:j * lc + (c + 1) * LANES] = fin[c]
            st_ref[:, S5_STATE + j * lc + c * LANES:S5_STATE + j * lc + (c + 1) * LANES] = fin[nv + c]
        xs = jnp.concatenate([bu_scr[c] for c in range(2 * nv)], axis=1).astype(BF16)
        y_scr[:, j * gw:(j + 1) * gw] = _dot(xs, cmat_ref[j]) + d_ref[0:1, j * gw:(j + 1) * gw] * uj
    y = _dot(perm_ref[1], y_scr[...].astype(BF16)).astype(BF16)
    g = _dot(y, wglu_ref[...])
    out = g[:, 0:D_MODEL] * jax.nn.sigmoid(g[:, D_MODEL:2 * D_MODEL])
    o_ref[...] = (x + _rms(out, nw_ref[1:2, :])).reshape(nb, ts, D_MODEL)


def _s5_layer(x3, nw, win, bmat, a, cmat, d, wglu, st0, ts):
    nb, seq, _ = x3.shape
    rows = nb * ts
    blk = pl.BlockSpec((nb, ts, D_MODEL), lambda t: (0, t, 0))
    r = np.arange(rows)
    fwd = np.zeros((rows, rows), np.float32)
    fwd[(r % ts) * nb + r // ts, r] = 1.0
    perm = jnp.asarray(np.stack([fwd, fwd.T]), BF16)
    consts = [nw, perm, win, bmat, a, cmat, d, wglu, st0]
    return pl.pallas_call(
        functools.partial(_s5_body, ts=ts),
        grid=(seq // ts,),
        in_specs=[blk] + [_const_spec(c.shape) for c in consts],
        out_specs=[blk, pl.BlockSpec(st0.shape, lambda t: (0, 0))],
        out_shape=[jax.ShapeDtypeStruct(x3.shape, F32), jax.ShapeDtypeStruct(st0.shape, F32)],
        scratch_shapes=[pltpu.VMEM((rows, D_MODEL), F32),
                        pltpu.VMEM((2 * S5_LANE_CHUNK // LANES, rows, LANES), F32),
                        pltpu.VMEM((rows, D_MODEL), F32)],
        compiler_params=_params(("arbitrary",)),
        name="s5_layer",
    )(x3, *consts)


def _rotary_tables(pos, rows):
    half = RET_D // 2
    inv = ROPE_BASE ** (-jnp.arange(half, dtype=F32) / half)
    ang = pos.astype(F32)[:, None] * inv[None, :]
    cos = jnp.cos(ang)
    sin = jnp.sin(ang)
    cos_t = jnp.concatenate([cos, cos], axis=1)
    sin_t = jnp.concatenate([-sin, sin], axis=1)
    reps = max(1, rows // pos.shape[0])
    return jnp.tile(cos_t, (reps, 1)), jnp.tile(sin_t, (reps, 1))


def _prepare_weights(norm_w, w_in_ab, b_fox_f, w_out_ab, w_in_c, s5_lambda_re, s5_lambda_im, s5_log_dt,
                     s5_b_re, s5_b_im, s5_c_re, s5_c_im, s5_d, w_glu, w_xq, w_xo, w_up, w_down):
    w = {}
    wab = w_in_ab[0].astype(BF16)
    w["wr"] = wab[:, 0:4 * RET_W]
    w["wf"] = wab[:, 4 * RET_W:4 * RET_W + 3 * FOX_W]
    w["wl"] = jnp.pad(wab[:, 4 * RET_W + 3 * FOX_W:], ((0, 0), (0, LANES - FOX_H)))
    w["bl"] = jnp.pad(b_fox_f[0].astype(F32), (0, LANES - FOX_H)).reshape(1, LANES)
    w["wo"] = w_out_ab[0].astype(BF16)
    w["nw"] = norm_w.astype(F32)
    w["win_c"] = w_in_c[0].astype(BF16)
    w["a"], w["bmat"] = _s5_discretize(s5_lambda_re[0], s5_lambda_im[0], s5_log_dt[0], s5_b_re[0], s5_b_im[0])
    w["cmat"] = _s5_out_matrices(s5_c_re[0].astype(F32), s5_c_im[0].astype(F32))
    w["d"] = s5_d[0].astype(F32).reshape(1, D_MODEL)
    w["wglu"] = w_glu[0].astype(BF16)
    for name, arr in (("wxq", w_xq), ("wxo", w_xo), ("wup", w_up), ("wdn", w_down)):
        w[name] = arr.astype(BF16)
    return w


def _run_group(x3, pos, w, gn_w, mem_kv, ret_s0, fox_past, s5_st0):
    batch, seq, _ = x3.shape
    n = batch * seq
    tm = min(ROW_TILE, n)
    tm_b = ROW_TILE if seq >= ROW_TILE else min(n, (ROW_TILE // seq) * seq)
    x = x3.reshape(n, D_MODEL)

    cos_t, sin_t = _rotary_tables(pos, tm)
    rq, rk, rv, rg, fq, fkb, fvp, fk, fv, lf = _inproj(
        x, w["nw"][0, 0:1], w["wr"], w["wf"], w["wl"], w["bl"], cos_t, sin_t, tm, seq)
    if tm <= seq:
        lf_t = lf
        fk = jnp.transpose(fk, (0, 3, 1, 2))
        fv = jnp.transpose(fv, (0, 3, 1, 2))
        lf3 = jnp.transpose(lf, (0, 2, 1))
    else:
        lf3 = lf.reshape(batch, seq, FOX_H)
        lf_t = jnp.transpose(lf3, (0, 2, 1))
    if fox_past is not None:
        lf_t = jnp.concatenate([jnp.transpose(fox_past[2].astype(F32), (0, 2, 1)), lf_t], axis=2)
    lk = lf_t.shape[2]
    pad = -lk % CUMSUM_CHUNK
    lf_rows = jnp.pad(lf_t.reshape(batch * FOX_H, lk), ((0, 0), (0, pad)))
    c_all = _cumsum_lanes(lf_rows).reshape(batch, FOX_H // 2, 2, lk + pad)
    if fox_past is None:
        fo = _fox(fq, fkb.reshape(batch, seq, FOX_W), fvp.reshape(batch, seq, FOX_H * LANES), c_all,
                  batch, seq, FOX_TILE)
    else:
        past = fox_past[0].shape[1]
        cache_t = lambda a: jnp.transpose(a, (0, 2, 3, 1)).reshape(batch, FOX_W, past)
        fo = _fox_cached(fq, cache_t(fox_past[0]), cache_t(fox_past[1]), fkb, fvp, c_all, batch, seq)

    chunk = CHUNK if fox_past is None else seq
    ro, ret_state = _retention(rq, rk, rv, rg, gn_w, ret_s0, batch, seq, chunk)

    x = _post(x, (ro, fo, w["wo"]), w["nw"][0], w["wxq"][0], w["wxo"][0], mem_kv[0][0], mem_kv[0][1],
              w["wup"][0], w["wdn"][0], seq, tm_b)

    ts = min(S5_TS, seq)
    x3n, s5_state = _s5_layer(x.reshape(batch, seq, D_MODEL), w["nw"][1], w["win_c"], w["bmat"], w["a"],
                              w["cmat"], w["d"], w["wglu"], s5_st0, ts)
    x = _post(x3n.reshape(n, D_MODEL), None, w["nw"][1], w["wxq"][1], w["wxo"][1], mem_kv[1][0], mem_kv[1][1],
              w["wup"][1], w["wdn"][1], seq, tm_b)

    new = dict(ret=ret_state[None], fox_k=fk.reshape(1, batch, seq, FOX_H, FOX_D),
               fox_v=fv.reshape(1, batch, seq, FOX_H, FOX_D), fox_logf=lf3[None],
               s5_re=s5_state[:, 0:S5_STATE].reshape(1, batch, S5_G, S5_P),
               s5_im=s5_state[:, S5_STATE:].reshape(1, batch, S5_G, S5_P))
    return x.reshape(batch, seq, D_MODEL), new


def kernel(x_prompt, x_sample, cache_ret_state, cache_fox_k, cache_fox_v, cache_fox_logf, state_s5_re, state_s5_im, cache_mem_k, cache_mem_v, mem_prompt, norm_w, w_in_ab, b_fox_f, ret_gn_w, w_out_ab, w_in_c, s5_lambda_re, s5_lambda_im, s5_log_dt, s5_b_re, s5_b_im, s5_c_re, s5_c_im, s5_d, w_glu, mem_norm_w, w_xq, w_xk, w_xv, w_xo, w_up, w_down):
    w = _prepare_weights(norm_w, w_in_ab, b_fox_f, w_out_ab, w_in_c, s5_lambda_re, s5_lambda_im, s5_log_dt,
                         s5_b_re, s5_b_im, s5_c_re, s5_c_im, s5_d, w_glu, w_xq, w_xo, w_up, w_down)
    batch, seq, _ = x_prompt.shape
    dec_batch, dec_seq, _ = x_sample.shape
    mem_len = mem_prompt.shape[1]
    depth = norm_w.shape[0]
    gn_w = ret_gn_w[0].astype(F32)

    mem_rows = mem_prompt.reshape(batch * mem_len, D_MODEL)
    mem_kv_prompt = []
    mem_k_out = []
    mem_v_out = []
    for layer in range(depth):
        wkv = jnp.concatenate([w_xk[layer], w_xv[layer]], axis=1).astype(BF16)
        kv, kvb = _normproj(mem_rows, mem_norm_w[layer].astype(F32).reshape(1, D_MODEL), wkv,
                            min(ROW_TILE, batch * mem_len))
        mem_kv_prompt.append((kvb[:, 0:D_MODEL], kvb[:, D_MODEL:]))
        mem_k_out.append(kv[:, 0:D_MODEL].reshape(batch, mem_len, X_H, X_D))
        mem_v_out.append(kv[:, D_MODEL:].reshape(batch, mem_len, X_H, X_D))

    pos_prompt = jnp.arange(seq)
    y_prompt, np_ = _run_group(
        x_prompt, pos_prompt, w, gn_w, mem_kv_prompt,
        jnp.zeros((batch, RET_H, RET_D, RET_D), F32), None, jnp.zeros((batch, 2 * S5_STATE), F32))

    past = cache_fox_k.shape[2]
    pos_sample = past + jnp.arange(dec_seq)
    mem_kv_sample = [(cache_mem_k[layer].reshape(dec_batch * mem_len, D_MODEL).astype(BF16),
                      cache_mem_v[layer].reshape(dec_batch * mem_len, D_MODEL).astype(BF16))
                     for layer in range(depth)]
    s5_st0 = jnp.concatenate([state_s5_re[0].reshape(dec_batch, S5_STATE),
                              state_s5_im[0].reshape(dec_batch, S5_STATE)], axis=1).astype(F32)
    y_sample, ns_ = _run_group(
        x_sample, pos_sample, w, gn_w, mem_kv_sample, cache_ret_state[0].astype(F32),
        (cache_fox_k[0], cache_fox_v[0], cache_fox_logf[0]), s5_st0)

    return (y_prompt, y_sample,
            np_["ret"], ns_["ret"],
            np_["fox_k"], np_["fox_v"], np_["fox_logf"],
            ns_["fox_k"], ns_["fox_v"], ns_["fox_logf"],
            np_["s5_re"], np_["s5_im"], ns_["s5_re"], ns_["s5_im"],
            jnp.stack(mem_k_out), jnp.stack(mem_v_out))
```

```python
import functools

import numpy as np
import jax
import jax.numpy as jnp
from jax import lax
from jax.experimental import pallas as pl
from jax.experimental.pallas import tpu as pltpu

F32 = jnp.float32
BF16 = jnp.bfloat16

D_MODEL = 1024
RET_H = 4
RET_D = 128
RET_W = RET_H * RET_D
CHUNK = 64
ROPE_BASE = 10000.0
FOX_H = 8
FOX_D = 64
FOX_W = FOX_H * FOX_D
S5_GROUP = 16
S5_G = D_MODEL // S5_GROUP
S5_P = 64
S5_STATE = S5_G * S5_P
X_H = 4
X_D = D_MODEL // X_H
D_FF = 4 * D_MODEL
EPS = 1e-6
NEG_INF = -1e30
LOG2E = float(np.log2(np.e))

LANES = 128
VMEM_LIMIT = 56 * 1024 * 1024

ROW_TILE = 512
POST_PARTS = 2
RET_BLOCK = 256
FOX_TILE = 512
FOX_UNROLL = 4
FOX_DROP_LOG2 = 152.0
FOX_NORM_SLACK = 1.02
S5_TS = 64
S5_LANE_CHUNK = 512
CUMSUM_CHUNK = 256


def _dot(a, b):
    return jnp.dot(a, b, preferred_element_type=F32)


def _dot_nt(a, b):
    return lax.dot_general(a, b, (((1,), (1,)), ((), ())), preferred_element_type=F32)


def _dot_tn(a, b):
    return lax.dot_general(a, b, (((0,), (0,)), ((), ())), preferred_element_type=F32)


def _rms(x, w):
    return x * lax.rsqrt(jnp.mean(x * x, axis=-1, keepdims=True) + EPS) * w


def _const_spec(shape):
    nd = len(shape)
    return pl.BlockSpec(shape, lambda *_: (0,) * nd, pipeline_mode=pl.Buffered(1))


def _params(sem):
    return pltpu.CompilerParams(dimension_semantics=sem, vmem_limit_bytes=VMEM_LIMIT)


def _inproj_body(x_ref, g_ref, wr_ref, wf_ref, wl_ref, bl_ref, cos_ref, sin_ref, hsum_ref,
                 rq_ref, rk_ref, rv_ref, rg_ref, fq_ref, fkb_ref, fvp_ref, fk_ref, fv_ref, lf_ref,
                 qn_ref, kn_ref, *, head_major):
    def put_kv(dst_ref, z):
        if head_major:
            dst_ref[0] = z.T.reshape(FOX_H, FOX_D, z.shape[0])
        else:
            dst_ref[...] = z

    h = _rms(x_ref[...], g_ref[...]).astype(BF16)
    cos = cos_ref[...]
    sin = sin_ref[...]

    def rotary_to(dst_ref, z, scale):
        for hh in range(RET_H):
            sl = slice(hh * RET_D, (hh + 1) * RET_D)
            zz = z[:, sl]
            r = zz * cos + pltpu.roll(zz, RET_D // 2, 1) * sin
            if scale is not None:
                r = r * scale
            dst_ref[:, sl] = r.astype(BF16)

    rotary_to(rq_ref, _dot(h, wr_ref[:, 0:RET_W]), None)
    rotary_to(rk_ref, _dot(h, wr_ref[:, RET_W:2 * RET_W]), RET_D ** -0.5)
    rv_ref[...] = _dot(h, wr_ref[:, 2 * RET_W:3 * RET_W]).astype(BF16)
    rg_ref[...] = _dot(h, wr_ref[:, 3 * RET_W:4 * RET_W])
    def max_sq_norm(zb):
        zf = zb.astype(F32)
        return jnp.max(_dot((zf * zf).astype(BF16), hsum_ref[...]), axis=0, keepdims=True)

    fqb = (_dot(h, wf_ref[:, 0:FOX_W]) * (LOG2E * FOX_D ** -0.5)).astype(BF16)
    fq_ref[...] = fqb
    qn_ref[0] = max_sq_norm(fqb)
    fk = _dot(h, wf_ref[:, FOX_W:2 * FOX_W])
    put_kv(fk_ref, fk)
    fkb = fk.astype(BF16)
    fkb_ref[...] = fkb
    kn_ref[0] = max_sq_norm(fkb)
    fv = _dot(h, wf_ref[:, 2 * FOX_W:3 * FOX_W])
    put_kv(fv_ref, fv)
    lane = lax.broadcasted_iota(jnp.int32, (fv.shape[0], LANES), 1)
    for pr in range(FOX_H // 2):
        vp = fv[:, pr * LANES:(pr + 1) * LANES]
        fvp_ref[:, (2 * pr) * LANES:(2 * pr + 1) * LANES] = jnp.where(lane < FOX_D, vp, 1.0).astype(BF16)
        fvp_ref[:, (2 * pr + 1) * LANES:(2 * pr + 2) * LANES] = jnp.where(lane >= FOX_D, vp, 1.0).astype(BF16)
    zl = _dot(h, wl_ref[...]) + bl_ref[...]
    lf = jnp.minimum(zl, 0.0) - jnp.log1p(jnp.exp(-jnp.abs(zl)))
    if head_major:
        lf_ref[0] = lf.T[0:FOX_H, :]
    else:
        lf_ref[...] = lf[:, 0:FOX_H]


def _inproj(x, g, wr, wf, wl, bl, cos_t, sin_t, tm, seq):
    n = x.shape[0]
    nt = cos_t.shape[0] // tm
    head_major = tm <= seq
    row = lambda w: pl.BlockSpec((tm, w), lambda i: (i, 0))
    tab = pl.BlockSpec((tm, LANES), lambda i: (i % nt, 0))
    shp = lambda w, dt: jax.ShapeDtypeStruct((n, w), dt)
    hsum = jnp.asarray(np.arange(FOX_W)[:, None] // FOX_D == np.arange(LANES)[None, :], BF16)
    stat_spec = pl.BlockSpec((1, 1, LANES), lambda i: (i, 0, 0))
    stat_shape = jax.ShapeDtypeStruct((n // tm, 1, LANES), F32)
    if head_major:
        per_batch = seq // tm
        kv_spec = pl.BlockSpec((1, FOX_H, FOX_D, tm), lambda i: (i // per_batch, 0, 0, i % per_batch))
        kv_shape = jax.ShapeDtypeStruct((n // seq, FOX_H, FOX_D, seq), F32)
        lf_spec = pl.BlockSpec((1, FOX_H, tm), lambda i: (i // per_batch, 0, i % per_batch))
        lf_shape = jax.ShapeDtypeStruct((n // seq, FOX_H, seq), F32)
    else:
        kv_spec, kv_shape = row(FOX_W), shp(FOX_W, F32)
        lf_spec, lf_shape = row(FOX_H), shp(FOX_H, F32)
    return pl.pallas_call(
        functools.partial(_inproj_body, head_major=head_major),
        grid=(n // tm,),
        in_specs=[row(D_MODEL), _const_spec(g.shape), _const_spec(wr.shape), _const_spec(wf.shape),
                  _const_spec(wl.shape), _const_spec(bl.shape), tab, tab, _const_spec(hsum.shape)],
        out_specs=[row(RET_W), row(RET_W), row(RET_W), row(RET_W), row(FOX_W), row(FOX_W),
                   row(FOX_H * LANES), kv_spec, kv_spec, lf_spec, stat_spec, stat_spec],
        out_shape=[shp(RET_W, BF16), shp(RET_W, BF16), shp(RET_W, BF16), shp(RET_W, F32),
                   shp(FOX_W, BF16), shp(FOX_W, BF16), shp(FOX_H * LANES, BF16), kv_shape, kv_shape,
                   lf_shape, stat_shape, stat_shape],
        compiler_params=_params(("parallel",)),
        name="inproj_ab",
    )(x, g, wr, wf, wl, bl, cos_t, sin_t, hsum)


def _cumsum_body(x_ref, u_ref, o_ref):
    rows, total = x_ref.shape
    u = u_ref[...]

    def chunk(i, carry):
        off = pl.multiple_of(i * CUMSUM_CHUNK, CUMSUM_CHUNK)
        xc = x_ref[:, pl.ds(off, CUMSUM_CHUNK)]
        hi = xc.astype(BF16)
        r1 = xc - hi.astype(F32)
        mid = r1.astype(BF16)
        lo = (r1 - mid.astype(F32)).astype(BF16)
        y = _dot(hi, u) + _dot(mid, u) + _dot(lo, u) + carry
        o_ref[:, pl.ds(off, CUMSUM_CHUNK)] = y * LOG2E
        return y[:, CUMSUM_CHUNK - 1:CUMSUM_CHUNK]

    lax.fori_loop(0, total // CUMSUM_CHUNK, chunk, jnp.zeros((rows, 1), F32))


def _cumsum_lanes(x):
    idx = np.arange(CUMSUM_CHUNK)
    u = jnp.asarray(idx[:, None] <= idx[None, :], BF16)
    return pl.pallas_call(
        _cumsum_body,
        out_shape=jax.ShapeDtypeStruct(x.shape, F32),
        compiler_params=pltpu.CompilerParams(vmem_limit_bytes=VMEM_LIMIT),
        name="logf_cumsum",
    )(x, u)


def _ret_body(q_ref, k_ref, v_ref, rg_ref, dm_ref, wq_ref, wk_ref, dec_ref, gn_ref, s0_ref,
              ro_ref, s_ref):
    @pl.when(pl.program_id(1) == 0)
    def _():
        s_ref[...] = s0_ref[...]

    heads = range(RET_H)
    sls = [slice(hh * RET_D, (hh + 1) * RET_D) for hh in heads]
    qs = [q_ref[:, sl] for sl in sls]
    ks = [k_ref[:, sl] for sl in sls]
    vs = [v_ref[:, sl] for sl in sls]
    states = [s_ref[0, hh] for hh in heads]
    scores = [(_dot_nt(qs[hh], ks[hh]) * dm_ref[hh]).astype(BF16) for hh in heads]
    inter = [_dot(qs[hh], states[hh].astype(BF16)) * wq_ref[hh] for hh in heads]
    outs = [_dot(scores[hh], vs[hh]) + inter[hh] for hh in heads]
    kws = [(ks[hh].astype(F32) * wk_ref[hh]).astype(BF16) for hh in heads]
    for hh in heads:
        s_ref[0, hh] = states[hh] * dec_ref[hh] + _dot_tn(kws[hh], vs[hh])
    for hh in heads:
        o = outs[hh]
        mu = jnp.mean(o, axis=-1, keepdims=True)
        oc = o - mu
        var = jnp.mean(oc * oc, axis=-1, keepdims=True)
        g = rg_ref[:, sls[hh]]
        ro_ref[:, sls[hh]] = (oc * lax.rsqrt(var + EPS) * gn_ref[hh] * (g * jax.nn.sigmoid(g))).astype(BF16)


def _retention(rq, rk, rv, rg, gn_w, s0, batch, seq, chunk):
    tb = min(RET_BLOCK, seq)
    nb = seq // tb
    lg = np.log(1.0 - 2.0 ** (-5.0 - np.arange(RET_H, dtype=np.float64)))
    idx = np.arange(tb, dtype=np.float64)
    visible = (idx[None, :] // chunk) <= (idx[:, None] // chunk)
    dm = np.exp(lg[:, None, None] * np.abs(idx[:, None] - idx[None, :])) * visible[None]
    wq = np.broadcast_to(np.exp(lg[:, None] * (idx[None, :] + 1.0))[:, :, None], (RET_H, tb, RET_D))
    wk = np.broadcast_to(np.exp(lg[:, None] * (tb - 1.0 - idx[None, :]))[:, :, None], (RET_H, tb, RET_D))
    dec = np.broadcast_to(np.exp(lg * tb)[:, None, None], (RET_H, 1, RET_D))
    consts = [jnp.asarray(a, F32) for a in (dm, wq, wk, dec)]
    gn = gn_w.reshape(RET_H, 1, RET_D)
    row = pl.BlockSpec((tb, RET_W), lambda b, t: (b * nb + t, 0))
    st = pl.BlockSpec((1, RET_H, RET_D, RET_D), lambda b, t: (b, 0, 0, 0))
    return pl.pallas_call(
        _ret_body,
        grid=(batch, nb),
        in_specs=[row, row, row, row] + [_const_spec(c.shape) for c in consts] + [_const_spec(gn.shape), st],
        out_specs=[row, st],
        out_shape=[jax.ShapeDtypeStruct((batch * seq, RET_W), BF16),
                   jax.ShapeDtypeStruct((batch, RET_H, RET_D, RET_D), F32)],
        compiler_params=_params(("parallel", "arbitrary")),
        name="retention",
    )(rq, rk, rv, rg, *consts, gn, s0)


def _fox_body(start_ref, q_ref, k_ref, v_ref, c_ref, mask_ref, o_ref, qs_scr, m_scr, acc_scr, *, tq, tk):
    n_full = pl.program_id(2)
    first = start_ref[(pl.program_id(0) * pl.num_programs(1) + pl.program_id(1)) * pl.num_programs(2) + n_full]
    count = n_full - first
    q = q_ref[...]
    lane = lax.broadcasted_iota(jnp.int32, q.shape, 1)
    zero = jnp.zeros_like(q)
    qs_scr[0:tq, :] = jnp.where(lane < FOX_D, q, zero)
    qs_scr[tq:2 * tq, :] = jnp.where(lane >= FOX_D, q, zero)
    m_scr[...] = jnp.full(m_scr.shape, -jnp.inf, F32)
    acc_scr[...] = jnp.zeros(acc_scr.shape, F32)

    def tile(kj, masked):
        off = pl.multiple_of(kj * tk, tk)
        s = _dot_nt(qs_scr[...], k_ref[0, pl.ds(off, tk), :])
        hrows = [slice(hh * tq, (hh + 1) * tq) for hh in range(2)]
        shs = [s[hrows[hh], :] - c_ref[0, 0, hh:hh + 1, pl.ds(off, tk)] for hh in range(2)]
        if masked:
            shs = [sh + mask_ref[...] for sh in shs]
        chunks = [[sh[:, c * LANES:(c + 1) * LANES] for c in range(tk // LANES)] for sh in shs]
        m_prevs = [m_scr[rows, :] for rows in hrows]
        m_news = [jnp.maximum(m_prevs[hh], jnp.max(functools.reduce(jnp.maximum, chunks[hh]), axis=1, keepdims=True))
                  for hh in range(2)]
        ps = [jnp.concatenate([jnp.exp2(ch - m_news[hh]) for ch in chunks[hh]], axis=1).astype(BF16)
              for hh in range(2)]
        pvs = [_dot(ps[hh], v_ref[0, pl.ds(off, tk), hh * LANES:(hh + 1) * LANES]) for hh in range(2)]
        for hh in range(2):
            acc_scr[hrows[hh], :] = jnp.exp2(m_prevs[hh] - m_news[hh]) * acc_scr[hrows[hh], :] + pvs[hh]
            m_scr[hrows[hh], :] = m_news[hh]

    def tile_group(i, carry):
        for j in range(FOX_UNROLL):
            tile(first + FOX_UNROLL * i + j, False)
        return carry

    lax.fori_loop(0, count // FOX_UNROLL, tile_group, 0)
    span = FOX_UNROLL // 2
    while span >= 1:
        @pl.when((count // span) % 2 == 1)
        def _(span=span):
            base = first + (count // (2 * span)) * (2 * span)
            for j in range(span):
                tile(base + j, False)
        span //= 2

    tile(n_full, True)
    acc_a = acc_scr[0:tq, :]
    acc_b = acc_scr[tq:2 * tq, :]
    half = LANES // 2
    o_ref[...] = jnp.where(lane < FOX_D, acc_a / pltpu.roll(acc_a, half, 1),
                           acc_b / pltpu.roll(acc_b, half, 1)).astype(BF16)


def _fox_first_tiles(qn2, kn2, c_all, batch, seq, tile_len):
    nq = seq // tile_len
    qn = jnp.sqrt(qn2[:, 0, 0:FOX_H]).reshape(batch, nq, FOX_H).transpose(0, 2, 1)
    kn = jnp.sqrt(kn2[:, 0, 0:FOX_H]).reshape(batch, nq, FOX_H).transpose(0, 2, 1)
    c = c_all[..., 0:seq].reshape(batch, FOX_H, nq, tile_len)
    c_first, c_last = c[..., 0], c[..., tile_len - 1]
    dots = FOX_NORM_SLACK * (qn[..., :, None] * kn[..., None, :] + (qn * kn)[..., :, None])
    gap = dots + c_first[..., :, None] - c_last[..., None, :]
    before = jnp.arange(nq)[None, :] < jnp.arange(nq)[:, None]
    droppable = (gap < -FOX_DROP_LOG2) & before
    first = jnp.sum(jnp.cumprod(droppable.astype(jnp.int32), axis=-1), axis=-1)
    return jnp.min(first.reshape(batch, FOX_H // 2, 2, nq), axis=2).reshape(-1).astype(jnp.int32)


def _fox(first_tile, fq, k_all, v_all, c_all, batch, seq, tile_len):
    tq = tk = tile_len
    nq = seq // tq
    idx = np.arange(tile_len)
    mask = jnp.asarray(np.where(idx[None, :] <= idx[:, None], 0.0, NEG_INF), F32)
    body = functools.partial(_fox_body, tq=tq, tk=tk)
    grid_spec = pltpu.PrefetchScalarGridSpec(
        num_scalar_prefetch=1,
        grid=(batch, FOX_H // 2, nq),
        in_specs=[pl.BlockSpec((tq, LANES), lambda b, h, i, _: (b * nq + i, h)),
                  pl.BlockSpec((1, seq, LANES), lambda b, h, i, _: (b, 0, h)),
                  pl.BlockSpec((1, seq, 2 * LANES), lambda b, h, i, _: (b, 0, h)),
                  pl.BlockSpec((1, 1, 2, seq), lambda b, h, i, _: (b, h, 0, 0)),
                  _const_spec(mask.shape)],
        out_specs=pl.BlockSpec((tq, LANES), lambda b, h, i, _: (b * nq + i, h)),
        scratch_shapes=[pltpu.VMEM((2 * tq, LANES), BF16), pltpu.VMEM((2 * tq, LANES), F32),
                        pltpu.VMEM((2 * tq, LANES), F32)])
    return pl.pallas_call(
        body,
        grid_spec=grid_spec,
        out_shape=jax.ShapeDtypeStruct((batch * seq, FOX_W), BF16),
        compiler_params=_params(("parallel", "parallel", "arbitrary")),
        name="fox_attention",
    )(first_tile, fq, k_all, v_all, c_all, mask)


def _fox_cached_body(q_ref, kp_ref, vp_ref, kn_ref, vn_ref, c_ref, o_ref, *, past, lq):
    q = q_ref[...]
    lane = lax.broadcasted_iota(jnp.int32, q.shape, 1)
    zero = jnp.zeros_like(q)
    qs = jnp.concatenate([jnp.where(lane < FOX_D, q, zero), jnp.where(lane >= FOX_D, q, zero)], axis=0)
    head_rows = lambda a, b, width: jnp.concatenate(
        [jnp.broadcast_to(a, (lq, width)), jnp.broadcast_to(b, (lq, width))], axis=0)
    s_past = _dot(qs, kp_ref[0].astype(BF16)) - head_rows(c_ref[0, 0, 0:1, 0:past], c_ref[0, 0, 1:2, 0:past], past)
    s_new = _dot_nt(qs, kn_ref[...]) - head_rows(c_ref[0, 0, 0:1, past:past + lq],
                                                 c_ref[0, 0, 1:2, past:past + lq], lq)
    qi = lax.broadcasted_iota(jnp.int32, s_new.shape, 0)
    ki = lax.broadcasted_iota(jnp.int32, s_new.shape, 1)
    s_new = jnp.where(ki <= jnp.where(qi >= lq, qi - lq, qi), s_new, NEG_INF)
    m = jnp.maximum(jnp.max(s_past, axis=1, keepdims=True), jnp.max(s_new, axis=1, keepdims=True))
    p_past = jnp.exp2(s_past - m)
    p_new = jnp.exp2(s_new - m)
    denom = jnp.sum(p_past, axis=1, keepdims=True) + jnp.sum(p_new, axis=1, keepdims=True)
    pb_new = p_new.astype(BF16)
    acc = _dot_nt(p_past.astype(BF16), vp_ref[0].astype(BF16))
    acc_a = (acc[0:lq] + _dot(pb_new[0:lq], vn_ref[:, 0:LANES])) / denom[0:lq]
    acc_b = (acc[lq:2 * lq] + _dot(pb_new[lq:2 * lq], vn_ref[:, LANES:2 * LANES])) / denom[lq:2 * lq]
    o_ref[...] = jnp.where(lane < FOX_D, acc_a, acc_b).astype(BF16)


def _fox_cached(fq, k_past, v_past, k_new, v_new, c_all, batch, lq):
    past = k_past.shape[2]
    new_rows = lambda w: pl.BlockSpec((lq, w), lambda b, h: (b, h))
    cache = pl.BlockSpec((1, LANES, past), lambda b, h: (b, h, 0))
    return pl.pallas_call(
        functools.partial(_fox_cached_body, past=past, lq=lq),
        grid=(batch, FOX_H // 2),
        in_specs=[new_rows(LANES), cache, cache, new_rows(LANES), new_rows(2 * LANES),
                  pl.BlockSpec((1, 1, 2, c_all.shape[-1]), lambda b, h: (b, h, 0, 0))],
        out_specs=new_rows(LANES),
        out_shape=jax.ShapeDtypeStruct((batch * lq, FOX_W), BF16),
        compiler_params=_params(("parallel", "parallel")),
        name="fox_attention_cached",
    )(fq, k_past, v_past, k_new, v_new, c_all)


def _normproj_body(x_ref, g_ref, w_ref, o_ref, ob_ref):
    y = _dot(_rms(x_ref[...], g_ref[...]).astype(BF16), w_ref[...])
    o_ref[...] = y
    ob_ref[...] = y.astype(BF16)


def _normproj(x, g, w, tm):
    n, wout = x.shape[0], w.shape[1]
    return pl.pallas_call(
        _normproj_body,
        grid=(n // tm,),
        in_specs=[pl.BlockSpec((tm, D_MODEL), lambda i: (i, 0)), _const_spec(g.shape), _const_spec(w.shape)],
        out_specs=[pl.BlockSpec((tm, wout), lambda i: (i, 0))] * 2,
        out_shape=[jax.ShapeDtypeStruct((n, wout), F32), jax.ShapeDtypeStruct((n, wout), BF16)],
        compiler_params=_params(("parallel",)),
        name="memory_kv",
    )(x, g, w)


def _post_body(*refs, has_mix, parts, seq_rows, mem_len):
    if has_mix:
        (x_ref, ro_ref, fo_ref, wo_ref, nw_ref, wxq_ref, wxo_ref, mk_ref, mv_ref, wup_ref, wdn_ref,
         o_ref) = refs
    else:
        x_ref, nw_ref, wxq_ref, wxo_ref, mk_ref, mv_ref, wup_ref, wdn_ref, o_ref = refs
    rows = [slice(r0, r0 + nrows) for r0, nrows in parts]
    xs = [x_ref[rs, :] for rs in rows]
    if has_mix:
        mixes = [_dot(ro_ref[rs, :], wo_ref[0:RET_W, :]) + _dot(fo_ref[rs, :], wo_ref[RET_W:RET_W + FOX_W, :])
                 for rs in rows]
        xs = [x + _rms(mix, nw_ref[1:2, :]) for x, mix in zip(xs, mixes)]

    qs = [_dot(_rms(x, nw_ref[2:3, :]).astype(BF16), wxq_ref[...]) for x in xs]
    ctx = []
    for (r0, nrows), q in zip(parts, qs):
        groups = []
        for g0 in range(0, nrows, min(seq_rows, nrows)):
            gs = slice(g0, g0 + min(seq_rows, nrows))
            m0 = ((r0 + g0) // seq_rows) * mem_len
            heads = []
            for hh in range(X_H):
                sl = slice(hh * X_D, (hh + 1) * X_D)
                s = _dot_nt(q[gs, sl].astype(BF16), mk_ref[m0:m0 + mem_len, sl]) * (X_D ** -0.5)
                p = jnp.exp(s - jnp.max(s, axis=-1, keepdims=True))
                o = _dot(p.astype(BF16), mv_ref[m0:m0 + mem_len, sl]) / jnp.sum(p, axis=-1, keepdims=True)
                heads.append(o.astype(BF16))
            groups.append(jnp.concatenate(heads, axis=1))
        ctx.append(groups[0] if len(groups) == 1 else jnp.concatenate(groups, axis=0))
    atts = [_dot(c, wxo_ref[...]) for c in ctx]
    xs = [x + _rms(att, nw_ref[3:4, :]) for x, att in zip(xs, atts)]

    hs = [_rms(x, nw_ref[4:5, :]).astype(BF16) for x in xs]
    accs = [None] * len(parts)
    for c in range(D_FF // D_MODEL):
        sl = slice(c * D_MODEL, (c + 1) * D_MODEL)
        for i, h in enumerate(hs):
            r = jnp.maximum(_dot(h, wup_ref[:, sl]), 0.0)
            part = _dot((r * r).astype(BF16), wdn_ref[sl, :])
            accs[i] = part if accs[i] is None else accs[i] + part
    for rs, x, acc in zip(rows, xs, accs):
        o_ref[rs, :] = x + _rms(acc, nw_ref[5:6, :])


def _post(x, mix_in, nw, wxq, wxo, mk, mv, wup, wdn, seq, tm):
    n = x.shape[0]
    mem_len = mk.shape[0] // (n // seq)
    row = lambda w: pl.BlockSpec((tm, w), lambda i: (i, 0))
    if tm <= seq:
        per_batch = seq // tm
        mem = pl.BlockSpec((mem_len, D_MODEL), lambda i: (i // per_batch, 0))
        seq_rows = tm
        nparts = POST_PARTS if tm % (POST_PARTS * 16) == 0 else 1
        parts = tuple((p * (tm // nparts), tm // nparts) for p in range(nparts))
    else:
        mem = pl.BlockSpec(((tm // seq) * mem_len, D_MODEL), lambda i: (i, 0))
        seq_rows = seq
        parts = ((0, tm),)
    args = [x]
    specs = [row(D_MODEL)]
    if mix_in is not None:
        ro, fo, wo = mix_in
        args += [ro, fo, wo]
        specs += [row(RET_W), row(FOX_W), _const_spec(wo.shape)]
    args += [nw, wxq, wxo, mk, mv, wup, wdn]
    specs += [_const_spec(nw.shape), _const_spec(wxq.shape), _const_spec(wxo.shape), mem, mem,
              _const_spec(wup.shape), _const_spec(wdn.shape)]
    return pl.pallas_call(
        functools.partial(_post_body, has_mix=mix_in is not None, parts=parts, seq_rows=seq_rows, mem_len=mem_len),
        grid=(n // tm,),
        in_specs=specs,
        out_specs=row(D_MODEL),
        out_shape=jax.ShapeDtypeStruct((n, D_MODEL), F32),
        compiler_params=_params(("parallel",)),
        name="xattn_mlp_mix" if mix_in is not None else "xattn_mlp",
    )(*args)


def _s5prep_body(lre_ref, lim_ref, ldt_ref, lre_rep_ref, lim_rep_ref, bre_ref, bim_ref,
                 are_ref, aim_ref, bbre_ref, bbim_ref):
    dt = jnp.exp(ldt_ref[...])

    def zoh(lre, lim):
        mag = jnp.exp(lre * dt)
        a_re = mag * jnp.cos(lim * dt)
        a_im = mag * jnp.sin(lim * dt)
        den = lre * lre + lim * lim
        num_re = a_re - 1.0
        f_re = (num_re * lre + a_im * lim) / den
        f_im = (a_im * lre - num_re * lim) / den
        return a_re, a_im, f_re, f_im

    a_re, a_im, _, _ = zoh(lre_ref[...], lim_ref[...])
    are_ref[...] = a_re
    aim_ref[...] = a_im
    _, _, f_re, f_im = zoh(lre_rep_ref[...], lim_rep_ref[...])
    b_re = bre_ref[...]
    b_im = bim_ref[...]
    bbre_ref[...] = f_re * b_re - f_im * b_im
    bbim_ref[...] = f_re * b_im + f_im * b_re


def _s5_discretize(lam_re, lam_im, log_dt, b_re, b_im):
    rep = lambda a: jnp.repeat(a, S5_GROUP, axis=1)
    wide = jax.ShapeDtypeStruct((S5_G, S5_P * S5_GROUP), F32)
    small = jax.ShapeDtypeStruct((S5_G, S5_P), F32)
    a_re, a_im, bb_re, bb_im = pl.pallas_call(
        _s5prep_body, out_shape=[small, small, wide, wide], name="s5_discretize",
    )(lam_re, lam_im, log_dt.reshape(S5_G, 1), rep(lam_re), rep(lam_im),
      b_re.reshape(S5_G, -1), b_im.reshape(S5_G, -1))
    a = jnp.stack([a_re.reshape(-1), a_im.reshape(-1)])
    gpc = S5_LANE_CHUNK // S5_P
    nch = S5_G // gpc
    eye = jnp.eye(gpc, dtype=F32)

    def in_blocks(bb):
        t = bb.reshape(nch, gpc, S5_P, S5_GROUP).transpose(0, 1, 3, 2)
        return (t[:, :, :, None, :] * eye[None, :, None, :, None]).reshape(nch, gpc * S5_GROUP, gpc * S5_P)

    bmat = jnp.concatenate([in_blocks(bb_re), in_blocks(bb_im)], axis=-1).astype(BF16)
    return a, bmat


def _s5_out_matrices(c_re, c_im):
    gpc = S5_LANE_CHUNK // S5_P
    nch = S5_G // gpc
    eye = jnp.eye(gpc, dtype=F32)

    def out_blocks(c):
        t = c.reshape(nch, gpc, S5_GROUP, S5_P).transpose(0, 1, 3, 2)
        return (t[:, :, :, None, :] * eye[None, :, None, :, None]).reshape(nch, gpc * S5_P, gpc * S5_GROUP)

    return jnp.concatenate([out_blocks(c_re), -out_blocks(c_im)], axis=1).astype(BF16)


def _s5_body(x_ref, nw_ref, perm_ref, win_ref, bmat_ref, a_ref, cmat_ref, d_ref, wglu_ref, st0_ref,
             o_ref, st_ref, u_scr, bu_scr, y_scr, *, ts):
    nb = x_ref.shape[0]
    rows = nb * ts
    lc = S5_LANE_CHUNK
    gw = (lc // S5_P) * S5_GROUP

    @pl.when(pl.program_id(0) == 0)
    def _():
        st_ref[...] = st0_ref[...]

    x = x_ref[...].reshape(rows, D_MODEL)
    h = _dot(perm_ref[0], _rms(x, nw_ref[0:1, :]).astype(BF16)).astype(BF16)
    u_scr[...] = _dot(h, win_ref[...])
    nv = lc // LANES
    for j in range(S5_STATE // lc):
        uj = u_scr[:, j * gw:(j + 1) * gw]
        bu = _dot(uj.astype(BF16), bmat_ref[j])
        for c in range(2 * nv):
            bu_scr[c] = bu[:, c * LANES:(c + 1) * LANES]
        a_re = [jnp.broadcast_to(a_ref[0:1, j * lc + c * LANES:j * lc + (c + 1) * LANES], (nb, LANES))
                for c in range(nv)]
        a_im = [jnp.broadcast_to(a_ref[1:2, j * lc + c * LANES:j * lc + (c + 1) * LANES], (nb, LANES))
                for c in range(nv)]

        def step(t, carry):
            idx = pl.ds(pl.multiple_of(t * nb, nb), nb)
            new = []
            for c in range(nv):
                s_re, s_im = carry[c], carry[nv + c]
                new.append(a_re[c] * s_re - a_im[c] * s_im + bu_scr[c, idx, :])
            for c in range(nv):
                s_re, s_im = carry[c], carry[nv + c]
                new.append(a_re[c] * s_im + a_im[c] * s_re + bu_scr[nv + c, idx, :])
            for c in range(2 * nv):
                bu_scr[c, idx, :] = new[c]
            return tuple(new)

        init = tuple(st_ref[:, j * lc + c * LANES:j * lc + (c + 1) * LANES] for c in range(nv)) + tuple(
            st_ref[:, S5_STATE + j * lc + c * LANES:S5_STATE + j * lc + (c + 1) * LANES] for c in range(nv))
        fin = lax.fori_loop(0, ts, step, init, unroll=True)
        for c in range(nv):
            st_ref[:, j * lc + c * LANES:j * lc + (c + 1) * LANES] = fin[c]
            st_ref[:, S5_STATE + j * lc + c * LANES:S5_STATE + j * lc + (c + 1) * LANES] = fin[nv + c]
        xs = jnp.concatenate([bu_scr[c] for c in range(2 * nv)], axis=1).astype(BF16)
        y_scr[:, j * gw:(j + 1) * gw] = _dot(xs, cmat_ref[j]) + d_ref[0:1, j * gw:(j + 1) * gw] * uj
    y = _dot(perm_ref[1], y_scr[...].astype(BF16)).astype(BF16)
    g = _dot(y, wglu_ref[...])
    out = g[:, 0:D_MODEL] * jax.nn.sigmoid(g[:, D_MODEL:2 * D_MODEL])
    o_ref[...] = (x + _rms(out, nw_ref[1:2, :])).reshape(nb, ts, D_MODEL)


def _s5_layer(x3, nw, win, bmat, a, cmat, d, wglu, st0, ts):
    nb, seq, _ = x3.shape
    rows = nb * ts
    blk = pl.BlockSpec((nb, ts, D_MODEL), lambda t: (0, t, 0))
    r = np.arange(rows)
    fwd = np.zeros((rows, rows), np.float32)
    fwd[(r % ts) * nb + r // ts, r] = 1.0
    perm = jnp.asarray(np.stack([fwd, fwd.T]), BF16)
    consts = [nw, perm, win, bmat, a, cmat, d, wglu, st0]
    return pl.pallas_call(
        functools.partial(_s5_body, ts=ts),
        grid=(seq // ts,),
        in_specs=[blk] + [_const_spec(c.shape) for c in consts],
        out_specs=[blk, pl.BlockSpec(st0.shape, lambda t: (0, 0))],
        out_shape=[jax.ShapeDtypeStruct(x3.shape, F32), jax.ShapeDtypeStruct(st0.shape, F32)],
        scratch_shapes=[pltpu.VMEM((rows, D_MODEL), F32),
                        pltpu.VMEM((2 * S5_LANE_CHUNK // LANES, rows, LANES), F32),
                        pltpu.VMEM((rows, D_MODEL), F32)],
        compiler_params=_params(("arbitrary",)),
        name="s5_layer",
    )(x3, *consts)


def _rotary_tables(pos, rows):
    half = RET_D // 2
    inv = ROPE_BASE ** (-jnp.arange(half, dtype=F32) / half)
    ang = pos.astype(F32)[:, None] * inv[None, :]
    cos = jnp.cos(ang)
    sin = jnp.sin(ang)
    cos_t = jnp.concatenate([cos, cos], axis=1)
    sin_t = jnp.concatenate([-sin, sin], axis=1)
    reps = max(1, rows // pos.shape[0])
    return jnp.tile(cos_t, (reps, 1)), jnp.tile(sin_t, (reps, 1))


def _prepare_weights(norm_w, w_in_ab, b_fox_f, w_out_ab, w_in_c, s5_lambda_re, s5_lambda_im, s5_log_dt,
                     s5_b_re, s5_b_im, s5_c_re, s5_c_im, s5_d, w_glu, w_xq, w_xo, w_up, w_down):
    w = {}
    wab = w_in_ab[0].astype(BF16)
    w["wr"] = wab[:, 0:4 * RET_W]
    w["wf"] = wab[:, 4 * RET_W:4 * RET_W + 3 * FOX_W]
    w["wl"] = jnp.pad(wab[:, 4 * RET_W + 3 * FOX_W:], ((0, 0), (0, LANES - FOX_H)))
    w["bl"] = jnp.pad(b_fox_f[0].astype(F32), (0, LANES - FOX_H)).reshape(1, LANES)
    w["wo"] = w_out_ab[0].astype(BF16)
    w["nw"] = norm_w.astype(F32)
    w["win_c"] = w_in_c[0].astype(BF16)
    w["a"], w["bmat"] = _s5_discretize(s5_lambda_re[0], s5_lambda_im[0], s5_log_dt[0], s5_b_re[0], s5_b_im[0])
    w["cmat"] = _s5_out_matrices(s5_c_re[0].astype(F32), s5_c_im[0].astype(F32))
    w["d"] = s5_d[0].astype(F32).reshape(1, D_MODEL)
    w["wglu"] = w_glu[0].astype(BF16)
    for name, arr in (("wxq", w_xq), ("wxo", w_xo), ("wup", w_up), ("wdn", w_down)):
        w[name] = arr.astype(BF16)
    return w


def _run_group(x3, pos, w, gn_w, mem_kv, ret_s0, fox_past, s5_st0):
    batch, seq, _ = x3.shape
    n = batch * seq
    tm = min(ROW_TILE, n)
    tm_b = ROW_TILE if seq >= ROW_TILE else min(n, (ROW_TILE // seq) * seq)
    x = x3.reshape(n, D_MODEL)

    cos_t, sin_t = _rotary_tables(pos, tm)
    rq, rk, rv, rg, fq, fkb, fvp, fk, fv, lf, qn2, kn2 = _inproj(
        x, w["nw"][0, 0:1], w["wr"], w["wf"], w["wl"], w["bl"], cos_t, sin_t, tm, seq)
    if tm <= seq:
        lf_t = lf
        fk = jnp.transpose(fk, (0, 3, 1, 2))
        fv = jnp.transpose(fv, (0, 3, 1, 2))
        lf3 = jnp.transpose(lf, (0, 2, 1))
    else:
        lf3 = lf.reshape(batch, seq, FOX_H)
        lf_t = jnp.transpose(lf3, (0, 2, 1))
    if fox_past is not None:
        lf_t = jnp.concatenate([jnp.transpose(fox_past[2].astype(F32), (0, 2, 1)), lf_t], axis=2)
    lk = lf_t.shape[2]
    pad = -lk % CUMSUM_CHUNK
    lf_rows = jnp.pad(lf_t.reshape(batch * FOX_H, lk), ((0, 0), (0, pad)))
    c_all = _cumsum_lanes(lf_rows).reshape(batch, FOX_H // 2, 2, lk + pad)
    if fox_past is None:
        assert tm == FOX_TILE
        first_tile = _fox_first_tiles(qn2, kn2, c_all, batch, seq, FOX_TILE)
        fo = _fox(first_tile, fq, fkb.reshape(batch, seq, FOX_W), fvp.reshape(batch, seq, FOX_H * LANES), c_all,
                  batch, seq, FOX_TILE)
    else:
        past = fox_past[0].shape[1]
        cache_t = lambda a: jnp.transpose(a, (0, 2, 3, 1)).reshape(batch, FOX_W, past)
        fo = _fox_cached(fq, cache_t(fox_past[0]), cache_t(fox_past[1]), fkb, fvp, c_all, batch, seq)

    chunk = CHUNK if fox_past is None else seq
    ro, ret_state = _retention(rq, rk, rv, rg, gn_w, ret_s0, batch, seq, chunk)

    x = _post(x, (ro, fo, w["wo"]), w["nw"][0], w["wxq"][0], w["wxo"][0], mem_kv[0][0], mem_kv[0][1],
              w["wup"][0], w["wdn"][0], seq, tm_b)

    ts = min(S5_TS, seq)
    x3n, s5_state = _s5_layer(x.reshape(batch, seq, D_MODEL), w["nw"][1], w["win_c"], w["bmat"], w["a"],
                              w["cmat"], w["d"], w["wglu"], s5_st0, ts)
    x = _post(x3n.reshape(n, D_MODEL), None, w["nw"][1], w["wxq"][1], w["wxo"][1], mem_kv[1][0], mem_kv[1][1],
              w["wup"][1], w["wdn"][1], seq, tm_b)

    new = dict(ret=ret_state[None], fox_k=fk.reshape(1, batch, seq, FOX_H, FOX_D),
               fox_v=fv.reshape(1, batch, seq, FOX_H, FOX_D), fox_logf=lf3[None],
               s5_re=s5_state[:, 0:S5_STATE].reshape(1, batch, S5_G, S5_P),
               s5_im=s5_state[:, S5_STATE:].reshape(1, batch, S5_G, S5_P))
    return x.reshape(batch, seq, D_MODEL), new


def kernel(x_prompt, x_sample, cache_ret_state, cache_fox_k, cache_fox_v, cache_fox_logf, state_s5_re, state_s5_im, cache_mem_k, cache_mem_v, mem_prompt, norm_w, w_in_ab, b_fox_f, ret_gn_w, w_out_ab, w_in_c, s5_lambda_re, s5_lambda_im, s5_log_dt, s5_b_re, s5_b_im, s5_c_re, s5_c_im, s5_d, w_glu, mem_norm_w, w_xq, w_xk, w_xv, w_xo, w_up, w_down):
    w = _prepare_weights(norm_w, w_in_ab, b_fox_f, w_out_ab, w_in_c, s5_lambda_re, s5_lambda_im, s5_log_dt,
                         s5_b_re, s5_b_im, s5_c_re, s5_c_im, s5_d, w_glu, w_xq, w_xo, w_up, w_down)
    batch, seq, _ = x_prompt.shape
    dec_batch, dec_seq, _ = x_sample.shape
    mem_len = mem_prompt.shape[1]
    depth = norm_w.shape[0]
    gn_w = ret_gn_w[0].astype(F32)

    mem_rows = mem_prompt.reshape(batch * mem_len, D_MODEL)
    mem_kv_prompt = []
    mem_k_out = []
    mem_v_out = []
    for layer in range(depth):
        wkv = jnp.concatenate([w_xk[layer], w_xv[layer]], axis=1).astype(BF16)
        kv, kvb = _normproj(mem_rows, mem_norm_w[layer].astype(F32).reshape(1, D_MODEL), wkv,
                            min(ROW_TILE, batch * mem_len))
        mem_kv_prompt.append((kvb[:, 0:D_MODEL], kvb[:, D_MODEL:]))
        mem_k_out.append(kv[:, 0:D_MODEL].reshape(batch, mem_len, X_H, X_D))
        mem_v_out.append(kv[:, D_MODEL:].reshape(batch, mem_len, X_H, X_D))

    pos_prompt = jnp.arange(seq)
    y_prompt, np_ = _run_group(
        x_prompt, pos_prompt, w, gn_w, mem_kv_prompt,
        jnp.zeros((batch, RET_H, RET_D, RET_D), F32), None, jnp.zeros((batch, 2 * S5_STATE), F32))

    past = cache_fox_k.shape[2]
    pos_sample = past + jnp.arange(dec_seq)
    mem_kv_sample = [(cache_mem_k[layer].reshape(dec_batch * mem_len, D_MODEL).astype(BF16),
                      cache_mem_v[layer].reshape(dec_batch * mem_len, D_MODEL).astype(BF16))
                     for layer in range(depth)]
    s5_st0 = jnp.concatenate([state_s5_re[0].reshape(dec_batch, S5_STATE),
                              state_s5_im[0].reshape(dec_batch, S5_STATE)], axis=1).astype(F32)
    y_sample, ns_ = _run_group(
        x_sample, pos_sample, w, gn_w, mem_kv_sample, cache_ret_state[0].astype(F32),
        (cache_fox_k[0], cache_fox_v[0], cache_fox_logf[0]), s5_st0)

    return (y_prompt, y_sample,
            np_["ret"], ns_["ret"],
            np_["fox_k"], np_["fox_v"], np_["fox_logf"],
            ns_["fox_k"], ns_["fox_v"], ns_["fox_logf"],
            np_["s5_re"], np_["s5_im"], ns_["s5_re"], ns_["s5_im"],
            jnp.stack(mem_k_out), jnp.stack(mem_v_out))
```

```python
import functools

import numpy as np
import jax
import jax.numpy as jnp
from jax import lax
from jax.experimental import pallas as pl
from jax.experimental.pallas import tpu as pltpu

F32 = jnp.float32
BF16 = jnp.bfloat16

D_MODEL = 1024
RET_H = 4
RET_D = 128
RET_W = RET_H * RET_D
CHUNK = 64
ROPE_BASE = 10000.0
FOX_H = 8
FOX_D = 64
FOX_W = FOX_H * FOX_D
S5_GROUP = 16
S5_G = D_MODEL // S5_GROUP
S5_P = 64
S5_STATE = S5_G * S5_P
X_H = 4
X_D = D_MODEL // X_H
D_FF = 4 * D_MODEL
EPS = 1e-6
NEG_INF = -1e30
LOG2E = float(np.log2(np.e))

LANES = 128
VMEM_LIMIT = 56 * 1024 * 1024

ROW_TILE = 512
POST_PARTS = 2
RET_BLOCK = 256
FOX_TILE = 512
FOX_UNROLL = 4
FOX_DROP_LOG2 = 152.0
FOX_NORM_SLACK = 1.02
S5_TS = 64
S5_LANE_CHUNK = 512
CUMSUM_CHUNK = 256


def _dot(a, b):
    return jnp.dot(a, b, preferred_element_type=F32)


def _dot_nt(a, b):
    return lax.dot_general(a, b, (((1,), (1,)), ((), ())), preferred_element_type=F32)


def _dot_tn(a, b):
    return lax.dot_general(a, b, (((0,), (0,)), ((), ())), preferred_element_type=F32)


def _rms(x, w):
    return x * lax.rsqrt(jnp.mean(x * x, axis=-1, keepdims=True) + EPS) * w


def _const_spec(shape):
    nd = len(shape)
    return pl.BlockSpec(shape, lambda *_: (0,) * nd, pipeline_mode=pl.Buffered(1))


def _params(sem):
    return pltpu.CompilerParams(dimension_semantics=sem, vmem_limit_bytes=VMEM_LIMIT)


def _inproj_body(x_ref, g_ref, wr_ref, wf_ref, wl_ref, bl_ref, cos_ref, sin_ref, hsum_ref,
                 rq_ref, rk_ref, rv_ref, rg_ref, fq_ref, fkb_ref, fvp_ref, fk_ref, fv_ref, lf_ref,
                 qn_ref, kn_ref, *, head_major):
    def put_kv(dst_ref, z):
        if head_major:
            dst_ref[0] = z.T.reshape(FOX_H, FOX_D, z.shape[0])
        else:
            dst_ref[...] = z

    h = _rms(x_ref[...], g_ref[...]).astype(BF16)
    cos = cos_ref[...]
    sin = sin_ref[...]

    def rotary_to(dst_ref, z, scale):
        for hh in range(RET_H):
            sl = slice(hh * RET_D, (hh + 1) * RET_D)
            zz = z[:, sl]
            r = zz * cos + pltpu.roll(zz, RET_D // 2, 1) * sin
            if scale is not None:
                r = r * scale
            dst_ref[:, sl] = r.astype(BF16)

    rotary_to(rq_ref, _dot(h, wr_ref[:, 0:RET_W]), None)
    rotary_to(rk_ref, _dot(h, wr_ref[:, RET_W:2 * RET_W]), RET_D ** -0.5)
    rv_ref[...] = _dot(h, wr_ref[:, 2 * RET_W:3 * RET_W]).astype(BF16)
    rg_ref[...] = _dot(h, wr_ref[:, 3 * RET_W:4 * RET_W])
    def max_sq_norm(zb):
        return jnp.max(_dot(zb * zb, hsum_ref[...]), axis=0, keepdims=True)

    fqb = (_dot(h, wf_ref[:, 0:FOX_W]) * (LOG2E * FOX_D ** -0.5)).astype(BF16)
    fq_ref[...] = fqb
    qn_ref[0] = max_sq_norm(fqb)
    fk = _dot(h, wf_ref[:, FOX_W:2 * FOX_W])
    put_kv(fk_ref, fk)
    fkb = fk.astype(BF16)
    fkb_ref[...] = fkb
    kn_ref[0] = max_sq_norm(fkb)
    fv = _dot(h, wf_ref[:, 2 * FOX_W:3 * FOX_W])
    put_kv(fv_ref, fv)
    lane = lax.broadcasted_iota(jnp.int32, (fv.shape[0], LANES), 1)
    for pr in range(FOX_H // 2):
        vp = fv[:, pr * LANES:(pr + 1) * LANES]
        fvp_ref[:, (2 * pr) * LANES:(2 * pr + 1) * LANES] = jnp.where(lane < FOX_D, vp, 1.0).astype(BF16)
        fvp_ref[:, (2 * pr + 1) * LANES:(2 * pr + 2) * LANES] = jnp.where(lane >= FOX_D, vp, 1.0).astype(BF16)
    zl = _dot(h, wl_ref[...]) + bl_ref[...]
    lf = jnp.minimum(zl, 0.0) - jnp.log1p(jnp.exp(-jnp.abs(zl)))
    if head_major:
        lf_ref[0] = lf.T[0:FOX_H, :]
    else:
        lf_ref[...] = lf[:, 0:FOX_H]


def _inproj(x, g, wr, wf, wl, bl, cos_t, sin_t, tm, seq):
    n = x.shape[0]
    nt = cos_t.shape[0] // tm
    head_major = tm <= seq
    row = lambda w: pl.BlockSpec((tm, w), lambda i: (i, 0))
    tab = pl.BlockSpec((tm, LANES), lambda i: (i % nt, 0))
    shp = lambda w, dt: jax.ShapeDtypeStruct((n, w), dt)
    hsum = jnp.asarray(np.arange(FOX_W)[:, None] // FOX_D == np.arange(LANES)[None, :], BF16)
    stat_spec = pl.BlockSpec((1, 1, LANES), lambda i: (i, 0, 0))
    stat_shape = jax.ShapeDtypeStruct((n // tm, 1, LANES), F32)
    if head_major:
        per_batch = seq // tm
        kv_spec = pl.BlockSpec((1, FOX_H, FOX_D, tm), lambda i: (i // per_batch, 0, 0, i % per_batch))
        kv_shape = jax.ShapeDtypeStruct((n // seq, FOX_H, FOX_D, seq), F32)
        lf_spec = pl.BlockSpec((1, FOX_H, tm), lambda i: (i // per_batch, 0, i % per_batch))
        lf_shape = jax.ShapeDtypeStruct((n // seq, FOX_H, seq), F32)
    else:
        kv_spec, kv_shape = row(FOX_W), shp(FOX_W, F32)
        lf_spec, lf_shape = row(FOX_H), shp(FOX_H, F32)
    return pl.pallas_call(
        functools.partial(_inproj_body, head_major=head_major),
        grid=(n // tm,),
        in_specs=[row(D_MODEL), _const_spec(g.shape), _const_spec(wr.shape), _const_spec(wf.shape),
                  _const_spec(wl.shape), _const_spec(bl.shape), tab, tab, _const_spec(hsum.shape)],
        out_specs=[row(RET_W), row(RET_W), row(RET_W), row(RET_W), row(FOX_W), row(FOX_W),
                   row(FOX_H * LANES), kv_spec, kv_spec, lf_spec, stat_spec, stat_spec],
        out_shape=[shp(RET_W, BF16), shp(RET_W, BF16), shp(RET_W, BF16), shp(RET_W, F32),
                   shp(FOX_W, BF16), shp(FOX_W, BF16), shp(FOX_H * LANES, BF16), kv_shape, kv_shape,
                   lf_shape, stat_shape, stat_shape],
        compiler_params=_params(("parallel",)),
        name="inproj_ab",
    )(x, g, wr, wf, wl, bl, cos_t, sin_t, hsum)


def _cumsum_body(x_ref, u_ref, o_ref):
    rows, total = x_ref.shape
    u = u_ref[...]

    def chunk(i, carry):
        off = pl.multiple_of(i * CUMSUM_CHUNK, CUMSUM_CHUNK)
        xc = x_ref[:, pl.ds(off, CUMSUM_CHUNK)]
        hi = xc.astype(BF16)
        r1 = xc - hi.astype(F32)
        mid = r1.astype(BF16)
        lo = (r1 - mid.astype(F32)).astype(BF16)
        y = _dot(hi, u) + _dot(mid, u) + _dot(lo, u) + carry
        o_ref[:, pl.ds(off, CUMSUM_CHUNK)] = y * LOG2E
        return y[:, CUMSUM_CHUNK - 1:CUMSUM_CHUNK]

    lax.fori_loop(0, total // CUMSUM_CHUNK, chunk, jnp.zeros((rows, 1), F32))


def _cumsum_lanes(x):
    idx = np.arange(CUMSUM_CHUNK)
    u = jnp.asarray(idx[:, None] <= idx[None, :], BF16)
    return pl.pallas_call(
        _cumsum_body,
        out_shape=jax.ShapeDtypeStruct(x.shape, F32),
        compiler_params=pltpu.CompilerParams(vmem_limit_bytes=VMEM_LIMIT),
        name="logf_cumsum",
    )(x, u)


def _ret_body(q_ref, k_ref, v_ref, rg_ref, dm_ref, wq_ref, wk_ref, dec_ref, gn_ref, s0_ref,
              ro_ref, s_ref):
    @pl.when(pl.program_id(1) == 0)
    def _():
        s_ref[...] = s0_ref[...]

    heads = range(RET_H)
    sls = [slice(hh * RET_D, (hh + 1) * RET_D) for hh in heads]
    qs = [q_ref[:, sl] for sl in sls]
    ks = [k_ref[:, sl] for sl in sls]
    vs = [v_ref[:, sl] for sl in sls]
    states = [s_ref[0, hh] for hh in heads]
    scores = [(_dot_nt(qs[hh], ks[hh]) * dm_ref[hh]).astype(BF16) for hh in heads]
    inter = [_dot(qs[hh], states[hh].astype(BF16)) * wq_ref[hh] for hh in heads]
    outs = [_dot(scores[hh], vs[hh]) + inter[hh] for hh in heads]
    kws = [(ks[hh].astype(F32) * wk_ref[hh]).astype(BF16) for hh in heads]
    for hh in heads:
        s_ref[0, hh] = states[hh] * dec_ref[hh] + _dot_tn(kws[hh], vs[hh])
    for hh in heads:
        o = outs[hh]
        mu = jnp.mean(o, axis=-1, keepdims=True)
        oc = o - mu
        var = jnp.mean(oc * oc, axis=-1, keepdims=True)
        g = rg_ref[:, sls[hh]]
        ro_ref[:, sls[hh]] = (oc * lax.rsqrt(var + EPS) * gn_ref[hh] * (g * jax.nn.sigmoid(g))).astype(BF16)


def _retention(rq, rk, rv, rg, gn_w, s0, batch, seq, chunk):
    tb = min(RET_BLOCK, seq)
    nb = seq // tb
    lg = np.log(1.0 - 2.0 ** (-5.0 - np.arange(RET_H, dtype=np.float64)))
    idx = np.arange(tb, dtype=np.float64)
    visible = (idx[None, :] // chunk) <= (idx[:, None] // chunk)
    dm = np.exp(lg[:, None, None] * np.abs(idx[:, None] - idx[None, :])) * visible[None]
    wq = np.broadcast_to(np.exp(lg[:, None] * (idx[None, :] + 1.0))[:, :, None], (RET_H, tb, RET_D))
    wk = np.broadcast_to(np.exp(lg[:, None] * (tb - 1.0 - idx[None, :]))[:, :, None], (RET_H, tb, RET_D))
    dec = np.broadcast_to(np.exp(lg * tb)[:, None, None], (RET_H, 1, RET_D))
    consts = [jnp.asarray(a, F32) for a in (dm, wq, wk, dec)]
    gn = gn_w.reshape(RET_H, 1, RET_D)
    row = pl.BlockSpec((tb, RET_W), lambda b, t: (b * nb + t, 0))
    st = pl.BlockSpec((1, RET_H, RET_D, RET_D), lambda b, t: (b, 0, 0, 0))
    return pl.pallas_call(
        _ret_body,
        grid=(batch, nb),
        in_specs=[row, row, row, row] + [_const_spec(c.shape) for c in consts] + [_const_spec(gn.shape), st],
        out_specs=[row, st],
        out_shape=[jax.ShapeDtypeStruct((batch * seq, RET_W), BF16),
                   jax.ShapeDtypeStruct((batch, RET_H, RET_D, RET_D), F32)],
        compiler_params=_params(("parallel", "arbitrary")),
        name="retention",
    )(rq, rk, rv, rg, *consts, gn, s0)


def _fox_body(start_ref, q_ref, k_ref, v_ref, c_ref, mask_ref, o_ref, qs_scr, m_scr, acc_scr, *, tq, tk):
    n_full = pl.program_id(2)
    pair = pl.program_id(0) * pl.num_programs(1) + pl.program_id(1)
    first_a = start_ref[(2 * pair) * pl.num_programs(2) + n_full]
    first_b = start_ref[(2 * pair + 1) * pl.num_programs(2) + n_full]
    solo_first = jnp.minimum(first_a, first_b)
    first = jnp.maximum(first_a, first_b)
    solo_head = (first_b < first_a).astype(jnp.int32)
    q = q_ref[...]
    lane = lax.broadcasted_iota(jnp.int32, q.shape, 1)
    zero = jnp.zeros_like(q)
    qs_scr[0:tq, :] = jnp.where(lane < FOX_D, q, zero)
    qs_scr[tq:2 * tq, :] = jnp.where(lane >= FOX_D, q, zero)
    m_scr[...] = jnp.full(m_scr.shape, -jnp.inf, F32)
    acc_scr[...] = jnp.zeros(acc_scr.shape, F32)

    def tile(kj, masked):
        off = pl.multiple_of(kj * tk, tk)
        s = _dot_nt(qs_scr[...], k_ref[0, pl.ds(off, tk), :])
        hrows = [slice(hh * tq, (hh + 1) * tq) for hh in range(2)]
        shs = [s[hrows[hh], :] - c_ref[0, 0, hh:hh + 1, pl.ds(off, tk)] for hh in range(2)]
        if masked:
            shs = [sh + mask_ref[...] for sh in shs]
        chunks = [[sh[:, c * LANES:(c + 1) * LANES] for c in range(tk // LANES)] for sh in shs]
        m_prevs = [m_scr[rows, :] for rows in hrows]
        m_news = [jnp.maximum(m_prevs[hh], jnp.max(functools.reduce(jnp.maximum, chunks[hh]), axis=1, keepdims=True))
                  for hh in range(2)]
        ps = [jnp.concatenate([jnp.exp2(ch - m_news[hh]) for ch in chunks[hh]], axis=1).astype(BF16)
              for hh in range(2)]
        pvs = [_dot(ps[hh], v_ref[0, pl.ds(off, tk), hh * LANES:(hh + 1) * LANES]) for hh in range(2)]
        for hh in range(2):
            acc_scr[hrows[hh], :] = jnp.exp2(m_prevs[hh] - m_news[hh]) * acc_scr[hrows[hh], :] + pvs[hh]
            m_scr[hrows[hh], :] = m_news[hh]

    def solo_tile(kj):
        off = pl.multiple_of(kj * tk, tk)
        rows = pl.ds(pl.multiple_of(solo_head * tq, tq), tq)
        sh = (_dot_nt(qs_scr[rows, :], k_ref[0, pl.ds(off, tk), :])
              - c_ref[0, 0, pl.ds(solo_head, 1), pl.ds(off, tk)])
        chunks = [sh[:, c * LANES:(c + 1) * LANES] for c in range(tk // LANES)]
        m_prev = m_scr[rows, :]
        m_new = jnp.maximum(m_prev, jnp.max(functools.reduce(jnp.maximum, chunks), axis=1, keepdims=True))
        p = jnp.concatenate([jnp.exp2(ch - m_new) for ch in chunks], axis=1).astype(BF16)
        pv = _dot(p, v_ref[0, pl.ds(off, tk), :])
        pv = jnp.where(solo_head == 0, pv[:, 0:LANES], pv[:, LANES:2 * LANES])
        acc_scr[rows, :] = jnp.exp2(m_prev - m_new) * acc_scr[rows, :] + pv
        m_scr[rows, :] = m_new

    def run_tiles(start, count, visit):
        def group(i, carry):
            for j in range(FOX_UNROLL):
                visit(start + FOX_UNROLL * i + j)
            return carry

        lax.fori_loop(0, count // FOX_UNROLL, group, 0)
        span = FOX_UNROLL // 2
        while span >= 1:
            @pl.when((count // span) % 2 == 1)
            def _(span=span):
                base = start + (count // (2 * span)) * (2 * span)
                for j in range(span):
                    visit(base + j)
            span //= 2

    run_tiles(solo_first, first - solo_first, solo_tile)
    run_tiles(first, n_full - first, lambda kj: tile(kj, False))
    tile(n_full, True)

    acc_a = acc_scr[0:tq, :]
    acc_b = acc_scr[tq:2 * tq, :]
    half = LANES // 2
    o_ref[...] = jnp.where(lane < FOX_D, acc_a / pltpu.roll(acc_a, half, 1),
                           acc_b / pltpu.roll(acc_b, half, 1)).astype(BF16)


def _fox_first_tiles(qn2, kn2, c_all, batch, seq, tile_len):
    nq = seq // tile_len
    qn = jnp.sqrt(qn2[:, 0, 0:FOX_H]).reshape(batch, nq, FOX_H).transpose(0, 2, 1)
    kn = jnp.sqrt(kn2[:, 0, 0:FOX_H]).reshape(batch, nq, FOX_H).transpose(0, 2, 1)
    c = c_all[..., 0:seq].reshape(batch, FOX_H, nq, tile_len)
    c_first, c_last = c[..., 0], c[..., tile_len - 1]
    dots = FOX_NORM_SLACK * (qn[..., :, None] * kn[..., None, :] + (qn * kn)[..., :, None])
    gap = dots + c_first[..., :, None] - c_last[..., None, :]
    before = jnp.arange(nq)[None, :] < jnp.arange(nq)[:, None]
    droppable = (gap < -FOX_DROP_LOG2) & before
    first = jnp.sum(jnp.cumprod(droppable.astype(jnp.int32), axis=-1), axis=-1)
    return first.reshape(-1).astype(jnp.int32)


def _fox(first_tile, fq, k_all, v_all, c_all, batch, seq, tile_len):
    tq = tk = tile_len
    nq = seq // tq
    idx = np.arange(tile_len)
    mask = jnp.asarray(np.where(idx[None, :] <= idx[:, None], 0.0, NEG_INF), F32)
    body = functools.partial(_fox_body, tq=tq, tk=tk)
    grid_spec = pltpu.PrefetchScalarGridSpec(
        num_scalar_prefetch=1,
        grid=(batch, FOX_H // 2, nq),
        in_specs=[pl.BlockSpec((tq, LANES), lambda b, h, i, _: (b * nq + i, h)),
                  pl.BlockSpec((1, seq, LANES), lambda b, h, i, _: (b, 0, h)),
                  pl.BlockSpec((1, seq, 2 * LANES), lambda b, h, i, _: (b, 0, h)),
                  pl.BlockSpec((1, 1, 2, seq), lambda b, h, i, _: (b, h, 0, 0)),
                  _const_spec(mask.shape)],
        out_specs=pl.BlockSpec((tq, LANES), lambda b, h, i, _: (b * nq + i, h)),
        scratch_shapes=[pltpu.VMEM((2 * tq, LANES), BF16), pltpu.VMEM((2 * tq, LANES), F32),
                        pltpu.VMEM((2 * tq, LANES), F32)])
    return pl.pallas_call(
        body,
        grid_spec=grid_spec,
        out_shape=jax.ShapeDtypeStruct((batch * seq, FOX_W), BF16),
        compiler_params=_params(("parallel", "parallel", "arbitrary")),
        name="fox_attention",
    )(first_tile, fq, k_all, v_all, c_all, mask)


def _fox_cached_body(q_ref, kp_ref, vp_ref, kn_ref, vn_ref, c_ref, o_ref, *, past, lq):
    q = q_ref[...]
    lane = lax.broadcasted_iota(jnp.int32, q.shape, 1)
    zero = jnp.zeros_like(q)
    qs = jnp.concatenate([jnp.where(lane < FOX_D, q, zero), jnp.where(lane >= FOX_D, q, zero)], axis=0)
    head_rows = lambda a, b, width: jnp.concatenate(
        [jnp.broadcast_to(a, (lq, width)), jnp.broadcast_to(b, (lq, width))], axis=0)
    s_past = _dot(qs, kp_ref[0].astype(BF16)) - head_rows(c_ref[0, 0, 0:1, 0:past], c_ref[0, 0, 1:2, 0:past], past)
    s_new = _dot_nt(qs, kn_ref[...]) - head_rows(c_ref[0, 0, 0:1, past:past + lq],
                                                 c_ref[0, 0, 1:2, past:past + lq], lq)
    qi = lax.broadcasted_iota(jnp.int32, s_new.shape, 0)
    ki = lax.broadcasted_iota(jnp.int32, s_new.shape, 1)
    s_new = jnp.where(ki <= jnp.where(qi >= lq, qi - lq, qi), s_new, NEG_INF)
    m = jnp.maximum(jnp.max(s_past, axis=1, keepdims=True), jnp.max(s_new, axis=1, keepdims=True))
    p_past = jnp.exp2(s_past - m)
    p_new = jnp.exp2(s_new - m)
    denom = jnp.sum(p_past, axis=1, keepdims=True) + jnp.sum(p_new, axis=1, keepdims=True)
    pb_new = p_new.astype(BF16)
    acc = _dot_nt(p_past.astype(BF16), vp_ref[0].astype(BF16))
    acc_a = (acc[0:lq] + _dot(pb_new[0:lq], vn_ref[:, 0:LANES])) / denom[0:lq]
    acc_b = (acc[lq:2 * lq] + _dot(pb_new[lq:2 * lq], vn_ref[:, LANES:2 * LANES])) / denom[lq:2 * lq]
    o_ref[...] = jnp.where(lane < FOX_D, acc_a, acc_b).astype(BF16)


def _fox_cached(fq, k_past, v_past, k_new, v_new, c_all, batch, lq):
    past = k_past.shape[2]
    new_rows = lambda w: pl.BlockSpec((lq, w), lambda b, h: (b, h))
    cache = pl.BlockSpec((1, LANES, past), lambda b, h: (b, h, 0))
    return pl.pallas_call(
        functools.partial(_fox_cached_body, past=past, lq=lq),
        grid=(batch, FOX_H // 2),
        in_specs=[new_rows(LANES), cache, cache, new_rows(LANES), new_rows(2 * LANES),
                  pl.BlockSpec((1, 1, 2, c_all.shape[-1]), lambda b, h: (b, h, 0, 0))],
        out_specs=new_rows(LANES),
        out_shape=jax.ShapeDtypeStruct((batch * lq, FOX_W), BF16),
        compiler_params=_params(("parallel", "parallel")),
        name="fox_attention_cached",
    )(fq, k_past, v_past, k_new, v_new, c_all)


def _normproj_body(x_ref, g_ref, w_ref, o_ref, ob_ref):
    y = _dot(_rms(x_ref[...], g_ref[...]).astype(BF16), w_ref[...])
    o_ref[...] = y
    ob_ref[...] = y.astype(BF16)


def _normproj(x, g, w, tm):
    n, wout = x.shape[0], w.shape[1]
    return pl.pallas_call(
        _normproj_body,
        grid=(n // tm,),
        in_specs=[pl.BlockSpec((tm, D_MODEL), lambda i: (i, 0)), _const_spec(g.shape), _const_spec(w.shape)],
        out_specs=[pl.BlockSpec((tm, wout), lambda i: (i, 0))] * 2,
        out_shape=[jax.ShapeDtypeStruct((n, wout), F32), jax.ShapeDtypeStruct((n, wout), BF16)],
        compiler_params=_params(("parallel",)),
        name="memory_kv",
    )(x, g, w)


def _post_body(*refs, has_mix, parts, seq_rows, mem_len):
    if has_mix:
        (x_ref, ro_ref, fo_ref, wo_ref, nw_ref, wxq_ref, wxo_ref, mk_ref, mv_ref, wup_ref, wdn_ref,
         o_ref) = refs
    else:
        x_ref, nw_ref, wxq_ref, wxo_ref, mk_ref, mv_ref, wup_ref, wdn_ref, o_ref = refs
    rows = [slice(r0, r0 + nrows) for r0, nrows in parts]
    xs = [x_ref[rs, :] for rs in rows]
    if has_mix:
        mixes = [_dot(ro_ref[rs, :], wo_ref[0:RET_W, :]) + _dot(fo_ref[rs, :], wo_ref[RET_W:RET_W + FOX_W, :])
                 for rs in rows]
        xs = [x + _rms(mix, nw_ref[1:2, :]) for x, mix in zip(xs, mixes)]

    qs = [_dot(_rms(x, nw_ref[2:3, :]).astype(BF16), wxq_ref[...]) for x in xs]
    ctx = []
    for (r0, nrows), q in zip(parts, qs):
        groups = []
        for g0 in range(0, nrows, min(seq_rows, nrows)):
            gs = slice(g0, g0 + min(seq_rows, nrows))
            m0 = ((r0 + g0) // seq_rows) * mem_len
            heads = []
            for hh in range(X_H):
                sl = slice(hh * X_D, (hh + 1) * X_D)
                s = _dot_nt(q[gs, sl].astype(BF16), mk_ref[m0:m0 + mem_len, sl]) * (X_D ** -0.5)
                p = jnp.exp(s - jnp.max(s, axis=-1, keepdims=True))
                o = _dot(p.astype(BF16), mv_ref[m0:m0 + mem_len, sl]) / jnp.sum(p, axis=-1, keepdims=True)
                heads.append(o.astype(BF16))
            groups.append(jnp.concatenate(heads, axis=1))
        ctx.append(groups[0] if len(groups) == 1 else jnp.concatenate(groups, axis=0))
    atts = [_dot(c, wxo_ref[...]) for c in ctx]
    xs = [x + _rms(att, nw_ref[3:4, :]) for x, att in zip(xs, atts)]

    hs = [_rms(x, nw_ref[4:5, :]).astype(BF16) for x in xs]
    accs = [None] * len(parts)
    for c in range(D_FF // D_MODEL):
        sl = slice(c * D_MODEL, (c + 1) * D_MODEL)
        for i, h in enumerate(hs):
            r = jnp.maximum(_dot(h, wup_ref[:, sl]), 0.0)
            part = _dot((r * r).astype(BF16), wdn_ref[sl, :])
            accs[i] = part if accs[i] is None else accs[i] + part
    for rs, x, acc in zip(rows, xs, accs):
        o_ref[rs, :] = x + _rms(acc, nw_ref[5:6, :])


def _post(x, mix_in, nw, wxq, wxo, mk, mv, wup, wdn, seq, tm):
    n = x.shape[0]
    mem_len = mk.shape[0] // (n // seq)
    row = lambda w: pl.BlockSpec((tm, w), lambda i: (i, 0))
    if tm <= seq:
        per_batch = seq // tm
        mem = pl.BlockSpec((mem_len, D_MODEL), lambda i: (i // per_batch, 0))
        seq_rows = tm
        nparts = POST_PARTS if tm % (POST_PARTS * 16) == 0 else 1
        parts = tuple((p * (tm // nparts), tm // nparts) for p in range(nparts))
    else:
        mem = pl.BlockSpec(((tm // seq) * mem_len, D_MODEL), lambda i: (i, 0))
        seq_rows = seq
        parts = ((0, tm),)
    args = [x]
    specs = [row(D_MODEL)]
    if mix_in is not None:
        ro, fo, wo = mix_in
        args += [ro, fo, wo]
        specs += [row(RET_W), row(FOX_W), _const_spec(wo.shape)]
    args += [nw, wxq, wxo, mk, mv, wup, wdn]
    specs += [_const_spec(nw.shape), _const_spec(wxq.shape), _const_spec(wxo.shape), mem, mem,
              _const_spec(wup.shape), _const_spec(wdn.shape)]
    return pl.pallas_call(
        functools.partial(_post_body, has_mix=mix_in is not None, parts=parts, seq_rows=seq_rows, mem_len=mem_len),
        grid=(n // tm,),
        in_specs=specs,
        out_specs=row(D_MODEL),
        out_shape=jax.ShapeDtypeStruct((n, D_MODEL), F32),
        compiler_params=_params(("parallel",)),
        name="xattn_mlp_mix" if mix_in is not None else "xattn_mlp",
    )(*args)


def _s5prep_body(lre_ref, lim_ref, ldt_ref, lre_rep_ref, lim_rep_ref, bre_ref, bim_ref,
                 are_ref, aim_ref, bbre_ref, bbim_ref):
    dt = jnp.exp(ldt_ref[...])

    def zoh(lre, lim):
        mag = jnp.exp(lre * dt)
        a_re = mag * jnp.cos(lim * dt)
        a_im = mag * jnp.sin(lim * dt)
        den = lre * lre + lim * lim
        num_re = a_re - 1.0
        f_re = (num_re * lre + a_im * lim) / den
        f_im = (a_im * lre - num_re * lim) / den
        return a_re, a_im, f_re, f_im

    a_re, a_im, _, _ = zoh(lre_ref[...], lim_ref[...])
    are_ref[...] = a_re
    aim_ref[...] = a_im
    _, _, f_re, f_im = zoh(lre_rep_ref[...], lim_rep_ref[...])
    b_re = bre_ref[...]
    b_im = bim_ref[...]
    bbre_ref[...] = f_re * b_re - f_im * b_im
    bbim_ref[...] = f_re * b_im + f_im * b_re


def _s5_discretize(lam_re, lam_im, log_dt, b_re, b_im):
    rep = lambda a: jnp.repeat(a, S5_GROUP, axis=1)
    wide = jax.ShapeDtypeStruct((S5_G, S5_P * S5_GROUP), F32)
    small = jax.ShapeDtypeStruct((S5_G, S5_P), F32)
    a_re, a_im, bb_re, bb_im = pl.pallas_call(
        _s5prep_body, out_shape=[small, small, wide, wide], name="s5_discretize",
    )(lam_re, lam_im, log_dt.reshape(S5_G, 1), rep(lam_re), rep(lam_im),
      b_re.reshape(S5_G, -1), b_im.reshape(S5_G, -1))
    a = jnp.stack([a_re.reshape(-1), a_im.reshape(-1)])
    gpc = S5_LANE_CHUNK // S5_P
    nch = S5_G // gpc
    eye = jnp.eye(gpc, dtype=F32)

    def in_blocks(bb):
        t = bb.reshape(nch, gpc, S5_P, S5_GROUP).transpose(0, 1, 3, 2)
        return (t[:, :, :, None, :] * eye[None, :, None, :, None]).reshape(nch, gpc * S5_GROUP, gpc * S5_P)

    bmat = jnp.concatenate([in_blocks(bb_re), in_blocks(bb_im)], axis=-1).astype(BF16)
    return a, bmat


def _s5_out_matrices(c_re, c_im):
    gpc = S5_LANE_CHUNK // S5_P
    nch = S5_G // gpc
    eye = jnp.eye(gpc, dtype=F32)

    def out_blocks(c):
        t = c.reshape(nch, gpc, S5_GROUP, S5_P).transpose(0, 1, 3, 2)
        return (t[:, :, :, None, :] * eye[None, :, None, :, None]).reshape(nch, gpc * S5_P, gpc * S5_GROUP)

    return jnp.concatenate([out_blocks(c_re), -out_blocks(c_im)], axis=1).astype(BF16)


def _s5_body(x_ref, nw_ref, perm_ref, win_ref, bmat_ref, a_ref, cmat_ref, d_ref, wglu_ref, st0_ref,
             o_ref, st_ref, u_scr, bu_scr, y_scr, *, ts):
    nb = x_ref.shape[0]
    rows = nb * ts
    lc = S5_LANE_CHUNK
    gw = (lc // S5_P) * S5_GROUP

    @pl.when(pl.program_id(0) == 0)
    def _():
        st_ref[...] = st0_ref[...]

    x = x_ref[...].reshape(rows, D_MODEL)
    h = _dot(perm_ref[0], _rms(x, nw_ref[0:1, :]).astype(BF16)).astype(BF16)
    u_scr[...] = _dot(h, win_ref[...])
    nv = lc // LANES
    for j in range(S5_STATE // lc):
        uj = u_scr[:, j * gw:(j + 1) * gw]
        bu = _dot(uj.astype(BF16), bmat_ref[j])
        for c in range(2 * nv):
            bu_scr[c] = bu[:, c * LANES:(c + 1) * LANES]
        a_re = [jnp.broadcast_to(a_ref[0:1, j * lc + c * LANES:j * lc + (c + 1) * LANES], (nb, LANES))
                for c in range(nv)]
        a_im = [jnp.broadcast_to(a_ref[1:2, j * lc + c * LANES:j * lc + (c + 1) * LANES], (nb, LANES))
                for c in range(nv)]

        def step(t, carry):
            idx = pl.ds(pl.multiple_of(t * nb, nb), nb)
            new = []
            for c in range(nv):
                s_re, s_im = carry[c], carry[nv + c]
                new.append(a_re[c] * s_re - a_im[c] * s_im + bu_scr[c, idx, :])
            for c in range(nv):
                s_re, s_im = carry[c], carry[nv + c]
                new.append(a_re[c] * s_im + a_im[c] * s_re + bu_scr[nv + c, idx, :])
            for c in range(2 * nv):
                bu_scr[c, idx, :] = new[c]
            return tuple(new)

        init = tuple(st_ref[:, j * lc + c * LANES:j * lc + (c + 1) * LANES] for c in range(nv)) + tuple(
            st_ref[:, S5_STATE + j * lc + c * LANES:S5_STATE + j * lc + (c + 1) * LANES] for c in range(nv))
        fin = lax.fori_loop(0, ts, step, init, unroll=True)
        for c in range(nv):
            st_ref[:, j * lc + c * LANES:j * lc + (c + 1) * LANES] = fin[c]
            st_ref[:, S5_STATE + j * lc + c * LANES:S5_STATE + j * lc + (c + 1) * LANES] = fin[nv + c]
        xs = jnp.concatenate([bu_scr[c] for c in range(2 * nv)], axis=1).astype(BF16)
        y_scr[:, j * gw:(j + 1) * gw] = _dot(xs, cmat_ref[j]) + d_ref[0:1, j * gw:(j + 1) * gw] * uj
    y = _dot(perm_ref[1], y_scr[...].astype(BF16)).astype(BF16)
    g = _dot(y, wglu_ref[...])
    out = g[:, 0:D_MODEL] * jax.nn.sigmoid(g[:, D_MODEL:2 * D_MODEL])
    o_ref[...] = (x + _rms(out, nw_ref[1:2, :])).reshape(nb, ts, D_MODEL)


def _s5_layer(x3, nw, win, bmat, a, cmat, d, wglu, st0, ts):
    nb, seq, _ = x3.shape
    rows = nb * ts
    blk = pl.BlockSpec((nb, ts, D_MODEL), lambda t: (0, t, 0))
    r = np.arange(rows)
    fwd = np.zeros((rows, rows), np.float32)
    fwd[(r % ts) * nb + r // ts, r] = 1.0
    perm = jnp.asarray(np.stack([fwd, fwd.T]), BF16)
    consts = [nw, perm, win, bmat, a, cmat, d, wglu, st0]
    return pl.pallas_call(
        functools.partial(_s5_body, ts=ts),
        grid=(seq // ts,),
        in_specs=[blk] + [_const_spec(c.shape) for c in consts],
        out_specs=[blk, pl.BlockSpec(st0.shape, lambda t: (0, 0))],
        out_shape=[jax.ShapeDtypeStruct(x3.shape, F32), jax.ShapeDtypeStruct(st0.shape, F32)],
        scratch_shapes=[pltpu.VMEM((rows, D_MODEL), F32),
                        pltpu.VMEM((2 * S5_LANE_CHUNK // LANES, rows, LANES), F32),
                        pltpu.VMEM((rows, D_MODEL), F32)],
        compiler_params=_params(("arbitrary",)),
        name="s5_layer",
    )(x3, *consts)


def _rotary_tables(pos, rows):
    half = RET_D // 2
    inv = ROPE_BASE ** (-jnp.arange(half, dtype=F32) / half)
    ang = pos.astype(F32)[:, None] * inv[None, :]
    cos = jnp.cos(ang)
    sin = jnp.sin(ang)
    cos_t = jnp.concatenate([cos, cos], axis=1)
    sin_t = jnp.concatenate([-sin, sin], axis=1)
    reps = max(1, rows // pos.shape[0])
    return jnp.tile(cos_t, (reps, 1)), jnp.tile(sin_t, (reps, 1))


def _prepare_weights(norm_w, w_in_ab, b_fox_f, w_out_ab, w_in_c, s5_lambda_re, s5_lambda_im, s5_log_dt,
                     s5_b_re, s5_b_im, s5_c_re, s5_c_im, s5_d, w_glu, w_xq, w_xo, w_up, w_down):
    w = {}
    wab = w_in_ab[0].astype(BF16)
    w["wr"] = wab[:, 0:4 * RET_W]
    w["wf"] = wab[:, 4 * RET_W:4 * RET_W + 3 * FOX_W]
    w["wl"] = jnp.pad(wab[:, 4 * RET_W + 3 * FOX_W:], ((0, 0), (0, LANES - FOX_H)))
    w["bl"] = jnp.pad(b_fox_f[0].astype(F32), (0, LANES - FOX_H)).reshape(1, LANES)
    w["wo"] = w_out_ab[0].astype(BF16)
    w["nw"] = norm_w.astype(F32)
    w["win_c"] = w_in_c[0].astype(BF16)
    w["a"], w["bmat"] = _s5_discretize(s5_lambda_re[0], s5_lambda_im[0], s5_log_dt[0], s5_b_re[0], s5_b_im[0])
    w["cmat"] = _s5_out_matrices(s5_c_re[0].astype(F32), s5_c_im[0].astype(F32))
    w["d"] = s5_d[0].astype(F32).reshape(1, D_MODEL)
    w["wglu"] = w_glu[0].astype(BF16)
    for name, arr in (("wxq", w_xq), ("wxo", w_xo), ("wup", w_up), ("wdn", w_down)):
        w[name] = arr.astype(BF16)
    return w


def _run_group(x3, pos, w, gn_w, mem_kv, ret_s0, fox_past, s5_st0):
    batch, seq, _ = x3.shape
    n = batch * seq
    tm = min(ROW_TILE, n)
    tm_b = ROW_TILE if seq >= ROW_TILE else min(n, (ROW_TILE // seq) * seq)
    x = x3.reshape(n, D_MODEL)

    cos_t, sin_t = _rotary_tables(pos, tm)
    rq, rk, rv, rg, fq, fkb, fvp, fk, fv, lf, qn2, kn2 = _inproj(
        x, w["nw"][0, 0:1], w["wr"], w["wf"], w["wl"], w["bl"], cos_t, sin_t, tm, seq)
    if tm <= seq:
        lf_t = lf
        fk = jnp.transpose(fk, (0, 3, 1, 2))
        fv = jnp.transpose(fv, (0, 3, 1, 2))
        lf3 = jnp.transpose(lf, (0, 2, 1))
    else:
        lf3 = lf.reshape(batch, seq, FOX_H)
        lf_t = jnp.transpose(lf3, (0, 2, 1))
    if fox_past is not None:
        lf_t = jnp.concatenate([jnp.transpose(fox_past[2].astype(F32), (0, 2, 1)), lf_t], axis=2)
    lk = lf_t.shape[2]
    pad = -lk % CUMSUM_CHUNK
    lf_rows = jnp.pad(lf_t.reshape(batch * FOX_H, lk), ((0, 0), (0, pad)))
    c_all = _cumsum_lanes(lf_rows).reshape(batch, FOX_H // 2, 2, lk + pad)
    if fox_past is None:
        assert tm == FOX_TILE
        first_tile = _fox_first_tiles(qn2, kn2, c_all, batch, seq, FOX_TILE)
        fo = _fox(first_tile, fq, fkb.reshape(batch, seq, FOX_W), fvp.reshape(batch, seq, FOX_H * LANES), c_all,
                  batch, seq, FOX_TILE)
    else:
        past = fox_past[0].shape[1]
        cache_t = lambda a: jnp.transpose(a, (0, 2, 3, 1)).reshape(batch, FOX_W, past)
        fo = _fox_cached(fq, cache_t(fox_past[0]), cache_t(fox_past[1]), fkb, fvp, c_all, batch, seq)

    chunk = CHUNK if fox_past is None else seq
    ro, ret_state = _retention(rq, rk, rv, rg, gn_w, ret_s0, batch, seq, chunk)

    x = _post(x, (ro, fo, w["wo"]), w["nw"][0], w["wxq"][0], w["wxo"][0], mem_kv[0][0], mem_kv[0][1],
              w["wup"][0], w["wdn"][0], seq, tm_b)

    ts = min(S5_TS, seq)
    x3n, s5_state = _s5_layer(x.reshape(batch, seq, D_MODEL), w["nw"][1], w["win_c"], w["bmat"], w["a"],
                              w["cmat"], w["d"], w["wglu"], s5_st0, ts)
    x = _post(x3n.reshape(n, D_MODEL), None, w["nw"][1], w["wxq"][1], w["wxo"][1], mem_kv[1][0], mem_kv[1][1],
              w["wup"][1], w["wdn"][1], seq, tm_b)

    new = dict(ret=ret_state[None], fox_k=fk.reshape(1, batch, seq, FOX_H, FOX_D),
               fox_v=fv.reshape(1, batch, seq, FOX_H, FOX_D), fox_logf=lf3[None],
               s5_re=s5_state[:, 0:S5_STATE].reshape(1, batch, S5_G, S5_P),
               s5_im=s5_state[:, S5_STATE:].reshape(1, batch, S5_G, S5_P))
    return x.reshape(batch, seq, D_MODEL), new


def kernel(x_prompt, x_sample, cache_ret_state, cache_fox_k, cache_fox_v, cache_fox_logf, state_s5_re, state_s5_im, cache_mem_k, cache_mem_v, mem_prompt, norm_w, w_in_ab, b_fox_f, ret_gn_w, w_out_ab, w_in_c, s5_lambda_re, s5_lambda_im, s5_log_dt, s5_b_re, s5_b_im, s5_c_re, s5_c_im, s5_d, w_glu, mem_norm_w, w_xq, w_xk, w_xv, w_xo, w_up, w_down):
    w = _prepare_weights(norm_w, w_in_ab, b_fox_f, w_out_ab, w_in_c, s5_lambda_re, s5_lambda_im, s5_log_dt,
                         s5_b_re, s5_b_im, s5_c_re, s5_c_im, s5_d, w_glu, w_xq, w_xo, w_up, w_down)
    batch, seq, _ = x_prompt.shape
    dec_batch, dec_seq, _ = x_sample.shape
    mem_len = mem_prompt.shape[1]
    depth = norm_w.shape[0]
    gn_w = ret_gn_w[0].astype(F32)

    mem_rows = mem_prompt.reshape(batch * mem_len, D_MODEL)
    mem_kv_prompt = []
    mem_k_out = []
    mem_v_out = []
    for layer in range(depth):
        wkv = jnp.concatenate([w_xk[layer], w_xv[layer]], axis=1).astype(BF16)
        kv, kvb = _normproj(mem_rows, mem_norm_w[layer].astype(F32).reshape(1, D_MODEL), wkv,
                            min(ROW_TILE, batch * mem_len))
        mem_kv_prompt.append((kvb[:, 0:D_MODEL], kvb[:, D_MODEL:]))
        mem_k_out.append(kv[:, 0:D_MODEL].reshape(batch, mem_len, X_H, X_D))
        mem_v_out.append(kv[:, D_MODEL:].reshape(batch, mem_len, X_H, X_D))

    pos_prompt = jnp.arange(seq)
    y_prompt, np_ = _run_group(
        x_prompt, pos_prompt, w, gn_w, mem_kv_prompt,
        jnp.zeros((batch, RET_H, RET_D, RET_D), F32), None, jnp.zeros((batch, 2 * S5_STATE), F32))

    past = cache_fox_k.shape[2]
    pos_sample = past + jnp.arange(dec_seq)
    mem_kv_sample = [(cache_mem_k[layer].reshape(dec_batch * mem_len, D_MODEL).astype(BF16),
                      cache_mem_v[layer].reshape(dec_batch * mem_len, D_MODEL).astype(BF16))
                     for layer in range(depth)]
    s5_st0 = jnp.concatenate([state_s5_re[0].reshape(dec_batch, S5_STATE),
                              state_s5_im[0].reshape(dec_batch, S5_STATE)], axis=1).astype(F32)
    y_sample, ns_ = _run_group(
        x_sample, pos_sample, w, gn_w, mem_kv_sample, cache_ret_state[0].astype(F32),
        (cache_fox_k[0], cache_fox_v[0], cache_fox_logf[0]), s5_st0)

    return (y_prompt, y_sample,
            np_["ret"], ns_["ret"],
            np_["fox_k"], np_["fox_v"], np_["fox_logf"],
            ns_["fox_k"], ns_["fox_v"], ns_["fox_logf"],
            np_["s5_re"], np_["s5_im"], ns_["s5_re"], ns_["s5_im"],
            jnp.stack(mem_k_out), jnp.stack(mem_v_out))
```

```python
import functools

import numpy as np
import jax
import jax.numpy as jnp
from jax import lax
from jax.experimental import pallas as pl
from jax.experimental.pallas import tpu as pltpu

F32 = jnp.float32
BF16 = jnp.bfloat16

D_MODEL = 1024
RET_H = 4
RET_D = 128
RET_W = RET_H * RET_D
CHUNK = 64
ROPE_BASE = 10000.0
FOX_H = 8
FOX_D = 64
FOX_W = FOX_H * FOX_D
S5_GROUP = 16
S5_G = D_MODEL // S5_GROUP
S5_P = 64
S5_STATE = S5_G * S5_P
X_H = 4
X_D = D_MODEL // X_H
D_FF = 4 * D_MODEL
EPS = 1e-6
NEG_INF = -1e30
LOG2E = float(np.log2(np.e))

LANES = 128
VMEM_LIMIT = 56 * 1024 * 1024

ROW_TILE = 512
POST_TILE = 1024
POST_PARTS = 2
RET_BLOCK = 256
FOX_TILE = 512
FOX_UNROLL = 4
FOX_DROP_LOG2 = 152.0
FOX_NORM_SLACK = 1.02
S5_TS = 64
S5_LANE_CHUNK = 512
CUMSUM_CHUNK = 256


def _dot(a, b):
    return jnp.dot(a, b, preferred_element_type=F32)


def _dot_nt(a, b):
    return lax.dot_general(a, b, (((1,), (1,)), ((), ())), preferred_element_type=F32)


def _dot_tn(a, b):
    return lax.dot_general(a, b, (((0,), (0,)), ((), ())), preferred_element_type=F32)


def _rms(x, w):
    return x * lax.rsqrt(jnp.mean(x * x, axis=-1, keepdims=True) + EPS) * w


def _const_spec(shape):
    nd = len(shape)
    return pl.BlockSpec(shape, lambda *_: (0,) * nd, pipeline_mode=pl.Buffered(1))


def _params(sem):
    return pltpu.CompilerParams(dimension_semantics=sem, vmem_limit_bytes=VMEM_LIMIT)


def _inproj_body(x_ref, g_ref, wr_ref, wf_ref, wl_ref, bl_ref, cos_ref, sin_ref, hsum_ref,
                 rq_ref, rk_ref, rv_ref, rg_ref, fq_ref, fkb_ref, fvp_ref, fk_ref, fv_ref, lf_ref,
                 qn_ref, kn_ref, *, head_major):
    def put_kv(dst_ref, z):
        if head_major:
            dst_ref[0] = z.T.reshape(FOX_H, FOX_D, z.shape[0])
        else:
            dst_ref[...] = z

    h = _rms(x_ref[...], g_ref[...]).astype(BF16)
    cos = cos_ref[...]
    sin = sin_ref[...]

    def rotary_to(dst_ref, z, scale):
        for hh in range(RET_H):
            sl = slice(hh * RET_D, (hh + 1) * RET_D)
            zz = z[:, sl]
            r = zz * cos + pltpu.roll(zz, RET_D // 2, 1) * sin
            if scale is not None:
                r = r * scale
            dst_ref[:, sl] = r.astype(BF16)

    rotary_to(rq_ref, _dot(h, wr_ref[:, 0:RET_W]), None)
    rotary_to(rk_ref, _dot(h, wr_ref[:, RET_W:2 * RET_W]), RET_D ** -0.5)
    rv_ref[...] = _dot(h, wr_ref[:, 2 * RET_W:3 * RET_W]).astype(BF16)
    rg_ref[...] = _dot(h, wr_ref[:, 3 * RET_W:4 * RET_W])
    def max_sq_norm(zb):
        return jnp.max(_dot(zb * zb, hsum_ref[...]), axis=0, keepdims=True)

    fqb = (_dot(h, wf_ref[:, 0:FOX_W]) * (LOG2E * FOX_D ** -0.5)).astype(BF16)
    fq_ref[...] = fqb
    qn_ref[0] = max_sq_norm(fqb)
    fk = _dot(h, wf_ref[:, FOX_W:2 * FOX_W])
    put_kv(fk_ref, fk)
    fkb = fk.astype(BF16)
    fkb_ref[...] = fkb
    kn_ref[0] = max_sq_norm(fkb)
    fv = _dot(h, wf_ref[:, 2 * FOX_W:3 * FOX_W])
    put_kv(fv_ref, fv)
    lane = lax.broadcasted_iota(jnp.int32, (fv.shape[0], LANES), 1)
    for pr in range(FOX_H // 2):
        vp = fv[:, pr * LANES:(pr + 1) * LANES]
        fvp_ref[:, (2 * pr) * LANES:(2 * pr + 1) * LANES] = jnp.where(lane < FOX_D, vp, 1.0).astype(BF16)
        fvp_ref[:, (2 * pr + 1) * LANES:(2 * pr + 2) * LANES] = jnp.where(lane >= FOX_D, vp, 1.0).astype(BF16)
    zl = _dot(h, wl_ref[...]) + bl_ref[...]
    lf = jnp.minimum(zl, 0.0) - jnp.log1p(jnp.exp(-jnp.abs(zl)))
    if head_major:
        lf_ref[0] = lf.T[0:FOX_H, :]
    else:
        lf_ref[...] = lf[:, 0:FOX_H]


def _inproj(x, g, wr, wf, wl, bl, cos_t, sin_t, tm, seq):
    n = x.shape[0]
    nt = cos_t.shape[0] // tm
    head_major = tm <= seq
    row = lambda w: pl.BlockSpec((tm, w), lambda i: (i, 0))
    tab = pl.BlockSpec((tm, LANES), lambda i: (i % nt, 0))
    shp = lambda w, dt: jax.ShapeDtypeStruct((n, w), dt)
    hsum = jnp.asarray(np.arange(FOX_W)[:, None] // FOX_D == np.arange(LANES)[None, :], BF16)
    stat_spec = pl.BlockSpec((1, 1, LANES), lambda i: (i, 0, 0))
    stat_shape = jax.ShapeDtypeStruct((n // tm, 1, LANES), F32)
    if head_major:
        per_batch = seq // tm
        kv_spec = pl.BlockSpec((1, FOX_H, FOX_D, tm), lambda i: (i // per_batch, 0, 0, i % per_batch))
        kv_shape = jax.ShapeDtypeStruct((n // seq, FOX_H, FOX_D, seq), F32)
        lf_spec = pl.BlockSpec((1, FOX_H, tm), lambda i: (i // per_batch, 0, i % per_batch))
        lf_shape = jax.ShapeDtypeStruct((n // seq, FOX_H, seq), F32)
    else:
        kv_spec, kv_shape = row(FOX_W), shp(FOX_W, F32)
        lf_spec, lf_shape = row(FOX_H), shp(FOX_H, F32)
    return pl.pallas_call(
        functools.partial(_inproj_body, head_major=head_major),
        grid=(n // tm,),
        in_specs=[row(D_MODEL), _const_spec(g.shape), _const_spec(wr.shape), _const_spec(wf.shape),
                  _const_spec(wl.shape), _const_spec(bl.shape), tab, tab, _const_spec(hsum.shape)],
        out_specs=[row(RET_W), row(RET_W), row(RET_W), row(RET_W), row(FOX_W), row(FOX_W),
                   row(FOX_H * LANES), kv_spec, kv_spec, lf_spec, stat_spec, stat_spec],
        out_shape=[shp(RET_W, BF16), shp(RET_W, BF16), shp(RET_W, BF16), shp(RET_W, F32),
                   shp(FOX_W, BF16), shp(FOX_W, BF16), shp(FOX_H * LANES, BF16), kv_shape, kv_shape,
                   lf_shape, stat_shape, stat_shape],
        compiler_params=_params(("parallel",)),
        name="inproj_ab",
    )(x, g, wr, wf, wl, bl, cos_t, sin_t, hsum)


def _cumsum_body(x_ref, u_ref, o_ref):
    rows, total = x_ref.shape
    u = u_ref[...]

    def chunk(i, carry):
        off = pl.multiple_of(i * CUMSUM_CHUNK, CUMSUM_CHUNK)
        xc = x_ref[:, pl.ds(off, CUMSUM_CHUNK)]
        hi = xc.astype(BF16)
        r1 = xc - hi.astype(F32)
        mid = r1.astype(BF16)
        lo = (r1 - mid.astype(F32)).astype(BF16)
        y = _dot(hi, u) + _dot(mid, u) + _dot(lo, u) + carry
        o_ref[:, pl.ds(off, CUMSUM_CHUNK)] = y * LOG2E
        return y[:, CUMSUM_CHUNK - 1:CUMSUM_CHUNK]

    lax.fori_loop(0, total // CUMSUM_CHUNK, chunk, jnp.zeros((rows, 1), F32))


def _cumsum_lanes(x):
    idx = np.arange(CUMSUM_CHUNK)
    u = jnp.asarray(idx[:, None] <= idx[None, :], BF16)
    return pl.pallas_call(
        _cumsum_body,
        out_shape=jax.ShapeDtypeStruct(x.shape, F32),
        compiler_params=pltpu.CompilerParams(vmem_limit_bytes=VMEM_LIMIT),
        name="logf_cumsum",
    )(x, u)


def _ret_body(q_ref, k_ref, v_ref, rg_ref, dm_ref, wq_ref, wk_ref, dec_ref, gn_ref, s0_ref,
              ro_ref, s_ref):
    @pl.when(pl.program_id(1) == 0)
    def _():
        s_ref[...] = s0_ref[...]

    heads = range(RET_H)
    sls = [slice(hh * RET_D, (hh + 1) * RET_D) for hh in heads]
    qs = [q_ref[:, sl] for sl in sls]
    ks = [k_ref[:, sl] for sl in sls]
    vs = [v_ref[:, sl] for sl in sls]
    states = [s_ref[0, hh] for hh in heads]
    scores = [(_dot_nt(qs[hh], ks[hh]) * dm_ref[hh]).astype(BF16) for hh in heads]
    inter = [_dot(qs[hh], states[hh].astype(BF16)) * wq_ref[hh] for hh in heads]
    outs = [_dot(scores[hh], vs[hh]) + inter[hh] for hh in heads]
    kws = [(ks[hh].astype(F32) * wk_ref[hh]).astype(BF16) for hh in heads]
    for hh in heads:
        s_ref[0, hh] = states[hh] * dec_ref[hh] + _dot_tn(kws[hh], vs[hh])
    for hh in heads:
        o = outs[hh]
        mu = jnp.mean(o, axis=-1, keepdims=True)
        oc = o - mu
        var = jnp.mean(oc * oc, axis=-1, keepdims=True)
        g = rg_ref[:, sls[hh]]
        ro_ref[:, sls[hh]] = (oc * lax.rsqrt(var + EPS) * gn_ref[hh] * (g * jax.nn.sigmoid(g))).astype(BF16)


def _retention(rq, rk, rv, rg, gn_w, s0, batch, seq, chunk):
    tb = min(RET_BLOCK, seq)
    nb = seq // tb
    lg = np.log(1.0 - 2.0 ** (-5.0 - np.arange(RET_H, dtype=np.float64)))
    idx = np.arange(tb, dtype=np.float64)
    visible = (idx[None, :] // chunk) <= (idx[:, None] // chunk)
    dm = np.exp(lg[:, None, None] * np.abs(idx[:, None] - idx[None, :])) * visible[None]
    wq = np.broadcast_to(np.exp(lg[:, None] * (idx[None, :] + 1.0))[:, :, None], (RET_H, tb, RET_D))
    wk = np.broadcast_to(np.exp(lg[:, None] * (tb - 1.0 - idx[None, :]))[:, :, None], (RET_H, tb, RET_D))
    dec = np.broadcast_to(np.exp(lg * tb)[:, None, None], (RET_H, 1, RET_D))
    consts = [jnp.asarray(a, F32) for a in (dm, wq, wk, dec)]
    gn = gn_w.reshape(RET_H, 1, RET_D)
    row = pl.BlockSpec((tb, RET_W), lambda b, t: (b * nb + t, 0))
    st = pl.BlockSpec((1, RET_H, RET_D, RET_D), lambda b, t: (b, 0, 0, 0))
    return pl.pallas_call(
        _ret_body,
        grid=(batch, nb),
        in_specs=[row, row, row, row] + [_const_spec(c.shape) for c in consts] + [_const_spec(gn.shape), st],
        out_specs=[row, st],
        out_shape=[jax.ShapeDtypeStruct((batch * seq, RET_W), BF16),
                   jax.ShapeDtypeStruct((batch, RET_H, RET_D, RET_D), F32)],
        compiler_params=_params(("parallel", "arbitrary")),
        name="retention",
    )(rq, rk, rv, rg, *consts, gn, s0)


def _fox_body(start_ref, q_ref, k_ref, v_ref, c_ref, mask_ref, o_ref, qs_scr, ql_scr, m_scr, acc_scr, *, tq, tk):
    n_full = pl.program_id(2)
    pair = pl.program_id(0) * pl.num_programs(1) + pl.program_id(1)
    first_a = start_ref[(2 * pair) * pl.num_programs(2) + n_full]
    first_b = start_ref[(2 * pair + 1) * pl.num_programs(2) + n_full]
    solo_first = jnp.minimum(first_a, first_b)
    first = jnp.maximum(first_a, first_b)
    solo_head = (first_b < first_a).astype(jnp.int32)
    q = q_ref[...]
    lane = lax.broadcasted_iota(jnp.int32, q.shape, 1)
    zero = jnp.zeros_like(q)
    qs_scr[0:tq, :] = jnp.where(lane < FOX_D, q, zero)
    qs_scr[tq:2 * tq, :] = jnp.where(lane >= FOX_D, q, zero)
    ql_scr[0:tq // 2, :] = qs_scr[tq // 2:tq, :]
    ql_scr[tq // 2:tq, :] = qs_scr[tq + tq // 2:2 * tq, :]
    m_scr[...] = jnp.full(m_scr.shape, -jnp.inf, F32)
    acc_scr[...] = jnp.zeros(acc_scr.shape, F32)

    def stacked_tile(q_scr, q_rows, state_rows, off, width, mask):
        s = _dot_nt(q_scr[...], k_ref[0, pl.ds(off, width), :])
        shs = [s[q_rows[hh], :] - c_ref[0, 0, hh:hh + 1, pl.ds(off, width)] for hh in range(2)]
        if mask is not None:
            shs = [sh + mask for sh in shs]
        chunks = [[sh[:, c * LANES:(c + 1) * LANES] for c in range(width // LANES)] for sh in shs]
        m_prevs = [m_scr[rows, :] for rows in state_rows]
        m_news = [jnp.maximum(m_prevs[hh], jnp.max(functools.reduce(jnp.maximum, chunks[hh]), axis=1, keepdims=True))
                  for hh in range(2)]
        ps = [jnp.concatenate([jnp.exp2(ch - m_news[hh]) for ch in chunks[hh]], axis=1).astype(BF16)
              for hh in range(2)]
        pvs = [_dot(ps[hh], v_ref[0, pl.ds(off, width), hh * LANES:(hh + 1) * LANES]) for hh in range(2)]
        for hh in range(2):
            rows = state_rows[hh]
            acc_scr[rows, :] = jnp.exp2(m_prevs[hh] - m_news[hh]) * acc_scr[rows, :] + pvs[hh]
            m_scr[rows, :] = m_news[hh]

    head_rows = [slice(hh * tq, (hh + 1) * tq) for hh in range(2)]

    def tile(kj):
        stacked_tile(qs_scr, head_rows, head_rows, pl.multiple_of(kj * tk, tk), tk, None)

    def diagonal_tile():
        hq = tq // 2
        off = pl.multiple_of(n_full * tk, tk)
        stacked_tile(qs_scr, head_rows, head_rows, off, hq, mask_ref[...])
        lower = [slice(hh * tq + hq, (hh + 1) * tq) for hh in range(2)]
        stacked_tile(ql_scr, [slice(hh * hq, (hh + 1) * hq) for hh in range(2)], lower,
                     pl.multiple_of(off + hq, hq), hq, mask_ref[0:hq, :])

    def solo_tile(kj):
        off = pl.multiple_of(kj * tk, tk)
        rows = pl.ds(pl.multiple_of(solo_head * tq, tq), tq)
        sh = (_dot_nt(qs_scr[rows, :], k_ref[0, pl.ds(off, tk), :])
              - c_ref[0, 0, pl.ds(solo_head, 1), pl.ds(off, tk)])
        chunks = [sh[:, c * LANES:(c + 1) * LANES] for c in range(tk // LANES)]
        m_prev = m_scr[rows, :]
        m_new = jnp.maximum(m_prev, jnp.max(functools.reduce(jnp.maximum, chunks), axis=1, keepdims=True))
        p = jnp.concatenate([jnp.exp2(ch - m_new) for ch in chunks], axis=1).astype(BF16)
        pv = _dot(p, v_ref[0, pl.ds(off, tk), :])
        pv = jnp.where(solo_head == 0, pv[:, 0:LANES], pv[:, LANES:2 * LANES])
        acc_scr[rows, :] = jnp.exp2(m_prev - m_new) * acc_scr[rows, :] + pv
        m_scr[rows, :] = m_new

    def run_tiles(start, count, visit):
        def group(i, carry):
            for j in range(FOX_UNROLL):
                visit(start + FOX_UNROLL * i + j)
            return carry

        lax.fori_loop(0, count // FOX_UNROLL, group, 0)
        span = FOX_UNROLL // 2
        while span >= 1:
            @pl.when((count // span) % 2 == 1)
            def _(span=span):
                base = start + (count // (2 * span)) * (2 * span)
                for j in range(span):
                    visit(base + j)
            span //= 2

    run_tiles(solo_first, first - solo_first, solo_tile)
    run_tiles(first, n_full - first, tile)
    diagonal_tile()

    acc_a = acc_scr[0:tq, :]
    acc_b = acc_scr[tq:2 * tq, :]
    half = LANES // 2
    o_ref[...] = jnp.where(lane < FOX_D, acc_a / pltpu.roll(acc_a, half, 1),
                           acc_b / pltpu.roll(acc_b, half, 1)).astype(BF16)


def _fox_first_tiles(qn2, kn2, c_all, batch, seq, tile_len):
    nq = seq // tile_len
    qn = jnp.sqrt(qn2[:, 0, 0:FOX_H]).reshape(batch, nq, FOX_H).transpose(0, 2, 1)
    kn = jnp.sqrt(kn2[:, 0, 0:FOX_H]).reshape(batch, nq, FOX_H).transpose(0, 2, 1)
    c = c_all[..., 0:seq].reshape(batch, FOX_H, nq, tile_len)
    c_first, c_last = c[..., 0], c[..., tile_len - 1]
    dots = FOX_NORM_SLACK * (qn[..., :, None] * kn[..., None, :] + (qn * kn)[..., :, None])
    gap = dots + c_first[..., :, None] - c_last[..., None, :]
    before = jnp.arange(nq)[None, :] < jnp.arange(nq)[:, None]
    droppable = (gap < -FOX_DROP_LOG2) & before
    first = jnp.sum(jnp.cumprod(droppable.astype(jnp.int32), axis=-1), axis=-1)
    return first.reshape(-1).astype(jnp.int32)


def _fox(first_tile, fq, k_all, v_all, c_all, batch, seq, tile_len):
    tq = tk = tile_len
    nq = seq // tq
    half = tile_len // 2
    mask = jnp.asarray(np.where(np.arange(half)[None, :] <= np.arange(tile_len)[:, None], 0.0, NEG_INF), F32)
    body = functools.partial(_fox_body, tq=tq, tk=tk)
    grid_spec = pltpu.PrefetchScalarGridSpec(
        num_scalar_prefetch=1,
        grid=(batch, FOX_H // 2, nq),
        in_specs=[pl.BlockSpec((tq, LANES), lambda b, h, i, _: (b * nq + i, h)),
                  pl.BlockSpec((1, seq, LANES), lambda b, h, i, _: (b, 0, h)),
                  pl.BlockSpec((1, seq, 2 * LANES), lambda b, h, i, _: (b, 0, h)),
                  pl.BlockSpec((1, 1, 2, seq), lambda b, h, i, _: (b, h, 0, 0)),
                  _const_spec(mask.shape)],
        out_specs=pl.BlockSpec((tq, LANES), lambda b, h, i, _: (b * nq + i, h)),
        scratch_shapes=[pltpu.VMEM((2 * tq, LANES), BF16), pltpu.VMEM((tq, LANES), BF16),
                        pltpu.VMEM((2 * tq, LANES), F32), pltpu.VMEM((2 * tq, LANES), F32)])
    return pl.pallas_call(
        body,
        grid_spec=grid_spec,
        out_shape=jax.ShapeDtypeStruct((batch * seq, FOX_W), BF16),
        compiler_params=_params(("parallel", "parallel", "arbitrary")),
        name="fox_attention",
    )(first_tile, fq, k_all, v_all, c_all, mask)


def _fox_cached_body(q_ref, kp_ref, vp_ref, kn_ref, vn_ref, c_ref, o_ref, *, past, lq):
    q = q_ref[...]
    lane = lax.broadcasted_iota(jnp.int32, q.shape, 1)
    zero = jnp.zeros_like(q)
    qs = jnp.concatenate([jnp.where(lane < FOX_D, q, zero), jnp.where(lane >= FOX_D, q, zero)], axis=0)
    head_rows = lambda a, b, width: jnp.concatenate(
        [jnp.broadcast_to(a, (lq, width)), jnp.broadcast_to(b, (lq, width))], axis=0)
    s_past = _dot(qs, kp_ref[0].astype(BF16)) - head_rows(c_ref[0, 0, 0:1, 0:past], c_ref[0, 0, 1:2, 0:past], past)
    s_new = _dot_nt(qs, kn_ref[...]) - head_rows(c_ref[0, 0, 0:1, past:past + lq],
                                                 c_ref[0, 0, 1:2, past:past + lq], lq)
    qi = lax.broadcasted_iota(jnp.int32, s_new.shape, 0)
    ki = lax.broadcasted_iota(jnp.int32, s_new.shape, 1)
    s_new = jnp.where(ki <= jnp.where(qi >= lq, qi - lq, qi), s_new, NEG_INF)
    m = jnp.maximum(jnp.max(s_past, axis=1, keepdims=True), jnp.max(s_new, axis=1, keepdims=True))
    p_past = jnp.exp2(s_past - m)
    p_new = jnp.exp2(s_new - m)
    denom = jnp.sum(p_past, axis=1, keepdims=True) + jnp.sum(p_new, axis=1, keepdims=True)
    pb_new = p_new.astype(BF16)
    acc = _dot_nt(p_past.astype(BF16), vp_ref[0].astype(BF16))
    acc_a = (acc[0:lq] + _dot(pb_new[0:lq], vn_ref[:, 0:LANES])) / denom[0:lq]
    acc_b = (acc[lq:2 * lq] + _dot(pb_new[lq:2 * lq], vn_ref[:, LANES:2 * LANES])) / denom[lq:2 * lq]
    o_ref[...] = jnp.where(lane < FOX_D, acc_a, acc_b).astype(BF16)


def _fox_cached(fq, k_past, v_past, k_new, v_new, c_all, batch, lq):
    past = k_past.shape[2]
    new_rows = lambda w: pl.BlockSpec((lq, w), lambda b, h: (b, h))
    cache = pl.BlockSpec((1, LANES, past), lambda b, h: (b, h, 0))
    return pl.pallas_call(
        functools.partial(_fox_cached_body, past=past, lq=lq),
        grid=(batch, FOX_H // 2),
        in_specs=[new_rows(LANES), cache, cache, new_rows(LANES), new_rows(2 * LANES),
                  pl.BlockSpec((1, 1, 2, c_all.shape[-1]), lambda b, h: (b, h, 0, 0))],
        out_specs=new_rows(LANES),
        out_shape=jax.ShapeDtypeStruct((batch * lq, FOX_W), BF16),
        compiler_params=_params(("parallel", "parallel")),
        name="fox_attention_cached",
    )(fq, k_past, v_past, k_new, v_new, c_all)


def _normproj_body(x_ref, g_ref, w_ref, o_ref, ob_ref):
    y = _dot(_rms(x_ref[...], g_ref[...]).astype(BF16), w_ref[...])
    o_ref[...] = y
    ob_ref[...] = y.astype(BF16)


def _normproj(x, g, w, tm):
    n, wout = x.shape[0], w.shape[1]
    return pl.pallas_call(
        _normproj_body,
        grid=(n // tm,),
        in_specs=[pl.BlockSpec((tm, D_MODEL), lambda i: (i, 0)), _const_spec(g.shape), _const_spec(w.shape)],
        out_specs=[pl.BlockSpec((tm, wout), lambda i: (i, 0))] * 2,
        out_shape=[jax.ShapeDtypeStruct((n, wout), F32), jax.ShapeDtypeStruct((n, wout), BF16)],
        compiler_params=_params(("parallel",)),
        name="memory_kv",
    )(x, g, w)


def _post_body(*refs, has_mix, parts, seq_rows, mem_len):
    if has_mix:
        (x_ref, ro_ref, fo_ref, wo_ref, nw_ref, wxq_ref, wxo_ref, mk_ref, mv_ref, wup_ref, wdn_ref,
         o_ref) = refs
    else:
        x_ref, nw_ref, wxq_ref, wxo_ref, mk_ref, mv_ref, wup_ref, wdn_ref, o_ref = refs
    rows = [slice(r0, r0 + nrows) for r0, nrows in parts]
    xs = [x_ref[rs, :] for rs in rows]
    if has_mix:
        mixes = [_dot(ro_ref[rs, :], wo_ref[0:RET_W, :]) + _dot(fo_ref[rs, :], wo_ref[RET_W:RET_W + FOX_W, :])
                 for rs in rows]
        xs = [x + _rms(mix, nw_ref[1:2, :]) for x, mix in zip(xs, mixes)]

    qs = [_dot(_rms(x, nw_ref[2:3, :]).astype(BF16), wxq_ref[...]) for x in xs]
    ctx = []
    for (r0, nrows), q in zip(parts, qs):
        groups = []
        for g0 in range(0, nrows, min(seq_rows, nrows)):
            gs = slice(g0, g0 + min(seq_rows, nrows))
            m0 = ((r0 + g0) // seq_rows) * mem_len
            heads = []
            for hh in range(X_H):
                sl = slice(hh * X_D, (hh + 1) * X_D)
                s = _dot_nt(q[gs, sl].astype(BF16), mk_ref[m0:m0 + mem_len, sl]) * (X_D ** -0.5)
                p = jnp.exp(s - jnp.max(s, axis=-1, keepdims=True))
                o = _dot(p.astype(BF16), mv_ref[m0:m0 + mem_len, sl]) / jnp.sum(p, axis=-1, keepdims=True)
                heads.append(o.astype(BF16))
            groups.append(jnp.concatenate(heads, axis=1))
        ctx.append(groups[0] if len(groups) == 1 else jnp.concatenate(groups, axis=0))
    atts = [_dot(c, wxo_ref[...]) for c in ctx]
    xs = [x + _rms(att, nw_ref[3:4, :]) for x, att in zip(xs, atts)]

    hs = [_rms(x, nw_ref[4:5, :]).astype(BF16) for x in xs]
    accs = [None] * len(parts)
    for c in range(D_FF // D_MODEL):
        sl = slice(c * D_MODEL, (c + 1) * D_MODEL)
        for i, h in enumerate(hs):
            r = jnp.maximum(_dot(h, wup_ref[:, sl]), 0.0)
            part = _dot((r * r).astype(BF16), wdn_ref[sl, :])
            accs[i] = part if accs[i] is None else accs[i] + part
    for rs, x, acc in zip(rows, xs, accs):
        o_ref[rs, :] = x + _rms(acc, nw_ref[5:6, :])


def _post(x, mix_in, nw, wxq, wxo, mk, mv, wup, wdn, seq, tm):
    n = x.shape[0]
    mem_len = mk.shape[0] // (n // seq)
    row = lambda w: pl.BlockSpec((tm, w), lambda i: (i, 0))
    if tm <= seq:
        per_batch = seq // tm
        mem = pl.BlockSpec((mem_len, D_MODEL), lambda i: (i // per_batch, 0))
        seq_rows = tm
        nparts = POST_PARTS if tm % (POST_PARTS * 16) == 0 else 1
        parts = tuple((p * (tm // nparts), tm // nparts) for p in range(nparts))
    else:
        mem = pl.BlockSpec(((tm // seq) * mem_len, D_MODEL), lambda i: (i, 0))
        seq_rows = seq
        parts = ((0, tm),)
    args = [x]
    specs = [row(D_MODEL)]
    if mix_in is not None:
        ro, fo, wo = mix_in
        args += [ro, fo, wo]
        specs += [row(RET_W), row(FOX_W), _const_spec(wo.shape)]
    args += [nw, wxq, wxo, mk, mv, wup, wdn]
    specs += [_const_spec(nw.shape), _const_spec(wxq.shape), _const_spec(wxo.shape), mem, mem,
              _const_spec(wup.shape), _const_spec(wdn.shape)]
    return pl.pallas_call(
        functools.partial(_post_body, has_mix=mix_in is not None, parts=parts, seq_rows=seq_rows, mem_len=mem_len),
        grid=(n // tm,),
        in_specs=specs,
        out_specs=row(D_MODEL),
        out_shape=jax.ShapeDtypeStruct((n, D_MODEL), F32),
        compiler_params=_params(("parallel",)),
        name="xattn_mlp_mix" if mix_in is not None else "xattn_mlp",
    )(*args)


def _s5prep_body(lre_ref, lim_ref, ldt_ref, lre_rep_ref, lim_rep_ref, bre_ref, bim_ref,
                 are_ref, aim_ref, bbre_ref, bbim_ref):
    dt = jnp.exp(ldt_ref[...])

    def zoh(lre, lim):
        mag = jnp.exp(lre * dt)
        a_re = mag * jnp.cos(lim * dt)
        a_im = mag * jnp.sin(lim * dt)
        den = lre * lre + lim * lim
        num_re = a_re - 1.0
        f_re = (num_re * lre + a_im * lim) / den
        f_im = (a_im * lre - num_re * lim) / den
        return a_re, a_im, f_re, f_im

    a_re, a_im, _, _ = zoh(lre_ref[...], lim_ref[...])
    are_ref[...] = a_re
    aim_ref[...] = a_im
    _, _, f_re, f_im = zoh(lre_rep_ref[...], lim_rep_ref[...])
    b_re = bre_ref[...]
    b_im = bim_ref[...]
    bbre_ref[...] = f_re * b_re - f_im * b_im
    bbim_ref[...] = f_re * b_im + f_im * b_re


def _s5_discretize(lam_re, lam_im, log_dt, b_re, b_im):
    rep = lambda a: jnp.repeat(a, S5_GROUP, axis=1)
    wide = jax.ShapeDtypeStruct((S5_G, S5_P * S5_GROUP), F32)
    small = jax.ShapeDtypeStruct((S5_G, S5_P), F32)
    a_re, a_im, bb_re, bb_im = pl.pallas_call(
        _s5prep_body, out_shape=[small, small, wide, wide], name="s5_discretize",
    )(lam_re, lam_im, log_dt.reshape(S5_G, 1), rep(lam_re), rep(lam_im),
      b_re.reshape(S5_G, -1), b_im.reshape(S5_G, -1))
    a = jnp.stack([a_re.reshape(-1), a_im.reshape(-1)])
    gpc = S5_LANE_CHUNK // S5_P
    nch = S5_G // gpc
    eye = jnp.eye(gpc, dtype=F32)

    def in_blocks(bb):
        t = bb.reshape(nch, gpc, S5_P, S5_GROUP).transpose(0, 1, 3, 2)
        return (t[:, :, :, None, :] * eye[None, :, None, :, None]).reshape(nch, gpc * S5_GROUP, gpc * S5_P)

    bmat = jnp.concatenate([in_blocks(bb_re), in_blocks(bb_im)], axis=-1).astype(BF16)
    return a, bmat


def _s5_out_matrices(c_re, c_im):
    gpc = S5_LANE_CHUNK // S5_P
    nch = S5_G // gpc
    eye = jnp.eye(gpc, dtype=F32)

    def out_blocks(c):
        t = c.reshape(nch, gpc, S5_GROUP, S5_P).transpose(0, 1, 3, 2)
        return (t[:, :, :, None, :] * eye[None, :, None, :, None]).reshape(nch, gpc * S5_P, gpc * S5_GROUP)

    return jnp.concatenate([out_blocks(c_re), -out_blocks(c_im)], axis=1).astype(BF16)


def _s5_body(x_ref, nw_ref, perm_ref, win_ref, bmat_ref, a_ref, cmat_ref, d_ref, wglu_ref, st0_ref,
             o_ref, st_ref, u_scr, bu_scr, y_scr, *, ts):
    nb = x_ref.shape[0]
    rows = nb * ts
    lc = S5_LANE_CHUNK
    gw = (lc // S5_P) * S5_GROUP

    @pl.when(pl.program_id(0) == 0)
    def _():
        st_ref[...] = st0_ref[...]

    x = x_ref[...].reshape(rows, D_MODEL)
    h = _dot(perm_ref[0], _rms(x, nw_ref[0:1, :]).astype(BF16)).astype(BF16)
    u_scr[...] = _dot(h, win_ref[...])
    nv = lc // LANES
    for j in range(S5_STATE // lc):
        uj = u_scr[:, j * gw:(j + 1) * gw]
        bu = _dot(uj.astype(BF16), bmat_ref[j])
        for c in range(2 * nv):
            bu_scr[c] = bu[:, c * LANES:(c + 1) * LANES]
        a_re = [jnp.broadcast_to(a_ref[0:1, j * lc + c * LANES:j * lc + (c + 1) * LANES], (nb, LANES))
                for c in range(nv)]
        a_im = [jnp.broadcast_to(a_ref[1:2, j * lc + c * LANES:j * lc + (c + 1) * LANES], (nb, LANES))
                for c in range(nv)]

        def step(t, carry):
            idx = pl.ds(pl.multiple_of(t * nb, nb), nb)
            new = []
            for c in range(nv):
                s_re, s_im = carry[c], carry[nv + c]
                new.append(a_re[c] * s_re - a_im[c] * s_im + bu_scr[c, idx, :])
            for c in range(nv):
                s_re, s_im = carry[c], carry[nv + c]
                new.append(a_re[c] * s_im + a_im[c] * s_re + bu_scr[nv + c, idx, :])
            for c in range(2 * nv):
                bu_scr[c, idx, :] = new[c]
            return tuple(new)

        init = tuple(st_ref[:, j * lc + c * LANES:j * lc + (c + 1) * LANES] for c in range(nv)) + tuple(
            st_ref[:, S5_STATE + j * lc + c * LANES:S5_STATE + j * lc + (c + 1) * LANES] for c in range(nv))
        fin = lax.fori_loop(0, ts, step, init, unroll=True)
        for c in range(nv):
            st_ref[:, j * lc + c * LANES:j * lc + (c + 1) * LANES] = fin[c]
            st_ref[:, S5_STATE + j * lc + c * LANES:S5_STATE + j * lc + (c + 1) * LANES] = fin[nv + c]
        xs = jnp.concatenate([bu_scr[c] for c in range(2 * nv)], axis=1).astype(BF16)
        y_scr[:, j * gw:(j + 1) * gw] = _dot(xs, cmat_ref[j]) + d_ref[0:1, j * gw:(j + 1) * gw] * uj
    y = _dot(perm_ref[1], y_scr[...].astype(BF16)).astype(BF16)
    g = _dot(y, wglu_ref[...])
    out = g[:, 0:D_MODEL] * jax.nn.sigmoid(g[:, D_MODEL:2 * D_MODEL])
    o_ref[...] = (x + _rms(out, nw_ref[1:2, :])).reshape(nb, ts, D_MODEL)


def _s5_layer(x3, nw, win, bmat, a, cmat, d, wglu, st0, ts):
    nb, seq, _ = x3.shape
    rows = nb * ts
    blk = pl.BlockSpec((nb, ts, D_MODEL), lambda t: (0, t, 0))
    r = np.arange(rows)
    fwd = np.zeros((rows, rows), np.float32)
    fwd[(r % ts) * nb + r // ts, r] = 1.0
    perm = jnp.asarray(np.stack([fwd, fwd.T]), BF16)
    consts = [nw, perm, win, bmat, a, cmat, d, wglu, st0]
    return pl.pallas_call(
        functools.partial(_s5_body, ts=ts),
        grid=(seq // ts,),
        in_specs=[blk] + [_const_spec(c.shape) for c in consts],
        out_specs=[blk, pl.BlockSpec(st0.shape, lambda t: (0, 0))],
        out_shape=[jax.ShapeDtypeStruct(x3.shape, F32), jax.ShapeDtypeStruct(st0.shape, F32)],
        scratch_shapes=[pltpu.VMEM((rows, D_MODEL), F32),
                        pltpu.VMEM((2 * S5_LANE_CHUNK // LANES, rows, LANES), F32),
                        pltpu.VMEM((rows, D_MODEL), F32)],
        compiler_params=_params(("arbitrary",)),
        name="s5_layer",
    )(x3, *consts)


def _rotary_tables(pos, rows):
    half = RET_D // 2
    inv = ROPE_BASE ** (-jnp.arange(half, dtype=F32) / half)
    ang = pos.astype(F32)[:, None] * inv[None, :]
    cos = jnp.cos(ang)
    sin = jnp.sin(ang)
    cos_t = jnp.concatenate([cos, cos], axis=1)
    sin_t = jnp.concatenate([-sin, sin], axis=1)
    reps = max(1, rows // pos.shape[0])
    return jnp.tile(cos_t, (reps, 1)), jnp.tile(sin_t, (reps, 1))


def _prepare_weights(norm_w, w_in_ab, b_fox_f, w_out_ab, w_in_c, s5_lambda_re, s5_lambda_im, s5_log_dt,
                     s5_b_re, s5_b_im, s5_c_re, s5_c_im, s5_d, w_glu, w_xq, w_xo, w_up, w_down):
    w = {}
    wab = w_in_ab[0].astype(BF16)
    w["wr"] = wab[:, 0:4 * RET_W]
    w["wf"] = wab[:, 4 * RET_W:4 * RET_W + 3 * FOX_W]
    w["wl"] = jnp.pad(wab[:, 4 * RET_W + 3 * FOX_W:], ((0, 0), (0, LANES - FOX_H)))
    w["bl"] = jnp.pad(b_fox_f[0].astype(F32), (0, LANES - FOX_H)).reshape(1, LANES)
    w["wo"] = w_out_ab[0].astype(BF16)
    w["nw"] = norm_w.astype(F32)
    w["win_c"] = w_in_c[0].astype(BF16)
    w["a"], w["bmat"] = _s5_discretize(s5_lambda_re[0], s5_lambda_im[0], s5_log_dt[0], s5_b_re[0], s5_b_im[0])
    w["cmat"] = _s5_out_matrices(s5_c_re[0].astype(F32), s5_c_im[0].astype(F32))
    w["d"] = s5_d[0].astype(F32).reshape(1, D_MODEL)
    w["wglu"] = w_glu[0].astype(BF16)
    for name, arr in (("wxq", w_xq), ("wxo", w_xo), ("wup", w_up), ("wdn", w_down)):
        w[name] = arr.astype(BF16)
    return w


def _run_group(x3, pos, w, gn_w, mem_kv, ret_s0, fox_past, s5_st0):
    batch, seq, _ = x3.shape
    n = batch * seq
    tm = min(ROW_TILE, n)
    post_tile = lambda t: t if seq >= t else min(n, (t // seq) * seq)
    x = x3.reshape(n, D_MODEL)

    cos_t, sin_t = _rotary_tables(pos, tm)
    rq, rk, rv, rg, fq, fkb, fvp, fk, fv, lf, qn2, kn2 = _inproj(
        x, w["nw"][0, 0:1], w["wr"], w["wf"], w["wl"], w["bl"], cos_t, sin_t, tm, seq)
    if tm <= seq:
        lf_t = lf
        fk = jnp.transpose(fk, (0, 3, 1, 2))
        fv = jnp.transpose(fv, (0, 3, 1, 2))
        lf3 = jnp.transpose(lf, (0, 2, 1))
    else:
        lf3 = lf.reshape(batch, seq, FOX_H)
        lf_t = jnp.transpose(lf3, (0, 2, 1))
    if fox_past is not None:
        lf_t = jnp.concatenate([jnp.transpose(fox_past[2].astype(F32), (0, 2, 1)), lf_t], axis=2)
    lk = lf_t.shape[2]
    pad = -lk % CUMSUM_CHUNK
    lf_rows = jnp.pad(lf_t.reshape(batch * FOX_H, lk), ((0, 0), (0, pad)))
    c_all = _cumsum_lanes(lf_rows).reshape(batch, FOX_H // 2, 2, lk + pad)
    if fox_past is None:
        assert tm == FOX_TILE
        first_tile = _fox_first_tiles(qn2, kn2, c_all, batch, seq, FOX_TILE)
        fo = _fox(first_tile, fq, fkb.reshape(batch, seq, FOX_W), fvp.reshape(batch, seq, FOX_H * LANES), c_all,
                  batch, seq, FOX_TILE)
    else:
        past = fox_past[0].shape[1]
        cache_t = lambda a: jnp.transpose(a, (0, 2, 3, 1)).reshape(batch, FOX_W, past)
        fo = _fox_cached(fq, cache_t(fox_past[0]), cache_t(fox_past[1]), fkb, fvp, c_all, batch, seq)

    chunk = CHUNK if fox_past is None else seq
    ro, ret_state = _retention(rq, rk, rv, rg, gn_w, ret_s0, batch, seq, chunk)

    x = _post(x, (ro, fo, w["wo"]), w["nw"][0], w["wxq"][0], w["wxo"][0], mem_kv[0][0], mem_kv[0][1],
              w["wup"][0], w["wdn"][0], seq, post_tile(ROW_TILE))

    ts = min(S5_TS, seq)
    x3n, s5_state = _s5_layer(x.reshape(batch, seq, D_MODEL), w["nw"][1], w["win_c"], w["bmat"], w["a"],
                              w["cmat"], w["d"], w["wglu"], s5_st0, ts)
    x = _post(x3n.reshape(n, D_MODEL), None, w["nw"][1], w["wxq"][1], w["wxo"][1], mem_kv[1][0], mem_kv[1][1],
              w["wup"][1], w["wdn"][1], seq, post_tile(POST_TILE))

    new = dict(ret=ret_state[None], fox_k=fk.reshape(1, batch, seq, FOX_H, FOX_D),
               fox_v=fv.reshape(1, batch, seq, FOX_H, FOX_D), fox_logf=lf3[None],
               s5_re=s5_state[:, 0:S5_STATE].reshape(1, batch, S5_G, S5_P),
               s5_im=s5_state[:, S5_STATE:].reshape(1, batch, S5_G, S5_P))
    return x.reshape(batch, seq, D_MODEL), new


def kernel(x_prompt, x_sample, cache_ret_state, cache_fox_k, cache_fox_v, cache_fox_logf, state_s5_re, state_s5_im, cache_mem_k, cache_mem_v, mem_prompt, norm_w, w_in_ab, b_fox_f, ret_gn_w, w_out_ab, w_in_c, s5_lambda_re, s5_lambda_im, s5_log_dt, s5_b_re, s5_b_im, s5_c_re, s5_c_im, s5_d, w_glu, mem_norm_w, w_xq, w_xk, w_xv, w_xo, w_up, w_down):
    w = _prepare_weights(norm_w, w_in_ab, b_fox_f, w_out_ab, w_in_c, s5_lambda_re, s5_lambda_im, s5_log_dt,
                         s5_b_re, s5_b_im, s5_c_re, s5_c_im, s5_d, w_glu, w_xq, w_xo, w_up, w_down)
    batch, seq, _ = x_prompt.shape
    dec_batch, dec_seq, _ = x_sample.shape
    mem_len = mem_prompt.shape[1]
    depth = norm_w.shape[0]
    gn_w = ret_gn_w[0].astype(F32)

    mem_rows = mem_prompt.reshape(batch * mem_len, D_MODEL)
    mem_kv_prompt = []
    mem_k_out = []
    mem_v_out = []
    for layer in range(depth):
        wkv = jnp.concatenate([w_xk[layer], w_xv[layer]], axis=1).astype(BF16)
        kv, kvb = _normproj(mem_rows, mem_norm_w[layer].astype(F32).reshape(1, D_MODEL), wkv,
                            min(ROW_TILE, batch * mem_len))
        mem_kv_prompt.append((kvb[:, 0:D_MODEL], kvb[:, D_MODEL:]))
        mem_k_out.append(kv[:, 0:D_MODEL].reshape(batch, mem_len, X_H, X_D))
        mem_v_out.append(kv[:, D_MODEL:].reshape(batch, mem_len, X_H, X_D))

    pos_prompt = jnp.arange(seq)
    y_prompt, np_ = _run_group(
        x_prompt, pos_prompt, w, gn_w, mem_kv_prompt,
        jnp.zeros((batch, RET_H, RET_D, RET_D), F32), None, jnp.zeros((batch, 2 * S5_STATE), F32))

    past = cache_fox_k.shape[2]
    pos_sample = past + jnp.arange(dec_seq)
    mem_kv_sample = [(cache_mem_k[layer].reshape(dec_batch * mem_len, D_MODEL).astype(BF16),
                      cache_mem_v[layer].reshape(dec_batch * mem_len, D_MODEL).astype(BF16))
                     for layer in range(depth)]
    s5_st0 = jnp.concatenate([state_s5_re[0].reshape(dec_batch, S5_STATE),
                              state_s5_im[0].reshape(dec_batch, S5_STATE)], axis=1).astype(F32)
    y_sample, ns_ = _run_group(
        x_sample, pos_sample, w, gn_w, mem_kv_sample, cache_ret_state[0].astype(F32),
        (cache_fox_k[0], cache_fox_v[0], cache_fox_logf[0]), s5_st0)

    return (y_prompt, y_sample,
            np_["ret"], ns_["ret"],
            np_["fox_k"], np_["fox_v"], np_["fox_logf"],
            ns_["fox_k"], ns_["fox_v"], ns_["fox_logf"],
            np_["s5_re"], np_["s5_im"], ns_["s5_re"], ns_["s5_im"],
            jnp.stack(mem_k_out), jnp.stack(mem_v_out))
```

```python
import functools

import numpy as np
import jax
import jax.numpy as jnp
from jax import lax
from jax.experimental import pallas as pl
from jax.experimental.pallas import tpu as pltpu

F32 = jnp.float32
BF16 = jnp.bfloat16

D_MODEL = 1024
RET_H = 4
RET_D = 128
RET_W = RET_H * RET_D
CHUNK = 64
ROPE_BASE = 10000.0
FOX_H = 8
FOX_D = 64
FOX_W = FOX_H * FOX_D
S5_GROUP = 16
S5_G = D_MODEL // S5_GROUP
S5_P = 64
S5_STATE = S5_G * S5_P
X_H = 4
X_D = D_MODEL // X_H
D_FF = 4 * D_MODEL
EPS = 1e-6
NEG_INF = -1e30
LOG2E = float(np.log2(np.e))

LANES = 128
VMEM_LIMIT = 56 * 1024 * 1024

ROW_TILE = 512
POST_TILE = 1024
POST_PARTS = 2
RET_BLOCK = 256
FOX_TILE = 512
FOX_UNROLL = 4
FOX_DROP_LOG2 = 152.0
FOX_NORM_SLACK = 1.02
S5_TS = 64
S5_LANE_CHUNK = 512
CUMSUM_CHUNK = 256


def _dot(a, b):
    return jnp.dot(a, b, preferred_element_type=F32)


def _dot_nt(a, b):
    return lax.dot_general(a, b, (((1,), (1,)), ((), ())), preferred_element_type=F32)


def _dot_tn(a, b):
    return lax.dot_general(a, b, (((0,), (0,)), ((), ())), preferred_element_type=F32)


def _rms(x, w):
    return x * lax.rsqrt(jnp.mean(x * x, axis=-1, keepdims=True) + EPS) * w


def _const_spec(shape):
    nd = len(shape)
    return pl.BlockSpec(shape, lambda *_: (0,) * nd, pipeline_mode=pl.Buffered(1))


def _params(sem):
    return pltpu.CompilerParams(dimension_semantics=sem, vmem_limit_bytes=VMEM_LIMIT)


def _retention_block(qs, ks, vs, gs, dm_ref, wq_ref, wk_ref, dec_ref, gn_ref, s_ref):
    heads = range(RET_H)
    states = [s_ref[0, hh] for hh in heads]
    scores = [(_dot_nt(qs[hh], ks[hh]) * dm_ref[hh]).astype(BF16) for hh in heads]
    inter = [_dot(qs[hh], states[hh].astype(BF16)) * wq_ref[hh] for hh in heads]
    outs = [_dot(scores[hh], vs[hh]) + inter[hh] for hh in heads]
    kws = [(ks[hh].astype(F32) * wk_ref[hh]).astype(BF16) for hh in heads]
    for hh in heads:
        s_ref[0, hh] = states[hh] * dec_ref[hh] + _dot_tn(kws[hh], vs[hh])
    res = []
    for hh in heads:
        o = outs[hh]
        mu = jnp.mean(o, axis=-1, keepdims=True)
        oc = o - mu
        var = jnp.mean(oc * oc, axis=-1, keepdims=True)
        g = gs[hh]
        res.append((oc * lax.rsqrt(var + EPS) * gn_ref[hh] * (g * jax.nn.sigmoid(g))).astype(BF16))
    return res


def _retention_consts(tb, chunk):
    assert chunk <= tb and tb % chunk == 0
    lg = np.log(1.0 - 2.0 ** (-5.0 - np.arange(RET_H, dtype=np.float64)))
    idx = np.arange(tb, dtype=np.float64)
    visible = (idx[None, :] // chunk) <= (idx[:, None] // chunk)
    dm = np.exp(lg[:, None, None] * np.abs(idx[:, None] - idx[None, :])) * visible[None]
    wq = np.broadcast_to(np.exp(lg[:, None] * (idx[None, :] + 1.0))[:, :, None], (RET_H, tb, RET_D))
    wk = np.broadcast_to(np.exp(lg[:, None] * (tb - 1.0 - idx[None, :]))[:, :, None], (RET_H, tb, RET_D))
    dec = np.broadcast_to(np.exp(lg * tb)[:, None, None], (RET_H, 1, RET_D))
    return [jnp.asarray(a, F32) for a in (dm, wq, wk, dec)]


def _inproj_body(*refs, head_major, per_batch):
    x_ref, g_ref, wr_ref, wf_ref, wl_ref, bl_ref, cos_ref, sin_ref, hsum_ref = refs[0:9]
    if head_major:
        dm_ref, wq_ref, wk_ref, dec_ref, gn_ref, s0_ref = refs[9:15]
        ro_ref, s_ref = refs[15:17]
    else:
        rq_ref, rk_ref, rv_ref, rg_ref = refs[9:13]
    fq_ref, fkb_ref, fvp_ref, fk_ref, fv_ref, lf_ref, qn_ref, kn_ref = refs[-8:]

    def put_kv(dst_ref, z):
        if head_major:
            dst_ref[0] = z.T.reshape(FOX_H, FOX_D, z.shape[0])
        else:
            dst_ref[...] = z

    h = _rms(x_ref[...], g_ref[...]).astype(BF16)
    cos = cos_ref[...]
    sin = sin_ref[...]

    sls = [slice(hh * RET_D, (hh + 1) * RET_D) for hh in range(RET_H)]

    def rotary(z, scale):
        res = []
        for sl in sls:
            zz = z[:, sl]
            r = zz * cos + pltpu.roll(zz, RET_D // 2, 1) * sin
            res.append((r if scale is None else r * scale).astype(BF16))
        return res

    rqs = rotary(_dot(h, wr_ref[:, 0:RET_W]), None)
    rks = rotary(_dot(h, wr_ref[:, RET_W:2 * RET_W]), RET_D ** -0.5)
    zv = _dot(h, wr_ref[:, 2 * RET_W:3 * RET_W])
    zg = _dot(h, wr_ref[:, 3 * RET_W:4 * RET_W])
    if head_major:
        @pl.when(pl.program_id(0) % per_batch == 0)
        def _():
            s_ref[...] = s0_ref[...]

        tb = dm_ref.shape[1]
        for r0 in range(0, h.shape[0], tb):
            rows = slice(r0, r0 + tb)
            outs = _retention_block([a[rows] for a in rqs], [a[rows] for a in rks],
                                    [zv[rows, sl].astype(BF16) for sl in sls], [zg[rows, sl] for sl in sls],
                                    dm_ref, wq_ref, wk_ref, dec_ref, gn_ref, s_ref)
            for sl, o in zip(sls, outs):
                ro_ref[rows, sl] = o
    else:
        for sl, a, b in zip(sls, rqs, rks):
            rq_ref[:, sl] = a
            rk_ref[:, sl] = b
        rv_ref[...] = zv.astype(BF16)
        rg_ref[...] = zg
    def max_sq_norm(zb):
        return jnp.max(_dot(zb * zb, hsum_ref[...]), axis=0, keepdims=True)

    fqb = (_dot(h, wf_ref[:, 0:FOX_W]) * (LOG2E * FOX_D ** -0.5)).astype(BF16)
    fq_ref[...] = fqb
    qn_ref[0] = max_sq_norm(fqb)
    fk = _dot(h, wf_ref[:, FOX_W:2 * FOX_W])
    put_kv(fk_ref, fk)
    fkb = fk.astype(BF16)
    fkb_ref[...] = fkb
    kn_ref[0] = max_sq_norm(fkb)
    fv = _dot(h, wf_ref[:, 2 * FOX_W:3 * FOX_W])
    put_kv(fv_ref, fv)
    lane = lax.broadcasted_iota(jnp.int32, (fv.shape[0], LANES), 1)
    for pr in range(FOX_H // 2):
        vp = fv[:, pr * LANES:(pr + 1) * LANES]
        fvp_ref[:, (2 * pr) * LANES:(2 * pr + 1) * LANES] = jnp.where(lane < FOX_D, vp, 1.0).astype(BF16)
        fvp_ref[:, (2 * pr + 1) * LANES:(2 * pr + 2) * LANES] = jnp.where(lane >= FOX_D, vp, 1.0).astype(BF16)
    zl = _dot(h, wl_ref[...]) + bl_ref[...]
    lf = jnp.minimum(zl, 0.0) - jnp.log1p(jnp.exp(-jnp.abs(zl)))
    if head_major:
        lf_ref[0] = lf.T[0:FOX_H, :]
    else:
        lf_ref[...] = lf[:, 0:FOX_H]


def _inproj(x, g, wr, wf, wl, bl, cos_t, sin_t, tm, seq, gn, s0, chunk):
    n = x.shape[0]
    nt = cos_t.shape[0] // tm
    head_major = tm <= seq
    row = lambda w: pl.BlockSpec((tm, w), lambda i: (i, 0))
    tab = pl.BlockSpec((tm, LANES), lambda i: (i % nt, 0))
    shp = lambda w, dt: jax.ShapeDtypeStruct((n, w), dt)
    hsum = jnp.asarray(np.arange(FOX_W)[:, None] // FOX_D == np.arange(LANES)[None, :], BF16)
    stat_spec = pl.BlockSpec((1, 1, LANES), lambda i: (i, 0, 0))
    stat_shape = jax.ShapeDtypeStruct((n // tm, 1, LANES), F32)
    if head_major:
        per_batch = seq // tm
        kv_spec = pl.BlockSpec((1, FOX_H, FOX_D, tm), lambda i: (i // per_batch, 0, 0, i % per_batch))
        kv_shape = jax.ShapeDtypeStruct((n // seq, FOX_H, FOX_D, seq), F32)
        lf_spec = pl.BlockSpec((1, FOX_H, tm), lambda i: (i // per_batch, 0, i % per_batch))
        lf_shape = jax.ShapeDtypeStruct((n // seq, FOX_H, seq), F32)
        consts = _retention_consts(min(RET_BLOCK, tm), chunk)
        st_spec = pl.BlockSpec((1, RET_H, RET_D, RET_D), lambda i: (i // per_batch, 0, 0, 0))
        extra_in = consts + [gn, s0]
        extra_in_specs = [_const_spec(c.shape) for c in consts] + [_const_spec(gn.shape), st_spec]
        ret_specs = [row(RET_W), st_spec]
        ret_shapes = [shp(RET_W, BF16), jax.ShapeDtypeStruct(s0.shape, F32)]
    else:
        per_batch = 1
        kv_spec, kv_shape = row(FOX_W), shp(FOX_W, F32)
        lf_spec, lf_shape = row(FOX_H), shp(FOX_H, F32)
        extra_in, extra_in_specs = [], []
        ret_specs = [row(RET_W)] * 4
        ret_shapes = [shp(RET_W, BF16), shp(RET_W, BF16), shp(RET_W, BF16), shp(RET_W, F32)]
    return pl.pallas_call(
        functools.partial(_inproj_body, head_major=head_major, per_batch=per_batch),
        grid=(n // tm,),
        in_specs=[row(D_MODEL), _const_spec(g.shape), _const_spec(wr.shape), _const_spec(wf.shape),
                  _const_spec(wl.shape), _const_spec(bl.shape), tab, tab, _const_spec(hsum.shape)] + extra_in_specs,
        out_specs=ret_specs + [row(FOX_W), row(FOX_W), row(FOX_H * LANES), kv_spec, kv_spec, lf_spec,
                               stat_spec, stat_spec],
        out_shape=ret_shapes + [shp(FOX_W, BF16), shp(FOX_W, BF16), shp(FOX_H * LANES, BF16), kv_shape,
                                kv_shape, lf_shape, stat_shape, stat_shape],
        compiler_params=_params(("arbitrary",)),
        name="inproj_ab_retention" if head_major else "inproj_ab",
    )(x, g, wr, wf, wl, bl, cos_t, sin_t, hsum, *extra_in)


def _cumsum_body(x_ref, u_ref, o_ref):
    rows, total = x_ref.shape
    u = u_ref[...]

    def chunk(i, carry):
        off = pl.multiple_of(i * CUMSUM_CHUNK, CUMSUM_CHUNK)
        xc = x_ref[:, pl.ds(off, CUMSUM_CHUNK)]
        hi = xc.astype(BF16)
        r1 = xc - hi.astype(F32)
        mid = r1.astype(BF16)
        lo = (r1 - mid.astype(F32)).astype(BF16)
        y = _dot(hi, u) + _dot(mid, u) + _dot(lo, u) + carry
        o_ref[:, pl.ds(off, CUMSUM_CHUNK)] = y * LOG2E
        return y[:, CUMSUM_CHUNK - 1:CUMSUM_CHUNK]

    lax.fori_loop(0, total // CUMSUM_CHUNK, chunk, jnp.zeros((rows, 1), F32))


def _cumsum_lanes(x):
    idx = np.arange(CUMSUM_CHUNK)
    u = jnp.asarray(idx[:, None] <= idx[None, :], BF16)
    return pl.pallas_call(
        _cumsum_body,
        out_shape=jax.ShapeDtypeStruct(x.shape, F32),
        compiler_params=pltpu.CompilerParams(vmem_limit_bytes=VMEM_LIMIT),
        name="logf_cumsum",
    )(x, u)


def _ret_body(q_ref, k_ref, v_ref, rg_ref, dm_ref, wq_ref, wk_ref, dec_ref, gn_ref, s0_ref,
              ro_ref, s_ref):
    @pl.when(pl.program_id(1) == 0)
    def _():
        s_ref[...] = s0_ref[...]

    sls = [slice(hh * RET_D, (hh + 1) * RET_D) for hh in range(RET_H)]
    outs = _retention_block([q_ref[:, sl] for sl in sls], [k_ref[:, sl] for sl in sls],
                            [v_ref[:, sl] for sl in sls], [rg_ref[:, sl] for sl in sls],
                            dm_ref, wq_ref, wk_ref, dec_ref, gn_ref, s_ref)
    for sl, o in zip(sls, outs):
        ro_ref[:, sl] = o


def _retention(rq, rk, rv, rg, gn, s0, batch, seq, chunk):
    tb = min(RET_BLOCK, seq)
    nb = seq // tb
    consts = _retention_consts(tb, chunk)
    row = pl.BlockSpec((tb, RET_W), lambda b, t: (b * nb + t, 0))
    st = pl.BlockSpec((1, RET_H, RET_D, RET_D), lambda b, t: (b, 0, 0, 0))
    return pl.pallas_call(
        _ret_body,
        grid=(batch, nb),
        in_specs=[row, row, row, row] + [_const_spec(c.shape) for c in consts] + [_const_spec(gn.shape), st],
        out_specs=[row, st],
        out_shape=[jax.ShapeDtypeStruct((batch * seq, RET_W), BF16),
                   jax.ShapeDtypeStruct((batch, RET_H, RET_D, RET_D), F32)],
        compiler_params=_params(("parallel", "arbitrary")),
        name="retention",
    )(rq, rk, rv, rg, *consts, gn, s0)


def _fox_body(start_ref, q_ref, k_ref, v_ref, c_ref, mask_ref, o_ref, qs_scr, ql_scr, m_scr, acc_scr, *, tq, tk):
    n_full = pl.program_id(2)
    pair = pl.program_id(0) * pl.num_programs(1) + pl.program_id(1)
    first_a = start_ref[(2 * pair) * pl.num_programs(2) + n_full]
    first_b = start_ref[(2 * pair + 1) * pl.num_programs(2) + n_full]
    solo_first = jnp.minimum(first_a, first_b)
    first = jnp.maximum(first_a, first_b)
    solo_head = (first_b < first_a).astype(jnp.int32)
    q = q_ref[...]
    lane = lax.broadcasted_iota(jnp.int32, q.shape, 1)
    zero = jnp.zeros_like(q)
    qs_scr[0:tq, :] = jnp.where(lane < FOX_D, q, zero)
    qs_scr[tq:2 * tq, :] = jnp.where(lane >= FOX_D, q, zero)
    ql_scr[0:tq // 2, :] = qs_scr[tq // 2:tq, :]
    ql_scr[tq // 2:tq, :] = qs_scr[tq + tq // 2:2 * tq, :]
    m_scr[...] = jnp.full(m_scr.shape, -jnp.inf, F32)
    acc_scr[...] = jnp.zeros(acc_scr.shape, F32)

    def stacked_tile(q_scr, q_rows, state_rows, off, width, mask):
        s = _dot_nt(q_scr[...], k_ref[0, pl.ds(off, width), :])
        shs = [s[q_rows[hh], :] - c_ref[0, 0, hh:hh + 1, pl.ds(off, width)] for hh in range(2)]
        if mask is not None:
            shs = [sh + mask for sh in shs]
        chunks = [[sh[:, c * LANES:(c + 1) * LANES] for c in range(width // LANES)] for sh in shs]
        m_prevs = [m_scr[rows, :] for rows in state_rows]
        m_news = [jnp.maximum(m_prevs[hh], jnp.max(functools.reduce(jnp.maximum, chunks[hh]), axis=1, keepdims=True))
                  for hh in range(2)]
        ps = [jnp.concatenate([jnp.exp2(ch - m_news[hh]) for ch in chunks[hh]], axis=1).astype(BF16)
              for hh in range(2)]
        pvs = [_dot(ps[hh], v_ref[0, pl.ds(off, width), hh * LANES:(hh + 1) * LANES]) for hh in range(2)]
        for hh in range(2):
            rows = state_rows[hh]
            acc_scr[rows, :] = jnp.exp2(m_prevs[hh] - m_news[hh]) * acc_scr[rows, :] + pvs[hh]
            m_scr[rows, :] = m_news[hh]

    head_rows = [slice(hh * tq, (hh + 1) * tq) for hh in range(2)]

    def tile(kj):
        stacked_tile(qs_scr, head_rows, head_rows, pl.multiple_of(kj * tk, tk), tk, None)

    def diagonal_tile():
        hq = tq // 2
        off = pl.multiple_of(n_full * tk, tk)
        stacked_tile(qs_scr, head_rows, head_rows, off, hq, mask_ref[...])
        lower = [slice(hh * tq + hq, (hh + 1) * tq) for hh in range(2)]
        stacked_tile(ql_scr, [slice(hh * hq, (hh + 1) * hq) for hh in range(2)], lower,
                     pl.multiple_of(off + hq, hq), hq, mask_ref[0:hq, :])

    def solo_tile(kj):
        off = pl.multiple_of(kj * tk, tk)
        rows = pl.ds(pl.multiple_of(solo_head * tq, tq), tq)
        sh = (_dot_nt(qs_scr[rows, :], k_ref[0, pl.ds(off, tk), :])
              - c_ref[0, 0, pl.ds(solo_head, 1), pl.ds(off, tk)])
        chunks = [sh[:, c * LANES:(c + 1) * LANES] for c in range(tk // LANES)]
        m_prev = m_scr[rows, :]
        m_new = jnp.maximum(m_prev, jnp.max(functools.reduce(jnp.maximum, chunks), axis=1, keepdims=True))
        p = jnp.concatenate([jnp.exp2(ch - m_new) for ch in chunks], axis=1).astype(BF16)
        pv = _dot(p, v_ref[0, pl.ds(off, tk), :])
        pv = jnp.where(solo_head == 0, pv[:, 0:LANES], pv[:, LANES:2 * LANES])
        acc_scr[rows, :] = jnp.exp2(m_prev - m_new) * acc_scr[rows, :] + pv
        m_scr[rows, :] = m_new

    def run_tiles(start, count, visit):
        def group(i, carry):
            for j in range(FOX_UNROLL):
                visit(start + FOX_UNROLL * i + j)
            return carry

        lax.fori_loop(0, count // FOX_UNROLL, group, 0)
        span = FOX_UNROLL // 2
        while span >= 1:
            @pl.when((count // span) % 2 == 1)
            def _(span=span):
                base = start + (count // (2 * span)) * (2 * span)
                for j in range(span):
                    visit(base + j)
            span //= 2

    run_tiles(solo_first, first - solo_first, solo_tile)
    run_tiles(first, n_full - first, tile)
    diagonal_tile()

    acc_a = acc_scr[0:tq, :]
    acc_b = acc_scr[tq:2 * tq, :]
    half = LANES // 2
    o_ref[...] = jnp.where(lane < FOX_D, acc_a / pltpu.roll(acc_a, half, 1),
                           acc_b / pltpu.roll(acc_b, half, 1)).astype(BF16)


def _fox_first_tiles(qn2, kn2, c_all, batch, seq, tile_len):
    nq = seq // tile_len
    qn = jnp.sqrt(qn2[:, 0, 0:FOX_H]).reshape(batch, nq, FOX_H).transpose(0, 2, 1)
    kn = jnp.sqrt(kn2[:, 0, 0:FOX_H]).reshape(batch, nq, FOX_H).transpose(0, 2, 1)
    c = c_all[..., 0:seq].reshape(batch, FOX_H, nq, tile_len)
    c_first, c_last = c[..., 0], c[..., tile_len - 1]
    dots = FOX_NORM_SLACK * (qn[..., :, None] * kn[..., None, :] + (qn * kn)[..., :, None])
    gap = dots + c_first[..., :, None] - c_last[..., None, :]
    before = jnp.arange(nq)[None, :] < jnp.arange(nq)[:, None]
    droppable = (gap < -FOX_DROP_LOG2) & before
    first = jnp.sum(jnp.cumprod(droppable.astype(jnp.int32), axis=-1), axis=-1)
    return first.reshape(-1).astype(jnp.int32)


def _fox(first_tile, fq, k_all, v_all, c_all, batch, seq, tile_len):
    tq = tk = tile_len
    nq = seq // tq
    half = tile_len // 2
    mask = jnp.asarray(np.where(np.arange(half)[None, :] <= np.arange(tile_len)[:, None], 0.0, NEG_INF), F32)
    body = functools.partial(_fox_body, tq=tq, tk=tk)
    grid_spec = pltpu.PrefetchScalarGridSpec(
        num_scalar_prefetch=1,
        grid=(batch, FOX_H // 2, nq),
        in_specs=[pl.BlockSpec((tq, LANES), lambda b, h, i, _: (b * nq + i, h)),
                  pl.BlockSpec((1, seq, LANES), lambda b, h, i, _: (b, 0, h)),
                  pl.BlockSpec((1, seq, 2 * LANES), lambda b, h, i, _: (b, 0, h)),
                  pl.BlockSpec((1, 1, 2, seq), lambda b, h, i, _: (b, h, 0, 0)),
                  _const_spec(mask.shape)],
        out_specs=pl.BlockSpec((tq, LANES), lambda b, h, i, _: (b * nq + i, h)),
        scratch_shapes=[pltpu.VMEM((2 * tq, LANES), BF16), pltpu.VMEM((tq, LANES), BF16),
                        pltpu.VMEM((2 * tq, LANES), F32), pltpu.VMEM((2 * tq, LANES), F32)])
    return pl.pallas_call(
        body,
        grid_spec=grid_spec,
        out_shape=jax.ShapeDtypeStruct((batch * seq, FOX_W), BF16),
        compiler_params=_params(("parallel", "parallel", "arbitrary")),
        name="fox_attention",
    )(first_tile, fq, k_all, v_all, c_all, mask)


def _fox_cached_body(q_ref, kp_ref, vp_ref, kn_ref, vn_ref, c_ref, o_ref, *, past, lq):
    q = q_ref[...]
    lane = lax.broadcasted_iota(jnp.int32, q.shape, 1)
    zero = jnp.zeros_like(q)
    qs = jnp.concatenate([jnp.where(lane < FOX_D, q, zero), jnp.where(lane >= FOX_D, q, zero)], axis=0)
    head_rows = lambda a, b, width: jnp.concatenate(
        [jnp.broadcast_to(a, (lq, width)), jnp.broadcast_to(b, (lq, width))], axis=0)
    s_past = _dot(qs, kp_ref[0].astype(BF16)) - head_rows(c_ref[0, 0, 0:1, 0:past], c_ref[0, 0, 1:2, 0:past], past)
    s_new = _dot_nt(qs, kn_ref[...]) - head_rows(c_ref[0, 0, 0:1, past:past + lq],
                                                 c_ref[0, 0, 1:2, past:past + lq], lq)
    qi = lax.broadcasted_iota(jnp.int32, s_new.shape, 0)
    ki = lax.broadcasted_iota(jnp.int32, s_new.shape, 1)
    s_new = jnp.where(ki <= jnp.where(qi >= lq, qi - lq, qi), s_new, NEG_INF)
    m = jnp.maximum(jnp.max(s_past, axis=1, keepdims=True), jnp.max(s_new, axis=1, keepdims=True))
    p_past = jnp.exp2(s_past - m)
    p_new = jnp.exp2(s_new - m)
    denom = jnp.sum(p_past, axis=1, keepdims=True) + jnp.sum(p_new, axis=1, keepdims=True)
    pb_new = p_new.astype(BF16)
    acc = _dot_nt(p_past.astype(BF16), vp_ref[0].astype(BF16))
    acc_a = (acc[0:lq] + _dot(pb_new[0:lq], vn_ref[:, 0:LANES])) / denom[0:lq]
    acc_b = (acc[lq:2 * lq] + _dot(pb_new[lq:2 * lq], vn_ref[:, LANES:2 * LANES])) / denom[lq:2 * lq]
    o_ref[...] = jnp.where(lane < FOX_D, acc_a, acc_b).astype(BF16)


def _fox_cached(fq, k_past, v_past, k_new, v_new, c_all, batch, lq):
    past = k_past.shape[2]
    new_rows = lambda w: pl.BlockSpec((lq, w), lambda b, h: (b, h))
    cache = pl.BlockSpec((1, LANES, past), lambda b, h: (b, h, 0))
    return pl.pallas_call(
        functools.partial(_fox_cached_body, past=past, lq=lq),
        grid=(batch, FOX_H // 2),
        in_specs=[new_rows(LANES), cache, cache, new_rows(LANES), new_rows(2 * LANES),
                  pl.BlockSpec((1, 1, 2, c_all.shape[-1]), lambda b, h: (b, h, 0, 0))],
        out_specs=new_rows(LANES),
        out_shape=jax.ShapeDtypeStruct((batch * lq, FOX_W), BF16),
        compiler_params=_params(("parallel", "parallel")),
        name="fox_attention_cached",
    )(fq, k_past, v_past, k_new, v_new, c_all)


def _normproj_body(x_ref, g_ref, w_ref, o_ref, ob_ref):
    y = _dot(_rms(x_ref[...], g_ref[...]).astype(BF16), w_ref[...])
    o_ref[...] = y
    ob_ref[...] = y.astype(BF16)


def _normproj(x, g, w, tm):
    n, wout = x.shape[0], w.shape[1]
    return pl.pallas_call(
        _normproj_body,
        grid=(n // tm,),
        in_specs=[pl.BlockSpec((tm, D_MODEL), lambda i: (i, 0)), _const_spec(g.shape), _const_spec(w.shape)],
        out_specs=[pl.BlockSpec((tm, wout), lambda i: (i, 0))] * 2,
        out_shape=[jax.ShapeDtypeStruct((n, wout), F32), jax.ShapeDtypeStruct((n, wout), BF16)],
        compiler_params=_params(("parallel",)),
        name="memory_kv",
    )(x, g, w)


def _post_body(*refs, has_mix, parts, seq_rows, mem_len):
    if has_mix:
        (x_ref, ro_ref, fo_ref, wo_ref, nw_ref, wxq_ref, wxo_ref, mk_ref, mv_ref, wup_ref, wdn_ref,
         o_ref) = refs
    else:
        x_ref, nw_ref, wxq_ref, wxo_ref, mk_ref, mv_ref, wup_ref, wdn_ref, o_ref = refs
    rows = [slice(r0, r0 + nrows) for r0, nrows in parts]
    xs = [x_ref[rs, :] for rs in rows]
    if has_mix:
        mixes = [_dot(ro_ref[rs, :], wo_ref[0:RET_W, :]) + _dot(fo_ref[rs, :], wo_ref[RET_W:RET_W + FOX_W, :])
                 for rs in rows]
        xs = [x + _rms(mix, nw_ref[1:2, :]) for x, mix in zip(xs, mixes)]

    qs = [_dot(_rms(x, nw_ref[2:3, :]).astype(BF16), wxq_ref[...]) for x in xs]
    ctx = []
    for (r0, nrows), q in zip(parts, qs):
        groups = []
        for g0 in range(0, nrows, min(seq_rows, nrows)):
            gs = slice(g0, g0 + min(seq_rows, nrows))
            m0 = ((r0 + g0) // seq_rows) * mem_len
            heads = []
            for hh in range(X_H):
                sl = slice(hh * X_D, (hh + 1) * X_D)
                s = _dot_nt(q[gs, sl].astype(BF16), mk_ref[m0:m0 + mem_len, sl]) * (X_D ** -0.5)
                p = jnp.exp(s - jnp.max(s, axis=-1, keepdims=True))
                o = _dot(p.astype(BF16), mv_ref[m0:m0 + mem_len, sl]) / jnp.sum(p, axis=-1, keepdims=True)
                heads.append(o.astype(BF16))
            groups.append(jnp.concatenate(heads, axis=1))
        ctx.append(groups[0] if len(groups) == 1 else jnp.concatenate(groups, axis=0))
    atts = [_dot(c, wxo_ref[...]) for c in ctx]
    xs = [x + _rms(att, nw_ref[3:4, :]) for x, att in zip(xs, atts)]

    hs = [_rms(x, nw_ref[4:5, :]).astype(BF16) for x in xs]
    accs = [None] * len(parts)
    for c in range(D_FF // D_MODEL):
        sl = slice(c * D_MODEL, (c + 1) * D_MODEL)
        for i, h in enumerate(hs):
            r = jnp.maximum(_dot(h, wup_ref[:, sl]), 0.0)
            part = _dot((r * r).astype(BF16), wdn_ref[sl, :])
            accs[i] = part if accs[i] is None else accs[i] + part
    for rs, x, acc in zip(rows, xs, accs):
        o_ref[rs, :] = x + _rms(acc, nw_ref[5:6, :])


def _post(x, mix_in, nw, wxq, wxo, mk, mv, wup, wdn, seq, tm):
    n = x.shape[0]
    mem_len = mk.shape[0] // (n // seq)
    row = lambda w: pl.BlockSpec((tm, w), lambda i: (i, 0))
    if tm <= seq:
        per_batch = seq // tm
        mem = pl.BlockSpec((mem_len, D_MODEL), lambda i: (i // per_batch, 0))
        seq_rows = tm
        nparts = POST_PARTS if tm % (POST_PARTS * 16) == 0 else 1
        parts = tuple((p * (tm // nparts), tm // nparts) for p in range(nparts))
    else:
        mem = pl.BlockSpec(((tm // seq) * mem_len, D_MODEL), lambda i: (i, 0))
        seq_rows = seq
        parts = ((0, tm),)
    args = [x]
    specs = [row(D_MODEL)]
    if mix_in is not None:
        ro, fo, wo = mix_in
        args += [ro, fo, wo]
        specs += [row(RET_W), row(FOX_W), _const_spec(wo.shape)]
    args += [nw, wxq, wxo, mk, mv, wup, wdn]
    specs += [_const_spec(nw.shape), _const_spec(wxq.shape), _const_spec(wxo.shape), mem, mem,
              _const_spec(wup.shape), _const_spec(wdn.shape)]
    return pl.pallas_call(
        functools.partial(_post_body, has_mix=mix_in is not None, parts=parts, seq_rows=seq_rows, mem_len=mem_len),
        grid=(n // tm,),
        in_specs=specs,
        out_specs=row(D_MODEL),
        out_shape=jax.ShapeDtypeStruct((n, D_MODEL), F32),
        compiler_params=_params(("parallel",)),
        name="xattn_mlp_mix" if mix_in is not None else "xattn_mlp",
    )(*args)


def _s5prep_body(lre_ref, lim_ref, ldt_ref, lre_rep_ref, lim_rep_ref, bre_ref, bim_ref,
                 are_ref, aim_ref, bbre_ref, bbim_ref):
    dt = jnp.exp(ldt_ref[...])

    def zoh(lre, lim):
        mag = jnp.exp(lre * dt)
        a_re = mag * jnp.cos(lim * dt)
        a_im = mag * jnp.sin(lim * dt)
        den = lre * lre + lim * lim
        num_re = a_re - 1.0
        f_re = (num_re * lre + a_im * lim) / den
        f_im = (a_im * lre - num_re * lim) / den
        return a_re, a_im, f_re, f_im

    a_re, a_im, _, _ = zoh(lre_ref[...], lim_ref[...])
    are_ref[...] = a_re
    aim_ref[...] = a_im
    _, _, f_re, f_im = zoh(lre_rep_ref[...], lim_rep_ref[...])
    b_re = bre_ref[...]
    b_im = bim_ref[...]
    bbre_ref[...] = f_re * b_re - f_im * b_im
    bbim_ref[...] = f_re * b_im + f_im * b_re


def _s5_discretize(lam_re, lam_im, log_dt, b_re, b_im):
    rep = lambda a: jnp.repeat(a, S5_GROUP, axis=1)
    wide = jax.ShapeDtypeStruct((S5_G, S5_P * S5_GROUP), F32)
    small = jax.ShapeDtypeStruct((S5_G, S5_P), F32)
    a_re, a_im, bb_re, bb_im = pl.pallas_call(
        _s5prep_body, out_shape=[small, small, wide, wide], name="s5_discretize",
    )(lam_re, lam_im, log_dt.reshape(S5_G, 1), rep(lam_re), rep(lam_im),
      b_re.reshape(S5_G, -1), b_im.reshape(S5_G, -1))
    a = jnp.stack([a_re.reshape(-1), a_im.reshape(-1)])
    gpc = S5_LANE_CHUNK // S5_P
    nch = S5_G // gpc
    eye = jnp.eye(gpc, dtype=F32)

    def in_blocks(bb):
        t = bb.reshape(nch, gpc, S5_P, S5_GROUP).transpose(0, 1, 3, 2)
        return (t[:, :, :, None, :] * eye[None, :, None, :, None]).reshape(nch, gpc * S5_GROUP, gpc * S5_P)

    bmat = jnp.concatenate([in_blocks(bb_re), in_blocks(bb_im)], axis=-1).astype(BF16)
    return a, bmat


def _s5_out_matrices(c_re, c_im):
    gpc = S5_LANE_CHUNK // S5_P
    nch = S5_G // gpc
    eye = jnp.eye(gpc, dtype=F32)

    def out_blocks(c):
        t = c.reshape(nch, gpc, S5_GROUP, S5_P).transpose(0, 1, 3, 2)
        return (t[:, :, :, None, :] * eye[None, :, None, :, None]).reshape(nch, gpc * S5_P, gpc * S5_GROUP)

    return jnp.concatenate([out_blocks(c_re), -out_blocks(c_im)], axis=1).astype(BF16)


def _s5_body(x_ref, nw_ref, perm_ref, win_ref, bmat_ref, a_ref, cmat_ref, d_ref, wglu_ref, st0_ref,
             o_ref, st_ref, u_scr, bu_scr, y_scr, *, ts):
    nb = x_ref.shape[0]
    rows = nb * ts
    lc = S5_LANE_CHUNK
    gw = (lc // S5_P) * S5_GROUP

    @pl.when(pl.program_id(0) == 0)
    def _():
        st_ref[...] = st0_ref[...]

    x = x_ref[...].reshape(rows, D_MODEL)
    h = _dot(perm_ref[0], _rms(x, nw_ref[0:1, :]).astype(BF16)).astype(BF16)
    u_scr[...] = _dot(h, win_ref[...])
    nv = lc // LANES
    for j in range(S5_STATE // lc):
        uj = u_scr[:, j * gw:(j + 1) * gw]
        bu = _dot(uj.astype(BF16), bmat_ref[j])
        for c in range(2 * nv):
            bu_scr[c] = bu[:, c * LANES:(c + 1) * LANES]
        a_re = [jnp.broadcast_to(a_ref[0:1, j * lc + c * LANES:j * lc + (c + 1) * LANES], (nb, LANES))
                for c in range(nv)]
        a_im = [jnp.broadcast_to(a_ref[1:2, j * lc + c * LANES:j * lc + (c + 1) * LANES], (nb, LANES))
                for c in range(nv)]

        def step(t, carry):
            idx = pl.ds(pl.multiple_of(t * nb, nb), nb)
            new = []
            for c in range(nv):
                s_re, s_im = carry[c], carry[nv + c]
                new.append(a_re[c] * s_re - a_im[c] * s_im + bu_scr[c, idx, :])
            for c in range(nv):
                s_re, s_im = carry[c], carry[nv + c]
                new.append(a_re[c] * s_im + a_im[c] * s_re + bu_scr[nv + c, idx, :])
            for c in range(2 * nv):
                bu_scr[c, idx, :] = new[c]
            return tuple(new)

        init = tuple(st_ref[:, j * lc + c * LANES:j * lc + (c + 1) * LANES] for c in range(nv)) + tuple(
            st_ref[:, S5_STATE + j * lc + c * LANES:S5_STATE + j * lc + (c + 1) * LANES] for c in range(nv))
        fin = lax.fori_loop(0, ts, step, init, unroll=True)
        for c in range(nv):
            st_ref[:, j * lc + c * LANES:j * lc + (c + 1) * LANES] = fin[c]
            st_ref[:, S5_STATE + j * lc + c * LANES:S5_STATE + j * lc + (c + 1) * LANES] = fin[nv + c]
        xs = jnp.concatenate([bu_scr[c] for c in range(2 * nv)], axis=1).astype(BF16)
        y_scr[:, j * gw:(j + 1) * gw] = _dot(xs, cmat_ref[j]) + d_ref[0:1, j * gw:(j + 1) * gw] * uj
    y = _dot(perm_ref[1], y_scr[...].astype(BF16)).astype(BF16)
    g = _dot(y, wglu_ref[...])
    out = g[:, 0:D_MODEL] * jax.nn.sigmoid(g[:, D_MODEL:2 * D_MODEL])
    o_ref[...] = (x + _rms(out, nw_ref[1:2, :])).reshape(nb, ts, D_MODEL)


def _s5_layer(x3, nw, win, bmat, a, cmat, d, wglu, st0, ts):
    nb, seq, _ = x3.shape
    rows = nb * ts
    blk = pl.BlockSpec((nb, ts, D_MODEL), lambda t: (0, t, 0))
    r = np.arange(rows)
    fwd = np.zeros((rows, rows), np.float32)
    fwd[(r % ts) * nb + r // ts, r] = 1.0
    perm = jnp.asarray(np.stack([fwd, fwd.T]), BF16)
    consts = [nw, perm, win, bmat, a, cmat, d, wglu, st0]
    return pl.pallas_call(
        functools.partial(_s5_body, ts=ts),
        grid=(seq // ts,),
        in_specs=[blk] + [_const_spec(c.shape) for c in consts],
        out_specs=[blk, pl.BlockSpec(st0.shape, lambda t: (0, 0))],
        out_shape=[jax.ShapeDtypeStruct(x3.shape, F32), jax.ShapeDtypeStruct(st0.shape, F32)],
        scratch_shapes=[pltpu.VMEM((rows, D_MODEL), F32),
                        pltpu.VMEM((2 * S5_LANE_CHUNK // LANES, rows, LANES), F32),
                        pltpu.VMEM((rows, D_MODEL), F32)],
        compiler_params=_params(("arbitrary",)),
        name="s5_layer",
    )(x3, *consts)


def _rotary_tables(pos, rows):
    half = RET_D // 2
    inv = ROPE_BASE ** (-jnp.arange(half, dtype=F32) / half)
    ang = pos.astype(F32)[:, None] * inv[None, :]
    cos = jnp.cos(ang)
    sin = jnp.sin(ang)
    cos_t = jnp.concatenate([cos, cos], axis=1)
    sin_t = jnp.concatenate([-sin, sin], axis=1)
    reps = max(1, rows // pos.shape[0])
    return jnp.tile(cos_t, (reps, 1)), jnp.tile(sin_t, (reps, 1))


def _prepare_weights(norm_w, w_in_ab, b_fox_f, w_out_ab, w_in_c, s5_lambda_re, s5_lambda_im, s5_log_dt,
                     s5_b_re, s5_b_im, s5_c_re, s5_c_im, s5_d, w_glu, w_xq, w_xo, w_up, w_down):
    w = {}
    wab = w_in_ab[0].astype(BF16)
    w["wr"] = wab[:, 0:4 * RET_W]
    w["wf"] = wab[:, 4 * RET_W:4 * RET_W + 3 * FOX_W]
    w["wl"] = jnp.pad(wab[:, 4 * RET_W + 3 * FOX_W:], ((0, 0), (0, LANES - FOX_H)))
    w["bl"] = jnp.pad(b_fox_f[0].astype(F32), (0, LANES - FOX_H)).reshape(1, LANES)
    w["wo"] = w_out_ab[0].astype(BF16)
    w["nw"] = norm_w.astype(F32)
    w["win_c"] = w_in_c[0].astype(BF16)
    w["a"], w["bmat"] = _s5_discretize(s5_lambda_re[0], s5_lambda_im[0], s5_log_dt[0], s5_b_re[0], s5_b_im[0])
    w["cmat"] = _s5_out_matrices(s5_c_re[0].astype(F32), s5_c_im[0].astype(F32))
    w["d"] = s5_d[0].astype(F32).reshape(1, D_MODEL)
    w["wglu"] = w_glu[0].astype(BF16)
    for name, arr in (("wxq", w_xq), ("wxo", w_xo), ("wup", w_up), ("wdn", w_down)):
        w[name] = arr.astype(BF16)
    return w


def _run_group(x3, pos, w, gn_w, mem_kv, ret_s0, fox_past, s5_st0):
    batch, seq, _ = x3.shape
    n = batch * seq
    tm = min(ROW_TILE, n)
    post_tile = lambda t: t if seq >= t else min(n, (t // seq) * seq)
    x = x3.reshape(n, D_MODEL)

    cos_t, sin_t = _rotary_tables(pos, tm)
    chunk = CHUNK if fox_past is None else seq
    gn = gn_w.reshape(RET_H, 1, RET_D)
    outs = _inproj(x, w["nw"][0, 0:1], w["wr"], w["wf"], w["wl"], w["bl"], cos_t, sin_t, tm, seq, gn, ret_s0, chunk)
    fq, fkb, fvp, fk, fv, lf, qn2, kn2 = outs[-8:]
    if tm <= seq:
        ro, ret_state = outs[0:2]
        lf_t = lf
        fk = jnp.transpose(fk, (0, 3, 1, 2))
        fv = jnp.transpose(fv, (0, 3, 1, 2))
        lf3 = jnp.transpose(lf, (0, 2, 1))
    else:
        lf3 = lf.reshape(batch, seq, FOX_H)
        lf_t = jnp.transpose(lf3, (0, 2, 1))
    if fox_past is not None:
        lf_t = jnp.concatenate([jnp.transpose(fox_past[2].astype(F32), (0, 2, 1)), lf_t], axis=2)
    lk = lf_t.shape[2]
    pad = -lk % CUMSUM_CHUNK
    lf_rows = jnp.pad(lf_t.reshape(batch * FOX_H, lk), ((0, 0), (0, pad)))
    c_all = _cumsum_lanes(lf_rows).reshape(batch, FOX_H // 2, 2, lk + pad)
    if fox_past is None:
        assert tm == FOX_TILE
        first_tile = _fox_first_tiles(qn2, kn2, c_all, batch, seq, FOX_TILE)
        fo = _fox(first_tile, fq, fkb.reshape(batch, seq, FOX_W), fvp.reshape(batch, seq, FOX_H * LANES), c_all,
                  batch, seq, FOX_TILE)
    else:
        past = fox_past[0].shape[1]
        cache_t = lambda a: jnp.transpose(a, (0, 2, 3, 1)).reshape(batch, FOX_W, past)
        fo = _fox_cached(fq, cache_t(fox_past[0]), cache_t(fox_past[1]), fkb, fvp, c_all, batch, seq)

    if tm > seq:
        ro, ret_state = _retention(*outs[0:4], gn, ret_s0, batch, seq, chunk)

    x = _post(x, (ro, fo, w["wo"]), w["nw"][0], w["wxq"][0], w["wxo"][0], mem_kv[0][0], mem_kv[0][1],
              w["wup"][0], w["wdn"][0], seq, post_tile(ROW_TILE))

    ts = min(S5_TS, seq)
    x3n, s5_state = _s5_layer(x.reshape(batch, seq, D_MODEL), w["nw"][1], w["win_c"], w["bmat"], w["a"],
                              w["cmat"], w["d"], w["wglu"], s5_st0, ts)
    x = _post(x3n.reshape(n, D_MODEL), None, w["nw"][1], w["wxq"][1], w["wxo"][1], mem_kv[1][0], mem_kv[1][1],
              w["wup"][1], w["wdn"][1], seq, post_tile(POST_TILE))

    new = dict(ret=ret_state[None], fox_k=fk.reshape(1, batch, seq, FOX_H, FOX_D),
               fox_v=fv.reshape(1, batch, seq, FOX_H, FOX_D), fox_logf=lf3[None],
               s5_re=s5_state[:, 0:S5_STATE].reshape(1, batch, S5_G, S5_P),
               s5_im=s5_state[:, S5_STATE:].reshape(1, batch, S5_G, S5_P))
    return x.reshape(batch, seq, D_MODEL), new


def kernel(x_prompt, x_sample, cache_ret_state, cache_fox_k, cache_fox_v, cache_fox_logf, state_s5_re, state_s5_im, cache_mem_k, cache_mem_v, mem_prompt, norm_w, w_in_ab, b_fox_f, ret_gn_w, w_out_ab, w_in_c, s5_lambda_re, s5_lambda_im, s5_log_dt, s5_b_re, s5_b_im, s5_c_re, s5_c_im, s5_d, w_glu, mem_norm_w, w_xq, w_xk, w_xv, w_xo, w_up, w_down):
    w = _prepare_weights(norm_w, w_in_ab, b_fox_f, w_out_ab, w_in_c, s5_lambda_re, s5_lambda_im, s5_log_dt,
                         s5_b_re, s5_b_im, s5_c_re, s5_c_im, s5_d, w_glu, w_xq, w_xo, w_up, w_down)
    batch, seq, _ = x_prompt.shape
    dec_batch, dec_seq, _ = x_sample.shape
    mem_len = mem_prompt.shape[1]
    depth = norm_w.shape[0]
    gn_w = ret_gn_w[0].astype(F32)

    mem_rows = mem_prompt.reshape(batch * mem_len, D_MODEL)
    mem_kv_prompt = []
    mem_k_out = []
    mem_v_out = []
    for layer in range(depth):
        wkv = jnp.concatenate([w_xk[layer], w_xv[layer]], axis=1).astype(BF16)
        kv, kvb = _normproj(mem_rows, mem_norm_w[layer].astype(F32).reshape(1, D_MODEL), wkv,
                            min(ROW_TILE, batch * mem_len))
        mem_kv_prompt.append((kvb[:, 0:D_MODEL], kvb[:, D_MODEL:]))
        mem_k_out.append(kv[:, 0:D_MODEL].reshape(batch, mem_len, X_H, X_D))
        mem_v_out.append(kv[:, D_MODEL:].reshape(batch, mem_len, X_H, X_D))

    pos_prompt = jnp.arange(seq)
    y_prompt, np_ = _run_group(
        x_prompt, pos_prompt, w, gn_w, mem_kv_prompt,
        jnp.zeros((batch, RET_H, RET_D, RET_D), F32), None, jnp.zeros((batch, 2 * S5_STATE), F32))

    past = cache_fox_k.shape[2]
    pos_sample = past + jnp.arange(dec_seq)
    mem_kv_sample = [(cache_mem_k[layer].reshape(dec_batch * mem_len, D_MODEL).astype(BF16),
                      cache_mem_v[layer].reshape(dec_batch * mem_len, D_MODEL).astype(BF16))
                     for layer in range(depth)]
    s5_st0 = jnp.concatenate([state_s5_re[0].reshape(dec_batch, S5_STATE),
                              state_s5_im[0].reshape(dec_batch, S5_STATE)], axis=1).astype(F32)
    y_sample, ns_ = _run_group(
        x_sample, pos_sample, w, gn_w, mem_kv_sample, cache_ret_state[0].astype(F32),
        (cache_fox_k[0], cache_fox_v[0], cache_fox_logf[0]), s5_st0)

    return (y_prompt, y_sample,
            np_["ret"], ns_["ret"],
            np_["fox_k"], np_["fox_v"], np_["fox_logf"],
            ns_["fox_k"], ns_["fox_v"], ns_["fox_logf"],
            np_["s5_re"], np_["s5_im"], ns_["s5_re"], ns_["s5_im"],
            jnp.stack(mem_k_out), jnp.stack(mem_v_out))
```

```python
import functools

import numpy as np
import jax
import jax.numpy as jnp
from jax import lax
from jax.experimental import pallas as pl
from jax.experimental.pallas import tpu as pltpu

F32 = jnp.float32
BF16 = jnp.bfloat16

D_MODEL = 1024
RET_H = 4
RET_D = 128
RET_W = RET_H * RET_D
CHUNK = 64
ROPE_BASE = 10000.0
FOX_H = 8
FOX_D = 64
FOX_W = FOX_H * FOX_D
S5_GROUP = 16
S5_G = D_MODEL // S5_GROUP
S5_P = 64
S5_STATE = S5_G * S5_P
X_H = 4
X_D = D_MODEL // X_H
D_FF = 4 * D_MODEL
EPS = 1e-6
NEG_INF = -1e30
LOG2E = float(np.log2(np.e))

LANES = 128
VMEM_LIMIT = 56 * 1024 * 1024

ROW_TILE = 512
POST_TILE = 1024
POST_PARTS = 2
RET_BLOCK = 256
FOX_TILE = 512
FOX_UNROLL = 4
FOX_DROP_LOG2 = 152.0
FOX_NORM_SLACK = 1.02
S5_TS = 64
S5_LANE_CHUNK = 512
CUMSUM_CHUNK = 256


def _dot(a, b):
    return jnp.dot(a, b, preferred_element_type=F32)


def _dot_nt(a, b):
    return lax.dot_general(a, b, (((1,), (1,)), ((), ())), preferred_element_type=F32)


def _dot_tn(a, b):
    return lax.dot_general(a, b, (((0,), (0,)), ((), ())), preferred_element_type=F32)


def _rms(x, w):
    return x * lax.rsqrt(jnp.mean(x * x, axis=-1, keepdims=True) + EPS) * w


def _const_spec(shape):
    nd = len(shape)
    return pl.BlockSpec(shape, lambda *_: (0,) * nd, pipeline_mode=pl.Buffered(1))


def _params(sem):
    return pltpu.CompilerParams(dimension_semantics=sem, vmem_limit_bytes=VMEM_LIMIT)


def _retention_block(qs, ks, vs, gs, dm_ref, wq_ref, wk_ref, dec_ref, gn_ref, s_ref):
    heads = range(RET_H)
    states = [s_ref[0, hh] for hh in heads]
    scores = [(_dot_nt(qs[hh], ks[hh]) * dm_ref[hh]).astype(BF16) for hh in heads]
    inter = [_dot(qs[hh], states[hh].astype(BF16)) * wq_ref[hh] for hh in heads]
    outs = [_dot(scores[hh], vs[hh]) + inter[hh] for hh in heads]
    kws = [(ks[hh].astype(F32) * wk_ref[hh]).astype(BF16) for hh in heads]
    for hh in heads:
        s_ref[0, hh] = states[hh] * dec_ref[hh] + _dot_tn(kws[hh], vs[hh])
    res = []
    for hh in heads:
        o = outs[hh]
        mu = jnp.mean(o, axis=-1, keepdims=True)
        oc = o - mu
        var = jnp.mean(oc * oc, axis=-1, keepdims=True)
        g = gs[hh]
        res.append((oc * lax.rsqrt(var + EPS) * gn_ref[hh] * (g * jax.nn.sigmoid(g))).astype(BF16))
    return res


def _retention_consts(tb, chunk):
    assert chunk <= tb and tb % chunk == 0
    lg = np.log(1.0 - 2.0 ** (-5.0 - np.arange(RET_H, dtype=np.float64)))
    idx = np.arange(tb, dtype=np.float64)
    visible = (idx[None, :] // chunk) <= (idx[:, None] // chunk)
    dm = np.exp(lg[:, None, None] * np.abs(idx[:, None] - idx[None, :])) * visible[None]
    wq = np.broadcast_to(np.exp(lg[:, None] * (idx[None, :] + 1.0))[:, :, None], (RET_H, tb, RET_D))
    wk = np.broadcast_to(np.exp(lg[:, None] * (tb - 1.0 - idx[None, :]))[:, :, None], (RET_H, tb, RET_D))
    dec = np.broadcast_to(np.exp(lg * tb)[:, None, None], (RET_H, 1, RET_D))
    return [jnp.asarray(a, F32) for a in (dm, wq, wk, dec)]


def _inproj_body(*refs, head_major, per_batch):
    x_ref, g_ref, wr_ref, wf_ref, wl_ref, bl_ref, cos_ref, sin_ref, hsum_ref = refs[0:9]
    if head_major:
        dm_ref, wq_ref, wk_ref, dec_ref, gn_ref, s0_ref = refs[9:15]
        ro_ref, s_ref = refs[15:17]
    else:
        rq_ref, rk_ref, rv_ref, rg_ref = refs[9:13]
    fq_ref, fkb_ref, fvp_ref, fk_ref, fv_ref, lf_ref, qn_ref, kn_ref = refs[-8:]

    def put_kv(dst_ref, z):
        if head_major:
            dst_ref[0] = z.T.reshape(FOX_H, FOX_D, z.shape[0])
        else:
            dst_ref[...] = z

    h = _rms(x_ref[...], g_ref[...]).astype(BF16)
    cos = cos_ref[...]
    sin = sin_ref[...]

    sls = [slice(hh * RET_D, (hh + 1) * RET_D) for hh in range(RET_H)]

    def rotary(z, scale):
        res = []
        for sl in sls:
            zz = z[:, sl]
            r = zz * cos + pltpu.roll(zz, RET_D // 2, 1) * sin
            res.append((r if scale is None else r * scale).astype(BF16))
        return res

    rqs = rotary(_dot(h, wr_ref[:, 0:RET_W]), None)
    rks = rotary(_dot(h, wr_ref[:, RET_W:2 * RET_W]), RET_D ** -0.5)
    zv = _dot(h, wr_ref[:, 2 * RET_W:3 * RET_W])
    zg = _dot(h, wr_ref[:, 3 * RET_W:4 * RET_W])
    def max_sq_norm(zb):
        return jnp.max(_dot(zb * zb, hsum_ref[...]), axis=0, keepdims=True)

    def project_q():
        fqb = (_dot(h, wf_ref[:, 0:FOX_W]) * (LOG2E * FOX_D ** -0.5)).astype(BF16)
        fq_ref[...] = fqb
        qn_ref[0] = max_sq_norm(fqb)

    def project_k():
        fk = _dot(h, wf_ref[:, FOX_W:2 * FOX_W])
        put_kv(fk_ref, fk)
        fkb = fk.astype(BF16)
        fkb_ref[...] = fkb
        kn_ref[0] = max_sq_norm(fkb)

    def project_v():
        fv = _dot(h, wf_ref[:, 2 * FOX_W:3 * FOX_W])
        put_kv(fv_ref, fv)
        lane = lax.broadcasted_iota(jnp.int32, (fv.shape[0], LANES), 1)
        for pr in range(FOX_H // 2):
            vp = fv[:, pr * LANES:(pr + 1) * LANES]
            fvp_ref[:, (2 * pr) * LANES:(2 * pr + 1) * LANES] = jnp.where(lane < FOX_D, vp, 1.0).astype(BF16)
            fvp_ref[:, (2 * pr + 1) * LANES:(2 * pr + 2) * LANES] = jnp.where(lane >= FOX_D, vp, 1.0).astype(BF16)

    def project_logf():
        zl = _dot(h, wl_ref[...]) + bl_ref[...]
        lf = jnp.minimum(zl, 0.0) - jnp.log1p(jnp.exp(-jnp.abs(zl)))
        if head_major:
            lf_ref[0] = lf.T[0:FOX_H, :]
        else:
            lf_ref[...] = lf[:, 0:FOX_H]

    projections = [project_q, project_k, project_v, project_logf]
    if head_major:
        @pl.when(pl.program_id(0) % per_batch == 0)
        def _():
            s_ref[...] = s0_ref[...]

        def retention_rows(r0):
            rows = slice(r0, r0 + tb)
            outs = _retention_block([a[rows] for a in rqs], [a[rows] for a in rks],
                                    [zv[rows, sl].astype(BF16) for sl in sls], [zg[rows, sl] for sl in sls],
                                    dm_ref, wq_ref, wk_ref, dec_ref, gn_ref, s_ref)
            for sl, o in zip(sls, outs):
                ro_ref[rows, sl] = o

        tb = dm_ref.shape[1]
        starts = list(range(0, h.shape[0], tb))
        per = -(-len(projections) // len(starts))
        for i, r0 in enumerate(starts):
            retention_rows(r0)
            for proj in projections[i * per:(i + 1) * per]:
                proj()
    else:
        for sl, a, b in zip(sls, rqs, rks):
            rq_ref[:, sl] = a
            rk_ref[:, sl] = b
        rv_ref[...] = zv.astype(BF16)
        rg_ref[...] = zg
        for proj in projections:
            proj()


def _inproj(x, g, wr, wf, wl, bl, cos_t, sin_t, tm, seq, gn, s0, chunk):
    n = x.shape[0]
    nt = cos_t.shape[0] // tm
    head_major = tm <= seq
    row = lambda w: pl.BlockSpec((tm, w), lambda i: (i, 0))
    tab = pl.BlockSpec((tm, LANES), lambda i: (i % nt, 0))
    shp = lambda w, dt: jax.ShapeDtypeStruct((n, w), dt)
    hsum = jnp.asarray(np.arange(FOX_W)[:, None] // FOX_D == np.arange(LANES)[None, :], BF16)
    stat_spec = pl.BlockSpec((1, 1, LANES), lambda i: (i, 0, 0))
    stat_shape = jax.ShapeDtypeStruct((n // tm, 1, LANES), F32)
    if head_major:
        per_batch = seq // tm
        kv_spec = pl.BlockSpec((1, FOX_H, FOX_D, tm), lambda i: (i // per_batch, 0, 0, i % per_batch))
        kv_shape = jax.ShapeDtypeStruct((n // seq, FOX_H, FOX_D, seq), F32)
        lf_spec = pl.BlockSpec((1, FOX_H, tm), lambda i: (i // per_batch, 0, i % per_batch))
        lf_shape = jax.ShapeDtypeStruct((n // seq, FOX_H, seq), F32)
        consts = _retention_consts(min(RET_BLOCK, tm), chunk)
        st_spec = pl.BlockSpec((1, RET_H, RET_D, RET_D), lambda i: (i // per_batch, 0, 0, 0))
        extra_in = consts + [gn, s0]
        extra_in_specs = [_const_spec(c.shape) for c in consts] + [_const_spec(gn.shape), st_spec]
        ret_specs = [row(RET_W), st_spec]
        ret_shapes = [shp(RET_W, BF16), jax.ShapeDtypeStruct(s0.shape, F32)]
    else:
        per_batch = 1
        kv_spec, kv_shape = row(FOX_W), shp(FOX_W, F32)
        lf_spec, lf_shape = row(FOX_H), shp(FOX_H, F32)
        extra_in, extra_in_specs = [], []
        ret_specs = [row(RET_W)] * 4
        ret_shapes = [shp(RET_W, BF16), shp(RET_W, BF16), shp(RET_W, BF16), shp(RET_W, F32)]
    return pl.pallas_call(
        functools.partial(_inproj_body, head_major=head_major, per_batch=per_batch),
        grid=(n // tm,),
        in_specs=[row(D_MODEL), _const_spec(g.shape), _const_spec(wr.shape), _const_spec(wf.shape),
                  _const_spec(wl.shape), _const_spec(bl.shape), tab, tab, _const_spec(hsum.shape)] + extra_in_specs,
        out_specs=ret_specs + [row(FOX_W), row(FOX_W), row(FOX_H * LANES), kv_spec, kv_spec, lf_spec,
                               stat_spec, stat_spec],
        out_shape=ret_shapes + [shp(FOX_W, BF16), shp(FOX_W, BF16), shp(FOX_H * LANES, BF16), kv_shape,
                                kv_shape, lf_shape, stat_shape, stat_shape],
        compiler_params=_params(("arbitrary",)),
        name="inproj_ab_retention" if head_major else "inproj_ab",
    )(x, g, wr, wf, wl, bl, cos_t, sin_t, hsum, *extra_in)


def _cumsum_body(x_ref, u_ref, o_ref):
    rows, total = x_ref.shape
    u = u_ref[...]

    def chunk(i, carry):
        off = pl.multiple_of(i * CUMSUM_CHUNK, CUMSUM_CHUNK)
        xc = x_ref[:, pl.ds(off, CUMSUM_CHUNK)]
        hi = xc.astype(BF16)
        r1 = xc - hi.astype(F32)
        mid = r1.astype(BF16)
        lo = (r1 - mid.astype(F32)).astype(BF16)
        y = _dot(hi, u) + _dot(mid, u) + _dot(lo, u) + carry
        o_ref[:, pl.ds(off, CUMSUM_CHUNK)] = y * LOG2E
        return y[:, CUMSUM_CHUNK - 1:CUMSUM_CHUNK]

    lax.fori_loop(0, total // CUMSUM_CHUNK, chunk, jnp.zeros((rows, 1), F32))


def _cumsum_lanes(x):
    idx = np.arange(CUMSUM_CHUNK)
    u = jnp.asarray(idx[:, None] <= idx[None, :], BF16)
    return pl.pallas_call(
        _cumsum_body,
        out_shape=jax.ShapeDtypeStruct(x.shape, F32),
        compiler_params=pltpu.CompilerParams(vmem_limit_bytes=VMEM_LIMIT),
        name="logf_cumsum",
    )(x, u)


def _ret_body(q_ref, k_ref, v_ref, rg_ref, dm_ref, wq_ref, wk_ref, dec_ref, gn_ref, s0_ref,
              ro_ref, s_ref):
    @pl.when(pl.program_id(1) == 0)
    def _():
        s_ref[...] = s0_ref[...]

    sls = [slice(hh * RET_D, (hh + 1) * RET_D) for hh in range(RET_H)]
    outs = _retention_block([q_ref[:, sl] for sl in sls], [k_ref[:, sl] for sl in sls],
                            [v_ref[:, sl] for sl in sls], [rg_ref[:, sl] for sl in sls],
                            dm_ref, wq_ref, wk_ref, dec_ref, gn_ref, s_ref)
    for sl, o in zip(sls, outs):
        ro_ref[:, sl] = o


def _retention(rq, rk, rv, rg, gn, s0, batch, seq, chunk):
    tb = min(RET_BLOCK, seq)
    nb = seq // tb
    consts = _retention_consts(tb, chunk)
    row = pl.BlockSpec((tb, RET_W), lambda b, t: (b * nb + t, 0))
    st = pl.BlockSpec((1, RET_H, RET_D, RET_D), lambda b, t: (b, 0, 0, 0))
    return pl.pallas_call(
        _ret_body,
        grid=(batch, nb),
        in_specs=[row, row, row, row] + [_const_spec(c.shape) for c in consts] + [_const_spec(gn.shape), st],
        out_specs=[row, st],
        out_shape=[jax.ShapeDtypeStruct((batch * seq, RET_W), BF16),
                   jax.ShapeDtypeStruct((batch, RET_H, RET_D, RET_D), F32)],
        compiler_params=_params(("parallel", "arbitrary")),
        name="retention",
    )(rq, rk, rv, rg, *consts, gn, s0)


def _fox_body(start_ref, q_ref, k_ref, v_ref, c_ref, mask_ref, o_ref, qs_scr, ql_scr, m_scr, acc_scr, *, tq, tk):
    n_full = pl.program_id(2)
    pair = pl.program_id(0) * pl.num_programs(1) + pl.program_id(1)
    first_a = start_ref[(2 * pair) * pl.num_programs(2) + n_full]
    first_b = start_ref[(2 * pair + 1) * pl.num_programs(2) + n_full]
    solo_first = jnp.minimum(first_a, first_b)
    first = jnp.maximum(first_a, first_b)
    solo_head = (first_b < first_a).astype(jnp.int32)
    q = q_ref[...]
    lane = lax.broadcasted_iota(jnp.int32, q.shape, 1)
    zero = jnp.zeros_like(q)
    qs_scr[0:tq, :] = jnp.where(lane < FOX_D, q, zero)
    qs_scr[tq:2 * tq, :] = jnp.where(lane >= FOX_D, q, zero)
    ql_scr[0:tq // 2, :] = qs_scr[tq // 2:tq, :]
    ql_scr[tq // 2:tq, :] = qs_scr[tq + tq // 2:2 * tq, :]
    m_scr[...] = jnp.full(m_scr.shape, -jnp.inf, F32)
    acc_scr[...] = jnp.zeros(acc_scr.shape, F32)

    def stacked_tile(q_scr, q_rows, state_rows, off, width, mask):
        s = _dot_nt(q_scr[...], k_ref[0, pl.ds(off, width), :])
        shs = [s[q_rows[hh], :] - c_ref[0, 0, hh:hh + 1, pl.ds(off, width)] for hh in range(2)]
        if mask is not None:
            shs = [sh + mask for sh in shs]
        chunks = [[sh[:, c * LANES:(c + 1) * LANES] for c in range(width // LANES)] for sh in shs]
        m_prevs = [m_scr[rows, :] for rows in state_rows]
        m_news = [jnp.maximum(m_prevs[hh], jnp.max(functools.reduce(jnp.maximum, chunks[hh]), axis=1, keepdims=True))
                  for hh in range(2)]
        ps = [jnp.concatenate([jnp.exp2(ch - m_news[hh]) for ch in chunks[hh]], axis=1).astype(BF16)
              for hh in range(2)]
        pvs = [_dot(ps[hh], v_ref[0, pl.ds(off, width), hh * LANES:(hh + 1) * LANES]) for hh in range(2)]
        for hh in range(2):
            rows = state_rows[hh]
            acc_scr[rows, :] = jnp.exp2(m_prevs[hh] - m_news[hh]) * acc_scr[rows, :] + pvs[hh]
            m_scr[rows, :] = m_news[hh]

    head_rows = [slice(hh * tq, (hh + 1) * tq) for hh in range(2)]

    def tile(kj):
        stacked_tile(qs_scr, head_rows, head_rows, pl.multiple_of(kj * tk, tk), tk, None)

    def diagonal_tile():
        hq = tq // 2
        off = pl.multiple_of(n_full * tk, tk)
        stacked_tile(qs_scr, head_rows, head_rows, off, hq, mask_ref[...])
        lower = [slice(hh * tq + hq, (hh + 1) * tq) for hh in range(2)]
        stacked_tile(ql_scr, [slice(hh * hq, (hh + 1) * hq) for hh in range(2)], lower,
                     pl.multiple_of(off + hq, hq), hq, mask_ref[0:hq, :])

    def solo_tile(kj):
        off = pl.multiple_of(kj * tk, tk)
        rows = pl.ds(pl.multiple_of(solo_head * tq, tq), tq)
        sh = (_dot_nt(qs_scr[rows, :], k_ref[0, pl.ds(off, tk), :])
              - c_ref[0, 0, pl.ds(solo_head, 1), pl.ds(off, tk)])
        chunks = [sh[:, c * LANES:(c + 1) * LANES] for c in range(tk // LANES)]
        m_prev = m_scr[rows, :]
        m_new = jnp.maximum(m_prev, jnp.max(functools.reduce(jnp.maximum, chunks), axis=1, keepdims=True))
        p = jnp.concatenate([jnp.exp2(ch - m_new) for ch in chunks], axis=1).astype(BF16)
        pv = _dot(p, v_ref[0, pl.ds(off, tk), :])
        pv = jnp.where(solo_head == 0, pv[:, 0:LANES], pv[:, LANES:2 * LANES])
        acc_scr[rows, :] = jnp.exp2(m_prev - m_new) * acc_scr[rows, :] + pv
        m_scr[rows, :] = m_new

    def run_tiles(start, count, visit):
        def group(i, carry):
            for j in range(FOX_UNROLL):
                visit(start + FOX_UNROLL * i + j)
            return carry

        lax.fori_loop(0, count // FOX_UNROLL, group, 0)
        span = FOX_UNROLL // 2
        while span >= 1:
            @pl.when((count // span) % 2 == 1)
            def _(span=span):
                base = start + (count // (2 * span)) * (2 * span)
                for j in range(span):
                    visit(base + j)
            span //= 2

    run_tiles(solo_first, first - solo_first, solo_tile)
    run_tiles(first, n_full - first, tile)
    diagonal_tile()

    acc_a = acc_scr[0:tq, :]
    acc_b = acc_scr[tq:2 * tq, :]
    half = LANES // 2
    o_ref[...] = jnp.where(lane < FOX_D, acc_a / pltpu.roll(acc_a, half, 1),
                           acc_b / pltpu.roll(acc_b, half, 1)).astype(BF16)


def _fox_first_tiles(qn2, kn2, c_all, batch, seq, tile_len):
    nq = seq // tile_len
    qn = jnp.sqrt(qn2[:, 0, 0:FOX_H]).reshape(batch, nq, FOX_H).transpose(0, 2, 1)
    kn = jnp.sqrt(kn2[:, 0, 0:FOX_H]).reshape(batch, nq, FOX_H).transpose(0, 2, 1)
    c = c_all[..., 0:seq].reshape(batch, FOX_H, nq, tile_len)
    c_first, c_last = c[..., 0], c[..., tile_len - 1]
    dots = FOX_NORM_SLACK * (qn[..., :, None] * kn[..., None, :] + (qn * kn)[..., :, None])
    gap = dots + c_first[..., :, None] - c_last[..., None, :]
    before = jnp.arange(nq)[None, :] < jnp.arange(nq)[:, None]
    droppable = (gap < -FOX_DROP_LOG2) & before
    first = jnp.sum(jnp.cumprod(droppable.astype(jnp.int32), axis=-1), axis=-1)
    return first.reshape(-1).astype(jnp.int32)


def _fox(first_tile, fq, k_all, v_all, c_all, batch, seq, tile_len):
    tq = tk = tile_len
    nq = seq // tq
    half = tile_len // 2
    mask = jnp.asarray(np.where(np.arange(half)[None, :] <= np.arange(tile_len)[:, None], 0.0, NEG_INF), F32)
    body = functools.partial(_fox_body, tq=tq, tk=tk)
    grid_spec = pltpu.PrefetchScalarGridSpec(
        num_scalar_prefetch=1,
        grid=(batch, FOX_H // 2, nq),
        in_specs=[pl.BlockSpec((tq, LANES), lambda b, h, i, _: (b * nq + i, h)),
                  pl.BlockSpec((1, seq, LANES), lambda b, h, i, _: (b, 0, h)),
                  pl.BlockSpec((1, seq, 2 * LANES), lambda b, h, i, _: (b, 0, h)),
                  pl.BlockSpec((1, 1, 2, seq), lambda b, h, i, _: (b, h, 0, 0)),
                  _const_spec(mask.shape)],
        out_specs=pl.BlockSpec((tq, LANES), lambda b, h, i, _: (b * nq + i, h)),
        scratch_shapes=[pltpu.VMEM((2 * tq, LANES), BF16), pltpu.VMEM((tq, LANES), BF16),
                        pltpu.VMEM((2 * tq, LANES), F32), pltpu.VMEM((2 * tq, LANES), F32)])
    return pl.pallas_call(
        body,
        grid_spec=grid_spec,
        out_shape=jax.ShapeDtypeStruct((batch * seq, FOX_W), BF16),
        compiler_params=_params(("parallel", "parallel", "arbitrary")),
        name="fox_attention",
    )(first_tile, fq, k_all, v_all, c_all, mask)


def _fox_cached_body(q_ref, kp_ref, vp_ref, kn_ref, vn_ref, c_ref, o_ref, *, past, lq):
    q = q_ref[...]
    lane = lax.broadcasted_iota(jnp.int32, q.shape, 1)
    zero = jnp.zeros_like(q)
    qs = jnp.concatenate([jnp.where(lane < FOX_D, q, zero), jnp.where(lane >= FOX_D, q, zero)], axis=0)
    head_rows = lambda a, b, width: jnp.concatenate(
        [jnp.broadcast_to(a, (lq, width)), jnp.broadcast_to(b, (lq, width))], axis=0)
    s_past = _dot(qs, kp_ref[0].astype(BF16)) - head_rows(c_ref[0, 0, 0:1, 0:past], c_ref[0, 0, 1:2, 0:past], past)
    s_new = _dot_nt(qs, kn_ref[...]) - head_rows(c_ref[0, 0, 0:1, past:past + lq],
                                                 c_ref[0, 0, 1:2, past:past + lq], lq)
    qi = lax.broadcasted_iota(jnp.int32, s_new.shape, 0)
    ki = lax.broadcasted_iota(jnp.int32, s_new.shape, 1)
    s_new = jnp.where(ki <= jnp.where(qi >= lq, qi - lq, qi), s_new, NEG_INF)
    m = jnp.maximum(jnp.max(s_past, axis=1, keepdims=True), jnp.max(s_new, axis=1, keepdims=True))
    p_past = jnp.exp2(s_past - m)
    p_new = jnp.exp2(s_new - m)
    denom = jnp.sum(p_past, axis=1, keepdims=True) + jnp.sum(p_new, axis=1, keepdims=True)
    pb_new = p_new.astype(BF16)
    acc = _dot_nt(p_past.astype(BF16), vp_ref[0].astype(BF16))
    acc_a = (acc[0:lq] + _dot(pb_new[0:lq], vn_ref[:, 0:LANES])) / denom[0:lq]
    acc_b = (acc[lq:2 * lq] + _dot(pb_new[lq:2 * lq], vn_ref[:, LANES:2 * LANES])) / denom[lq:2 * lq]
    o_ref[...] = jnp.where(lane < FOX_D, acc_a, acc_b).astype(BF16)


def _fox_cached(fq, k_past, v_past, k_new, v_new, c_all, batch, lq):
    past = k_past.shape[2]
    new_rows = lambda w: pl.BlockSpec((lq, w), lambda b, h: (b, h))
    cache = pl.BlockSpec((1, LANES, past), lambda b, h: (b, h, 0))
    return pl.pallas_call(
        functools.partial(_fox_cached_body, past=past, lq=lq),
        grid=(batch, FOX_H // 2),
        in_specs=[new_rows(LANES), cache, cache, new_rows(LANES), new_rows(2 * LANES),
                  pl.BlockSpec((1, 1, 2, c_all.shape[-1]), lambda b, h: (b, h, 0, 0))],
        out_specs=new_rows(LANES),
        out_shape=jax.ShapeDtypeStruct((batch * lq, FOX_W), BF16),
        compiler_params=_params(("parallel", "parallel")),
        name="fox_attention_cached",
    )(fq, k_past, v_past, k_new, v_new, c_all)


def _memory_kv_body(x_ref, g_ref, w_ref, k_ref, v_ref, kb_ref, vb_ref):
    y = _dot(_rms(x_ref[...], g_ref[0]).astype(BF16), w_ref[0])
    k = y[:, 0:D_MODEL]
    v = y[:, D_MODEL:2 * D_MODEL]
    k_ref[0] = k
    v_ref[0] = v
    kb_ref[0] = k.astype(BF16)
    vb_ref[0] = v.astype(BF16)


def _memory_kv(x, g, w, tm):
    n = x.shape[0]
    layers = w.shape[0]
    out = pl.BlockSpec((1, tm, D_MODEL), lambda l, i: (l, i, 0))
    shape = lambda dt: jax.ShapeDtypeStruct((layers, n, D_MODEL), dt)
    return pl.pallas_call(
        _memory_kv_body,
        grid=(layers, n // tm),
        in_specs=[pl.BlockSpec((tm, D_MODEL), lambda l, i: (i, 0)),
                  pl.BlockSpec((1, 1, D_MODEL), lambda l, i: (l, 0, 0)),
                  pl.BlockSpec((1, D_MODEL, 2 * D_MODEL), lambda l, i: (l, 0, 0))],
        out_specs=[out] * 4,
        out_shape=[shape(F32), shape(F32), shape(BF16), shape(BF16)],
        compiler_params=_params(("parallel", "parallel")),
        name="memory_kv",
    )(x, g, w)


def _post_body(*refs, has_mix, parts, seq_rows, mem_len):
    if has_mix:
        (x_ref, ro_ref, fo_ref, wo_ref, nw_ref, wxq_ref, wxo_ref, mk_ref, mv_ref, wup_ref, wdn_ref,
         o_ref) = refs
    else:
        x_ref, nw_ref, wxq_ref, wxo_ref, mk_ref, mv_ref, wup_ref, wdn_ref, o_ref = refs
    rows = [slice(r0, r0 + nrows) for r0, nrows in parts]
    xs = [x_ref[rs, :] for rs in rows]
    if has_mix:
        mixes = [_dot(ro_ref[rs, :], wo_ref[0:RET_W, :]) + _dot(fo_ref[rs, :], wo_ref[RET_W:RET_W + FOX_W, :])
                 for rs in rows]
        xs = [x + _rms(mix, nw_ref[1:2, :]) for x, mix in zip(xs, mixes)]

    qs = [_dot(_rms(x, nw_ref[2:3, :]).astype(BF16), wxq_ref[...]) for x in xs]
    ctx = []
    for (r0, nrows), q in zip(parts, qs):
        groups = []
        for g0 in range(0, nrows, min(seq_rows, nrows)):
            gs = slice(g0, g0 + min(seq_rows, nrows))
            m0 = ((r0 + g0) // seq_rows) * mem_len
            heads = []
            for hh in range(X_H):
                sl = slice(hh * X_D, (hh + 1) * X_D)
                s = _dot_nt(q[gs, sl].astype(BF16), mk_ref[m0:m0 + mem_len, sl]) * (X_D ** -0.5)
                p = jnp.exp(s - jnp.max(s, axis=-1, keepdims=True))
                o = _dot(p.astype(BF16), mv_ref[m0:m0 + mem_len, sl]) / jnp.sum(p, axis=-1, keepdims=True)
                heads.append(o.astype(BF16))
            groups.append(jnp.concatenate(heads, axis=1))
        ctx.append(groups[0] if len(groups) == 1 else jnp.concatenate(groups, axis=0))
    atts = [_dot(c, wxo_ref[...]) for c in ctx]
    xs = [x + _rms(att, nw_ref[3:4, :]) for x, att in zip(xs, atts)]

    hs = [_rms(x, nw_ref[4:5, :]).astype(BF16) for x in xs]
    accs = [None] * len(parts)
    for c in range(D_FF // D_MODEL):
        sl = slice(c * D_MODEL, (c + 1) * D_MODEL)
        for i, h in enumerate(hs):
            r = jnp.maximum(_dot(h, wup_ref[:, sl]), 0.0)
            part = _dot((r * r).astype(BF16), wdn_ref[sl, :])
            accs[i] = part if accs[i] is None else accs[i] + part
    for rs, x, acc in zip(rows, xs, accs):
        o_ref[rs, :] = x + _rms(acc, nw_ref[5:6, :])


def _post(x, mix_in, nw, wxq, wxo, mk, mv, wup, wdn, seq, tm):
    n = x.shape[0]
    mem_len = mk.shape[0] // (n // seq)
    row = lambda w: pl.BlockSpec((tm, w), lambda i: (i, 0))
    if tm <= seq:
        per_batch = seq // tm
        mem = pl.BlockSpec((mem_len, D_MODEL), lambda i: (i // per_batch, 0))
        seq_rows = tm
        nparts = POST_PARTS if tm % (POST_PARTS * 16) == 0 else 1
        parts = tuple((p * (tm // nparts), tm // nparts) for p in range(nparts))
    else:
        mem = pl.BlockSpec(((tm // seq) * mem_len, D_MODEL), lambda i: (i, 0))
        seq_rows = seq
        parts = ((0, tm),)
    args = [x]
    specs = [row(D_MODEL)]
    if mix_in is not None:
        ro, fo, wo = mix_in
        args += [ro, fo, wo]
        specs += [row(RET_W), row(FOX_W), _const_spec(wo.shape)]
    args += [nw, wxq, wxo, mk, mv, wup, wdn]
    specs += [_const_spec(nw.shape), _const_spec(wxq.shape), _const_spec(wxo.shape), mem, mem,
              _const_spec(wup.shape), _const_spec(wdn.shape)]
    return pl.pallas_call(
        functools.partial(_post_body, has_mix=mix_in is not None, parts=parts, seq_rows=seq_rows, mem_len=mem_len),
        grid=(n // tm,),
        in_specs=specs,
        out_specs=row(D_MODEL),
        out_shape=jax.ShapeDtypeStruct((n, D_MODEL), F32),
        compiler_params=_params(("parallel",)),
        name="xattn_mlp_mix" if mix_in is not None else "xattn_mlp",
    )(*args)


def _s5prep_body(lre_ref, lim_ref, ldt_ref, lre_rep_ref, lim_rep_ref, bre_ref, bim_ref,
                 are_ref, aim_ref, bbre_ref, bbim_ref):
    dt = jnp.exp(ldt_ref[...])

    def zoh(lre, lim):
        mag = jnp.exp(lre * dt)
        a_re = mag * jnp.cos(lim * dt)
        a_im = mag * jnp.sin(lim * dt)
        den = lre * lre + lim * lim
        num_re = a_re - 1.0
        f_re = (num_re * lre + a_im * lim) / den
        f_im = (a_im * lre - num_re * lim) / den
        return a_re, a_im, f_re, f_im

    a_re, a_im, _, _ = zoh(lre_ref[...], lim_ref[...])
    are_ref[...] = a_re
    aim_ref[...] = a_im
    _, _, f_re, f_im = zoh(lre_rep_ref[...], lim_rep_ref[...])
    b_re = bre_ref[...]
    b_im = bim_ref[...]
    bbre_ref[...] = f_re * b_re - f_im * b_im
    bbim_ref[...] = f_re * b_im + f_im * b_re


def _s5_discretize(lam_re, lam_im, log_dt, b_re, b_im):
    rep = lambda a: jnp.repeat(a, S5_GROUP, axis=1)
    wide = jax.ShapeDtypeStruct((S5_G, S5_P * S5_GROUP), F32)
    small = jax.ShapeDtypeStruct((S5_G, S5_P), F32)
    a_re, a_im, bb_re, bb_im = pl.pallas_call(
        _s5prep_body, out_shape=[small, small, wide, wide], name="s5_discretize",
    )(lam_re, lam_im, log_dt.reshape(S5_G, 1), rep(lam_re), rep(lam_im),
      b_re.reshape(S5_G, -1), b_im.reshape(S5_G, -1))
    a = jnp.stack([a_re.reshape(-1), a_im.reshape(-1)])
    gpc = S5_LANE_CHUNK // S5_P
    nch = S5_G // gpc
    eye = jnp.eye(gpc, dtype=F32)

    def in_blocks(bb):
        t = bb.reshape(nch, gpc, S5_P, S5_GROUP).transpose(0, 1, 3, 2)
        return (t[:, :, :, None, :] * eye[None, :, None, :, None]).reshape(nch, gpc * S5_GROUP, gpc * S5_P)

    bmat = jnp.concatenate([in_blocks(bb_re), in_blocks(bb_im)], axis=-1).astype(BF16)
    return a, bmat


def _s5_out_matrices(c_re, c_im):
    gpc = S5_LANE_CHUNK // S5_P
    nch = S5_G // gpc
    eye = jnp.eye(gpc, dtype=F32)

    def out_blocks(c):
        t = c.reshape(nch, gpc, S5_GROUP, S5_P).transpose(0, 1, 3, 2)
        return (t[:, :, :, None, :] * eye[None, :, None, :, None]).reshape(nch, gpc * S5_P, gpc * S5_GROUP)

    return jnp.concatenate([out_blocks(c_re), -out_blocks(c_im)], axis=1).astype(BF16)


def _s5_body(x_ref, nw_ref, perm_ref, win_ref, bmat_ref, a_ref, cmat_ref, d_ref, wglu_ref, st0_ref,
             o_ref, st_ref, u_scr, bu_scr, y_scr, *, ts):
    nb = x_ref.shape[0]
    rows = nb * ts
    lc = S5_LANE_CHUNK
    gw = (lc // S5_P) * S5_GROUP

    @pl.when(pl.program_id(0) == 0)
    def _():
        st_ref[...] = st0_ref[...]

    x = x_ref[...].reshape(rows, D_MODEL)
    h = _dot(perm_ref[0], _rms(x, nw_ref[0:1, :]).astype(BF16)).astype(BF16)
    u_scr[...] = _dot(h, win_ref[...])
    nv = lc // LANES
    for j in range(S5_STATE // lc):
        uj = u_scr[:, j * gw:(j + 1) * gw]
        bu = _dot(uj.astype(BF16), bmat_ref[j])
        for c in range(2 * nv):
            bu_scr[c] = bu[:, c * LANES:(c + 1) * LANES]
        a_re = [jnp.broadcast_to(a_ref[0:1, j * lc + c * LANES:j * lc + (c + 1) * LANES], (nb, LANES))
                for c in range(nv)]
        a_im = [jnp.broadcast_to(a_ref[1:2, j * lc + c * LANES:j * lc + (c + 1) * LANES], (nb, LANES))
                for c in range(nv)]

        def step(t, carry):
            idx = pl.ds(pl.multiple_of(t * nb, nb), nb)
            new = []
            for c in range(nv):
                s_re, s_im = carry[c], carry[nv + c]
                new.append(a_re[c] * s_re - a_im[c] * s_im + bu_scr[c, idx, :])
            for c in range(nv):
                s_re, s_im = carry[c], carry[nv + c]
                new.append(a_re[c] * s_im + a_im[c] * s_re + bu_scr[nv + c, idx, :])
            for c in range(2 * nv):
                bu_scr[c, idx, :] = new[c]
            return tuple(new)

        init = tuple(st_ref[:, j * lc + c * LANES:j * lc + (c + 1) * LANES] for c in range(nv)) + tuple(
            st_ref[:, S5_STATE + j * lc + c * LANES:S5_STATE + j * lc + (c + 1) * LANES] for c in range(nv))
        fin = lax.fori_loop(0, ts, step, init, unroll=True)
        for c in range(nv):
            st_ref[:, j * lc + c * LANES:j * lc + (c + 1) * LANES] = fin[c]
            st_ref[:, S5_STATE + j * lc + c * LANES:S5_STATE + j * lc + (c + 1) * LANES] = fin[nv + c]
        xs = jnp.concatenate([bu_scr[c] for c in range(2 * nv)], axis=1).astype(BF16)
        y_scr[:, j * gw:(j + 1) * gw] = _dot(xs, cmat_ref[j]) + d_ref[0:1, j * gw:(j + 1) * gw] * uj
    y = _dot(perm_ref[1], y_scr[...].astype(BF16)).astype(BF16)
    g = _dot(y, wglu_ref[...])
    out = g[:, 0:D_MODEL] * jax.nn.sigmoid(g[:, D_MODEL:2 * D_MODEL])
    o_ref[...] = (x + _rms(out, nw_ref[1:2, :])).reshape(nb, ts, D_MODEL)


def _s5_layer(x3, nw, win, bmat, a, cmat, d, wglu, st0, ts):
    nb, seq, _ = x3.shape
    rows = nb * ts
    blk = pl.BlockSpec((nb, ts, D_MODEL), lambda t: (0, t, 0))
    r = np.arange(rows)
    fwd = np.zeros((rows, rows), np.float32)
    fwd[(r % ts) * nb + r // ts, r] = 1.0
    perm = jnp.asarray(np.stack([fwd, fwd.T]), BF16)
    consts = [nw, perm, win, bmat, a, cmat, d, wglu, st0]
    return pl.pallas_call(
        functools.partial(_s5_body, ts=ts),
        grid=(seq // ts,),
        in_specs=[blk] + [_const_spec(c.shape) for c in consts],
        out_specs=[blk, pl.BlockSpec(st0.shape, lambda t: (0, 0))],
        out_shape=[jax.ShapeDtypeStruct(x3.shape, F32), jax.ShapeDtypeStruct(st0.shape, F32)],
        scratch_shapes=[pltpu.VMEM((rows, D_MODEL), F32),
                        pltpu.VMEM((2 * S5_LANE_CHUNK // LANES, rows, LANES), F32),
                        pltpu.VMEM((rows, D_MODEL), F32)],
        compiler_params=_params(("arbitrary",)),
        name="s5_layer",
    )(x3, *consts)


def _rotary_tables(pos, rows):
    half = RET_D // 2
    inv = ROPE_BASE ** (-jnp.arange(half, dtype=F32) / half)
    ang = pos.astype(F32)[:, None] * inv[None, :]
    cos = jnp.cos(ang)
    sin = jnp.sin(ang)
    cos_t = jnp.concatenate([cos, cos], axis=1)
    sin_t = jnp.concatenate([-sin, sin], axis=1)
    reps = max(1, rows // pos.shape[0])
    return jnp.tile(cos_t, (reps, 1)), jnp.tile(sin_t, (reps, 1))


def _prepare_weights(norm_w, w_in_ab, b_fox_f, w_out_ab, w_in_c, s5_lambda_re, s5_lambda_im, s5_log_dt,
                     s5_b_re, s5_b_im, s5_c_re, s5_c_im, s5_d, w_glu, w_xq, w_xo, w_up, w_down):
    w = {}
    wab = w_in_ab[0].astype(BF16)
    w["wr"] = wab[:, 0:4 * RET_W]
    w["wf"] = wab[:, 4 * RET_W:4 * RET_W + 3 * FOX_W]
    w["wl"] = jnp.pad(wab[:, 4 * RET_W + 3 * FOX_W:], ((0, 0), (0, LANES - FOX_H)))
    w["bl"] = jnp.pad(b_fox_f[0].astype(F32), (0, LANES - FOX_H)).reshape(1, LANES)
    w["wo"] = w_out_ab[0].astype(BF16)
    w["nw"] = norm_w.astype(F32)
    w["win_c"] = w_in_c[0].astype(BF16)
    w["a"], w["bmat"] = _s5_discretize(s5_lambda_re[0], s5_lambda_im[0], s5_log_dt[0], s5_b_re[0], s5_b_im[0])
    w["cmat"] = _s5_out_matrices(s5_c_re[0].astype(F32), s5_c_im[0].astype(F32))
    w["d"] = s5_d[0].astype(F32).reshape(1, D_MODEL)
    w["wglu"] = w_glu[0].astype(BF16)
    for name, arr in (("wxq", w_xq), ("wxo", w_xo), ("wup", w_up), ("wdn", w_down)):
        w[name] = arr.astype(BF16)
    return w


def _run_group(x3, pos, w, gn_w, mem_kv, ret_s0, fox_past, s5_st0):
    batch, seq, _ = x3.shape
    n = batch * seq
    tm = min(ROW_TILE, n)
    post_tile = lambda t: t if seq >= t else min(n, (t // seq) * seq)
    x = x3.reshape(n, D_MODEL)

    cos_t, sin_t = _rotary_tables(pos, tm)
    chunk = CHUNK if fox_past is None else seq
    gn = gn_w.reshape(RET_H, 1, RET_D)
    outs = _inproj(x, w["nw"][0, 0:1], w["wr"], w["wf"], w["wl"], w["bl"], cos_t, sin_t, tm, seq, gn, ret_s0, chunk)
    fq, fkb, fvp, fk, fv, lf, qn2, kn2 = outs[-8:]
    if tm <= seq:
        ro, ret_state = outs[0:2]
        lf_t = lf
        fk = jnp.transpose(fk, (0, 3, 1, 2))
        fv = jnp.transpose(fv, (0, 3, 1, 2))
        lf3 = jnp.transpose(lf, (0, 2, 1))
    else:
        lf3 = lf.reshape(batch, seq, FOX_H)
        lf_t = jnp.transpose(lf3, (0, 2, 1))
    if fox_past is not None:
        lf_t = jnp.concatenate([jnp.transpose(fox_past[2].astype(F32), (0, 2, 1)), lf_t], axis=2)
    lk = lf_t.shape[2]
    pad = -lk % CUMSUM_CHUNK
    lf_rows = jnp.pad(lf_t.reshape(batch * FOX_H, lk), ((0, 0), (0, pad)))
    c_all = _cumsum_lanes(lf_rows).reshape(batch, FOX_H // 2, 2, lk + pad)
    if fox_past is None:
        assert tm == FOX_TILE
        first_tile = _fox_first_tiles(qn2, kn2, c_all, batch, seq, FOX_TILE)
        fo = _fox(first_tile, fq, fkb.reshape(batch, seq, FOX_W), fvp.reshape(batch, seq, FOX_H * LANES), c_all,
                  batch, seq, FOX_TILE)
    else:
        past = fox_past[0].shape[1]
        cache_t = lambda a: jnp.transpose(a, (0, 2, 3, 1)).reshape(batch, FOX_W, past)
        fo = _fox_cached(fq, cache_t(fox_past[0]), cache_t(fox_past[1]), fkb, fvp, c_all, batch, seq)

    if tm > seq:
        ro, ret_state = _retention(*outs[0:4], gn, ret_s0, batch, seq, chunk)

    x = _post(x, (ro, fo, w["wo"]), w["nw"][0], w["wxq"][0], w["wxo"][0], mem_kv[0][0], mem_kv[0][1],
              w["wup"][0], w["wdn"][0], seq, post_tile(ROW_TILE))

    ts = min(S5_TS, seq)
    x3n, s5_state = _s5_layer(x.reshape(batch, seq, D_MODEL), w["nw"][1], w["win_c"], w["bmat"], w["a"],
                              w["cmat"], w["d"], w["wglu"], s5_st0, ts)
    x = _post(x3n.reshape(n, D_MODEL), None, w["nw"][1], w["wxq"][1], w["wxo"][1], mem_kv[1][0], mem_kv[1][1],
              w["wup"][1], w["wdn"][1], seq, post_tile(POST_TILE))

    new = dict(ret=ret_state[None], fox_k=fk.reshape(1, batch, seq, FOX_H, FOX_D),
               fox_v=fv.reshape(1, batch, seq, FOX_H, FOX_D), fox_logf=lf3[None],
               s5_re=s5_state[:, 0:S5_STATE].reshape(1, batch, S5_G, S5_P),
               s5_im=s5_state[:, S5_STATE:].reshape(1, batch, S5_G, S5_P))
    return x.reshape(batch, seq, D_MODEL), new


def kernel(x_prompt, x_sample, cache_ret_state, cache_fox_k, cache_fox_v, cache_fox_logf, state_s5_re, state_s5_im, cache_mem_k, cache_mem_v, mem_prompt, norm_w, w_in_ab, b_fox_f, ret_gn_w, w_out_ab, w_in_c, s5_lambda_re, s5_lambda_im, s5_log_dt, s5_b_re, s5_b_im, s5_c_re, s5_c_im, s5_d, w_glu, mem_norm_w, w_xq, w_xk, w_xv, w_xo, w_up, w_down):
    w = _prepare_weights(norm_w, w_in_ab, b_fox_f, w_out_ab, w_in_c, s5_lambda_re, s5_lambda_im, s5_log_dt,
                         s5_b_re, s5_b_im, s5_c_re, s5_c_im, s5_d, w_glu, w_xq, w_xo, w_up, w_down)
    batch, seq, _ = x_prompt.shape
    dec_batch, dec_seq, _ = x_sample.shape
    mem_len = mem_prompt.shape[1]
    depth = norm_w.shape[0]
    gn_w = ret_gn_w[0].astype(F32)

    mem_rows = mem_prompt.reshape(batch * mem_len, D_MODEL)
    wkv = jnp.concatenate([w_xk, w_xv], axis=2).astype(BF16)
    mem_k, mem_v, mem_kb, mem_vb = _memory_kv(mem_rows, mem_norm_w.astype(F32).reshape(depth, 1, D_MODEL), wkv,
                                              min(ROW_TILE, batch * mem_len))
    mem_kv_prompt = [(mem_kb[layer], mem_vb[layer]) for layer in range(depth)]

    pos_prompt = jnp.arange(seq)
    y_prompt, np_ = _run_group(
        x_prompt, pos_prompt, w, gn_w, mem_kv_prompt,
        jnp.zeros((batch, RET_H, RET_D, RET_D), F32), None, jnp.zeros((batch, 2 * S5_STATE), F32))

    past = cache_fox_k.shape[2]
    pos_sample = past + jnp.arange(dec_seq)
    mem_kv_sample = [(cache_mem_k[layer].reshape(dec_batch * mem_len, D_MODEL).astype(BF16),
                      cache_mem_v[layer].reshape(dec_batch * mem_len, D_MODEL).astype(BF16))
                     for layer in range(depth)]
    s5_st0 = jnp.concatenate([state_s5_re[0].reshape(dec_batch, S5_STATE),
                              state_s5_im[0].reshape(dec_batch, S5_STATE)], axis=1).astype(F32)
    y_sample, ns_ = _run_group(
        x_sample, pos_sample, w, gn_w, mem_kv_sample, cache_ret_state[0].astype(F32),
        (cache_fox_k[0], cache_fox_v[0], cache_fox_logf[0]), s5_st0)

    return (y_prompt, y_sample,
            np_["ret"], ns_["ret"],
            np_["fox_k"], np_["fox_v"], np_["fox_logf"],
            ns_["fox_k"], ns_["fox_v"], ns_["fox_logf"],
            np_["s5_re"], np_["s5_im"], ns_["s5_re"], ns_["s5_im"],
            mem_k.reshape(depth, batch, mem_len, X_H, X_D), mem_v.reshape(depth, batch, mem_len, X_H, X_D))
```

```python
import functools

import numpy as np
import jax
import jax.numpy as jnp
from jax import lax
from jax.experimental import pallas as pl
from jax.experimental.pallas import tpu as pltpu

F32 = jnp.float32
BF16 = jnp.bfloat16

D_MODEL = 1024
RET_H = 4
RET_D = 128
RET_W = RET_H * RET_D
CHUNK = 64
ROPE_BASE = 10000.0
FOX_H = 8
FOX_D = 64
FOX_W = FOX_H * FOX_D
S5_GROUP = 16
S5_G = D_MODEL // S5_GROUP
S5_P = 64
S5_STATE = S5_G * S5_P
X_H = 4
X_D = D_MODEL // X_H
D_FF = 4 * D_MODEL
EPS = 1e-6
NEG_INF = -1e30
LOG2E = float(np.log2(np.e))

LANES = 128
VMEM_LIMIT = 60 * 1024 * 1024

ROW_TILE = 512
POST_TILE = 1024
POST_PARTS = 2
RET_BLOCK = 256
FOX_TILE = 512
FOX_UNROLL = 4
FOX_DROP_LOG2 = 152.0
FOX_NORM_SLACK = 1.02
S5_TS = 64
S5_LANE_CHUNK = 512
CUMSUM_CHUNK = 256


def _dot(a, b):
    return jnp.dot(a, b, preferred_element_type=F32)


def _dot_nt(a, b):
    return lax.dot_general(a, b, (((1,), (1,)), ((), ())), preferred_element_type=F32)


def _dot_tn(a, b):
    return lax.dot_general(a, b, (((0,), (0,)), ((), ())), preferred_element_type=F32)


def _rms(x, w):
    return x * lax.rsqrt(jnp.mean(x * x, axis=-1, keepdims=True) + EPS) * w


def _const_spec(shape):
    nd = len(shape)
    return pl.BlockSpec(shape, lambda *_: (0,) * nd, pipeline_mode=pl.Buffered(1))


def _params(sem):
    return pltpu.CompilerParams(dimension_semantics=sem, vmem_limit_bytes=VMEM_LIMIT)


def _retention_block(qs, ks, vs, gs, dm_ref, wq_ref, wk_ref, dec_ref, gn_ref, s_ref):
    heads = range(RET_H)
    states = [s_ref[0, hh] for hh in heads]
    scores = [(_dot_nt(qs[hh], ks[hh]) * dm_ref[hh]).astype(BF16) for hh in heads]
    inter = [_dot(qs[hh], states[hh].astype(BF16)) * wq_ref[hh] for hh in heads]
    outs = [_dot(scores[hh], vs[hh]) + inter[hh] for hh in heads]
    kws = [(ks[hh].astype(F32) * wk_ref[hh]).astype(BF16) for hh in heads]
    for hh in heads:
        s_ref[0, hh] = states[hh] * dec_ref[hh] + _dot_tn(kws[hh], vs[hh])
    res = []
    for hh in heads:
        o = outs[hh]
        mu = jnp.mean(o, axis=-1, keepdims=True)
        oc = o - mu
        var = jnp.mean(oc * oc, axis=-1, keepdims=True)
        g = gs[hh]
        res.append((oc * lax.rsqrt(var + EPS) * gn_ref[hh] * (g * jax.nn.sigmoid(g))).astype(BF16))
    return res


def _retention_consts(tb, chunk):
    assert chunk <= tb and tb % chunk == 0
    lg = np.log(1.0 - 2.0 ** (-5.0 - np.arange(RET_H, dtype=np.float64)))
    idx = np.arange(tb, dtype=np.float64)
    visible = (idx[None, :] // chunk) <= (idx[:, None] // chunk)
    dm = np.exp(lg[:, None, None] * np.abs(idx[:, None] - idx[None, :])) * visible[None]
    wq = np.broadcast_to(np.exp(lg[:, None] * (idx[None, :] + 1.0))[:, :, None], (RET_H, tb, RET_D))
    wk = np.broadcast_to(np.exp(lg[:, None] * (tb - 1.0 - idx[None, :]))[:, :, None], (RET_H, tb, RET_D))
    dec = np.broadcast_to(np.exp(lg * tb)[:, None, None], (RET_H, 1, RET_D))
    return [jnp.asarray(a, F32) for a in (dm, wq, wk, dec)]


def _inproj_body(*refs, head_major, per_batch):
    x_ref, g_ref, wr_ref, wf_ref, wl_ref, bl_ref, cos_ref, sin_ref, hsum_ref = refs[0:9]
    if head_major:
        dm_ref, wq_ref, wk_ref, dec_ref, gn_ref, s0_ref = refs[9:15]
        ro_ref, s_ref = refs[15:17]
    else:
        rq_ref, rk_ref, rv_ref, rg_ref = refs[9:13]
    fq_ref, fkb_ref, fvp_ref, fk_ref, fv_ref, lf_ref, qn_ref, kn_ref = refs[-8:]

    def put_kv(dst_ref, z):
        if head_major:
            dst_ref[0] = z.T.reshape(FOX_H, FOX_D, z.shape[0])
        else:
            dst_ref[...] = z

    h = _rms(x_ref[...], g_ref[...]).astype(BF16)
    cos = cos_ref[...]
    sin = sin_ref[...]

    sls = [slice(hh * RET_D, (hh + 1) * RET_D) for hh in range(RET_H)]

    def rotary(z, scale):
        res = []
        for sl in sls:
            zz = z[:, sl]
            r = zz * cos + pltpu.roll(zz, RET_D // 2, 1) * sin
            res.append((r if scale is None else r * scale).astype(BF16))
        return res

    rqs = rotary(_dot(h, wr_ref[:, 0:RET_W]), None)
    rks = rotary(_dot(h, wr_ref[:, RET_W:2 * RET_W]), RET_D ** -0.5)
    zv = _dot(h, wr_ref[:, 2 * RET_W:3 * RET_W])
    zg = _dot(h, wr_ref[:, 3 * RET_W:4 * RET_W])
    def max_sq_norm(zb):
        return jnp.max(_dot(zb * zb, hsum_ref[...]), axis=0, keepdims=True)

    def project_q():
        fqb = (_dot(h, wf_ref[:, 0:FOX_W]) * (LOG2E * FOX_D ** -0.5)).astype(BF16)
        fq_ref[...] = fqb
        qn_ref[0] = max_sq_norm(fqb)

    def project_k():
        fk = _dot(h, wf_ref[:, FOX_W:2 * FOX_W])
        put_kv(fk_ref, fk)
        fkb = fk.astype(BF16)
        fkb_ref[...] = fkb
        kn_ref[0] = max_sq_norm(fkb)

    def project_v():
        fv = _dot(h, wf_ref[:, 2 * FOX_W:3 * FOX_W])
        put_kv(fv_ref, fv)
        lane = lax.broadcasted_iota(jnp.int32, (fv.shape[0], LANES), 1)
        for pr in range(FOX_H // 2):
            vp = fv[:, pr * LANES:(pr + 1) * LANES]
            fvp_ref[:, (2 * pr) * LANES:(2 * pr + 1) * LANES] = jnp.where(lane < FOX_D, vp, 1.0).astype(BF16)
            fvp_ref[:, (2 * pr + 1) * LANES:(2 * pr + 2) * LANES] = jnp.where(lane >= FOX_D, vp, 1.0).astype(BF16)

    def project_logf():
        zl = _dot(h, wl_ref[...]) + bl_ref[...]
        lf = jnp.minimum(zl, 0.0) - jnp.log1p(jnp.exp(-jnp.abs(zl)))
        if head_major:
            lf_ref[0] = lf.T[0:FOX_H, :]
        else:
            lf_ref[...] = lf[:, 0:FOX_H]

    projections = [project_q, project_k, project_v, project_logf]
    if head_major:
        @pl.when(pl.program_id(0) % per_batch == 0)
        def _():
            s_ref[...] = s0_ref[...]

        def retention_rows(r0):
            rows = slice(r0, r0 + tb)
            outs = _retention_block([a[rows] for a in rqs], [a[rows] for a in rks],
                                    [zv[rows, sl].astype(BF16) for sl in sls], [zg[rows, sl] for sl in sls],
                                    dm_ref, wq_ref, wk_ref, dec_ref, gn_ref, s_ref)
            for sl, o in zip(sls, outs):
                ro_ref[rows, sl] = o

        tb = dm_ref.shape[1]
        starts = list(range(0, h.shape[0], tb))
        per = -(-len(projections) // len(starts))
        for i, r0 in enumerate(starts):
            retention_rows(r0)
            for proj in projections[i * per:(i + 1) * per]:
                proj()
    else:
        for sl, a, b in zip(sls, rqs, rks):
            rq_ref[:, sl] = a
            rk_ref[:, sl] = b
        rv_ref[...] = zv.astype(BF16)
        rg_ref[...] = zg
        for proj in projections:
            proj()


def _inproj(x, g, wr, wf, wl, bl, cos_t, sin_t, tm, seq, gn, s0, chunk):
    n = x.shape[0]
    nt = cos_t.shape[0] // tm
    head_major = tm <= seq
    row = lambda w: pl.BlockSpec((tm, w), lambda i: (i, 0))
    tab = pl.BlockSpec((tm, LANES), lambda i: (i % nt, 0))
    shp = lambda w, dt: jax.ShapeDtypeStruct((n, w), dt)
    hsum = jnp.asarray(np.arange(FOX_W)[:, None] // FOX_D == np.arange(LANES)[None, :], BF16)
    stat_spec = pl.BlockSpec((1, 1, LANES), lambda i: (i, 0, 0))
    stat_shape = jax.ShapeDtypeStruct((n // tm, 1, LANES), F32)
    if head_major:
        per_batch = seq // tm
        kv_spec = pl.BlockSpec((1, FOX_H, FOX_D, tm), lambda i: (i // per_batch, 0, 0, i % per_batch))
        kv_shape = jax.ShapeDtypeStruct((n // seq, FOX_H, FOX_D, seq), F32)
        lf_spec = pl.BlockSpec((1, FOX_H, tm), lambda i: (i // per_batch, 0, i % per_batch))
        lf_shape = jax.ShapeDtypeStruct((n // seq, FOX_H, seq), F32)
        consts = _retention_consts(min(RET_BLOCK, tm), chunk)
        st_spec = pl.BlockSpec((1, RET_H, RET_D, RET_D), lambda i: (i // per_batch, 0, 0, 0))
        extra_in = consts + [gn, s0]
        extra_in_specs = [_const_spec(c.shape) for c in consts] + [_const_spec(gn.shape), st_spec]
        ret_specs = [row(RET_W), st_spec]
        ret_shapes = [shp(RET_W, BF16), jax.ShapeDtypeStruct(s0.shape, F32)]
    else:
        per_batch = 1
        kv_spec, kv_shape = row(FOX_W), shp(FOX_W, F32)
        lf_spec, lf_shape = row(FOX_H), shp(FOX_H, F32)
        extra_in, extra_in_specs = [], []
        ret_specs = [row(RET_W)] * 4
        ret_shapes = [shp(RET_W, BF16), shp(RET_W, BF16), shp(RET_W, BF16), shp(RET_W, F32)]
    return pl.pallas_call(
        functools.partial(_inproj_body, head_major=head_major, per_batch=per_batch),
        grid=(n // tm,),
        in_specs=[row(D_MODEL), _const_spec(g.shape), _const_spec(wr.shape), _const_spec(wf.shape),
                  _const_spec(wl.shape), _const_spec(bl.shape), tab, tab, _const_spec(hsum.shape)] + extra_in_specs,
        out_specs=ret_specs + [row(FOX_W), row(FOX_W), row(FOX_H * LANES), kv_spec, kv_spec, lf_spec,
                               stat_spec, stat_spec],
        out_shape=ret_shapes + [shp(FOX_W, BF16), shp(FOX_W, BF16), shp(FOX_H * LANES, BF16), kv_shape,
                                kv_shape, lf_shape, stat_shape, stat_shape],
        compiler_params=_params(("arbitrary",)),
        name="inproj_ab_retention" if head_major else "inproj_ab",
    )(x, g, wr, wf, wl, bl, cos_t, sin_t, hsum, *extra_in)


def _cumsum_body(x_ref, u_ref, o_ref):
    rows, total = x_ref.shape
    u = u_ref[...]

    def chunk(i, carry):
        off = pl.multiple_of(i * CUMSUM_CHUNK, CUMSUM_CHUNK)
        xc = x_ref[:, pl.ds(off, CUMSUM_CHUNK)]
        hi = xc.astype(BF16)
        r1 = xc - hi.astype(F32)
        mid = r1.astype(BF16)
        lo = (r1 - mid.astype(F32)).astype(BF16)
        y = _dot(hi, u) + _dot(mid, u) + _dot(lo, u) + carry
        o_ref[:, pl.ds(off, CUMSUM_CHUNK)] = y * LOG2E
        return y[:, CUMSUM_CHUNK - 1:CUMSUM_CHUNK]

    lax.fori_loop(0, total // CUMSUM_CHUNK, chunk, jnp.zeros((rows, 1), F32))


def _cumsum_lanes(x):
    idx = np.arange(CUMSUM_CHUNK)
    u = jnp.asarray(idx[:, None] <= idx[None, :], BF16)
    return pl.pallas_call(
        _cumsum_body,
        out_shape=jax.ShapeDtypeStruct(x.shape, F32),
        compiler_params=pltpu.CompilerParams(vmem_limit_bytes=VMEM_LIMIT),
        name="logf_cumsum",
    )(x, u)


def _ret_body(q_ref, k_ref, v_ref, rg_ref, dm_ref, wq_ref, wk_ref, dec_ref, gn_ref, s0_ref,
              ro_ref, s_ref):
    @pl.when(pl.program_id(1) == 0)
    def _():
        s_ref[...] = s0_ref[...]

    sls = [slice(hh * RET_D, (hh + 1) * RET_D) for hh in range(RET_H)]
    outs = _retention_block([q_ref[:, sl] for sl in sls], [k_ref[:, sl] for sl in sls],
                            [v_ref[:, sl] for sl in sls], [rg_ref[:, sl] for sl in sls],
                            dm_ref, wq_ref, wk_ref, dec_ref, gn_ref, s_ref)
    for sl, o in zip(sls, outs):
        ro_ref[:, sl] = o


def _retention(rq, rk, rv, rg, gn, s0, batch, seq, chunk):
    tb = min(RET_BLOCK, seq)
    nb = seq // tb
    consts = _retention_consts(tb, chunk)
    row = pl.BlockSpec((tb, RET_W), lambda b, t: (b * nb + t, 0))
    st = pl.BlockSpec((1, RET_H, RET_D, RET_D), lambda b, t: (b, 0, 0, 0))
    return pl.pallas_call(
        _ret_body,
        grid=(batch, nb),
        in_specs=[row, row, row, row] + [_const_spec(c.shape) for c in consts] + [_const_spec(gn.shape), st],
        out_specs=[row, st],
        out_shape=[jax.ShapeDtypeStruct((batch * seq, RET_W), BF16),
                   jax.ShapeDtypeStruct((batch, RET_H, RET_D, RET_D), F32)],
        compiler_params=_params(("parallel", "arbitrary")),
        name="retention",
    )(rq, rk, rv, rg, *consts, gn, s0)


def _fox_body(start_ref, q_ref, k_ref, v_ref, c_ref, mask_ref, o_ref, qs_scr, ql_scr, m_scr, acc_scr, *, tq, tk):
    n_full = pl.program_id(2)
    pair = pl.program_id(0) * pl.num_programs(1) + pl.program_id(1)
    first_a = start_ref[(2 * pair) * pl.num_programs(2) + n_full]
    first_b = start_ref[(2 * pair + 1) * pl.num_programs(2) + n_full]
    solo_first = jnp.minimum(first_a, first_b)
    first = jnp.maximum(first_a, first_b)
    solo_head = (first_b < first_a).astype(jnp.int32)
    q = q_ref[...]
    lane = lax.broadcasted_iota(jnp.int32, q.shape, 1)
    zero = jnp.zeros_like(q)
    qs_scr[0:tq, :] = jnp.where(lane < FOX_D, q, zero)
    qs_scr[tq:2 * tq, :] = jnp.where(lane >= FOX_D, q, zero)
    ql_scr[0:tq // 2, :] = qs_scr[tq // 2:tq, :]
    ql_scr[tq // 2:tq, :] = qs_scr[tq + tq // 2:2 * tq, :]
    m_scr[...] = jnp.full(m_scr.shape, -jnp.inf, F32)
    acc_scr[...] = jnp.zeros(acc_scr.shape, F32)

    def stacked_tile(q_scr, q_rows, state_rows, off, width, mask):
        s = _dot_nt(q_scr[...], k_ref[0, pl.ds(off, width), :])
        shs = [s[q_rows[hh], :] - c_ref[0, 0, hh:hh + 1, pl.ds(off, width)] for hh in range(2)]
        if mask is not None:
            shs = [sh + mask for sh in shs]
        chunks = [[sh[:, c * LANES:(c + 1) * LANES] for c in range(width // LANES)] for sh in shs]
        m_prevs = [m_scr[rows, :] for rows in state_rows]
        m_news = [jnp.maximum(m_prevs[hh], jnp.max(functools.reduce(jnp.maximum, chunks[hh]), axis=1, keepdims=True))
                  for hh in range(2)]
        ps = [jnp.concatenate([jnp.exp2(ch - m_news[hh]) for ch in chunks[hh]], axis=1).astype(BF16)
              for hh in range(2)]
        pvs = [_dot(ps[hh], v_ref[0, pl.ds(off, width), hh * LANES:(hh + 1) * LANES]) for hh in range(2)]
        for hh in range(2):
            rows = state_rows[hh]
            acc_scr[rows, :] = jnp.exp2(m_prevs[hh] - m_news[hh]) * acc_scr[rows, :] + pvs[hh]
            m_scr[rows, :] = m_news[hh]

    head_rows = [slice(hh * tq, (hh + 1) * tq) for hh in range(2)]

    def tile(kj):
        stacked_tile(qs_scr, head_rows, head_rows, pl.multiple_of(kj * tk, tk), tk, None)

    def diagonal_tile():
        hq = tq // 2
        off = pl.multiple_of(n_full * tk, tk)
        stacked_tile(qs_scr, head_rows, head_rows, off, hq, mask_ref[...])
        lower = [slice(hh * tq + hq, (hh + 1) * tq) for hh in range(2)]
        stacked_tile(ql_scr, [slice(hh * hq, (hh + 1) * hq) for hh in range(2)], lower,
                     pl.multiple_of(off + hq, hq), hq, mask_ref[0:hq, :])

    def solo_tile(kj):
        off = pl.multiple_of(kj * tk, tk)
        rows = pl.ds(pl.multiple_of(solo_head * tq, tq), tq)
        sh = (_dot_nt(qs_scr[rows, :], k_ref[0, pl.ds(off, tk), :])
              - c_ref[0, 0, pl.ds(solo_head, 1), pl.ds(off, tk)])
        chunks = [sh[:, c * LANES:(c + 1) * LANES] for c in range(tk // LANES)]
        m_prev = m_scr[rows, :]
        m_new = jnp.maximum(m_prev, jnp.max(functools.reduce(jnp.maximum, chunks), axis=1, keepdims=True))
        p = jnp.concatenate([jnp.exp2(ch - m_new) for ch in chunks], axis=1).astype(BF16)
        pv = _dot(p, v_ref[0, pl.ds(off, tk), :])
        pv = jnp.where(solo_head == 0, pv[:, 0:LANES], pv[:, LANES:2 * LANES])
        acc_scr[rows, :] = jnp.exp2(m_prev - m_new) * acc_scr[rows, :] + pv
        m_scr[rows, :] = m_new

    def run_tiles(start, count, visit):
        def group(i, carry):
            for j in range(FOX_UNROLL):
                visit(start + FOX_UNROLL * i + j)
            return carry

        lax.fori_loop(0, count // FOX_UNROLL, group, 0)
        span = FOX_UNROLL // 2
        while span >= 1:
            @pl.when((count // span) % 2 == 1)
            def _(span=span):
                base = start + (count // (2 * span)) * (2 * span)
                for j in range(span):
                    visit(base + j)
            span //= 2

    run_tiles(solo_first, first - solo_first, solo_tile)
    run_tiles(first, n_full - first, tile)
    diagonal_tile()

    acc_a = acc_scr[0:tq, :]
    acc_b = acc_scr[tq:2 * tq, :]
    half = LANES // 2
    o_ref[...] = jnp.where(lane < FOX_D, acc_a / pltpu.roll(acc_a, half, 1),
                           acc_b / pltpu.roll(acc_b, half, 1)).astype(BF16)


def _fox_first_tiles(qn2, kn2, c_all, batch, seq, tile_len):
    nq = seq // tile_len
    qn = jnp.sqrt(qn2[:, 0, 0:FOX_H]).reshape(batch, nq, FOX_H).transpose(0, 2, 1)
    kn = jnp.sqrt(kn2[:, 0, 0:FOX_H]).reshape(batch, nq, FOX_H).transpose(0, 2, 1)
    c = c_all[..., 0:seq].reshape(batch, FOX_H, nq, tile_len)
    c_first, c_last = c[..., 0], c[..., tile_len - 1]
    dots = FOX_NORM_SLACK * (qn[..., :, None] * kn[..., None, :] + (qn * kn)[..., :, None])
    gap = dots + c_first[..., :, None] - c_last[..., None, :]
    before = jnp.arange(nq)[None, :] < jnp.arange(nq)[:, None]
    droppable = (gap < -FOX_DROP_LOG2) & before
    first = jnp.sum(jnp.cumprod(droppable.astype(jnp.int32), axis=-1), axis=-1)
    return first.reshape(-1).astype(jnp.int32)


def _fox(first_tile, fq, k_all, v_all, c_all, batch, seq, tile_len):
    tq = tk = tile_len
    nq = seq // tq
    half = tile_len // 2
    mask = jnp.asarray(np.where(np.arange(half)[None, :] <= np.arange(tile_len)[:, None], 0.0, NEG_INF), F32)
    body = functools.partial(_fox_body, tq=tq, tk=tk)
    grid_spec = pltpu.PrefetchScalarGridSpec(
        num_scalar_prefetch=1,
        grid=(batch, FOX_H // 2, nq),
        in_specs=[pl.BlockSpec((tq, LANES), lambda b, h, i, _: (b * nq + i, h)),
                  pl.BlockSpec((1, seq, LANES), lambda b, h, i, _: (b, 0, h)),
                  pl.BlockSpec((1, seq, 2 * LANES), lambda b, h, i, _: (b, 0, h)),
                  pl.BlockSpec((1, 1, 2, seq), lambda b, h, i, _: (b, h, 0, 0)),
                  _const_spec(mask.shape)],
        out_specs=pl.BlockSpec((tq, LANES), lambda b, h, i, _: (b * nq + i, h)),
        scratch_shapes=[pltpu.VMEM((2 * tq, LANES), BF16), pltpu.VMEM((tq, LANES), BF16),
                        pltpu.VMEM((2 * tq, LANES), F32), pltpu.VMEM((2 * tq, LANES), F32)])
    return pl.pallas_call(
        body,
        grid_spec=grid_spec,
        out_shape=jax.ShapeDtypeStruct((batch * seq, FOX_W), BF16),
        compiler_params=_params(("parallel", "parallel", "arbitrary")),
        name="fox_attention",
    )(first_tile, fq, k_all, v_all, c_all, mask)


def _fox_cached_body(q_ref, kp_ref, vp_ref, kn_ref, vn_ref, c_ref, o_ref, *, past, lq):
    q = q_ref[...]
    lane = lax.broadcasted_iota(jnp.int32, q.shape, 1)
    zero = jnp.zeros_like(q)
    qs = jnp.concatenate([jnp.where(lane < FOX_D, q, zero), jnp.where(lane >= FOX_D, q, zero)], axis=0)
    head_rows = lambda a, b, width: jnp.concatenate(
        [jnp.broadcast_to(a, (lq, width)), jnp.broadcast_to(b, (lq, width))], axis=0)
    s_past = _dot(qs, kp_ref[0].astype(BF16)) - head_rows(c_ref[0, 0, 0:1, 0:past], c_ref[0, 0, 1:2, 0:past], past)
    s_new = _dot_nt(qs, kn_ref[...]) - head_rows(c_ref[0, 0, 0:1, past:past + lq],
                                                 c_ref[0, 0, 1:2, past:past + lq], lq)
    qi = lax.broadcasted_iota(jnp.int32, s_new.shape, 0)
    ki = lax.broadcasted_iota(jnp.int32, s_new.shape, 1)
    s_new = jnp.where(ki <= jnp.where(qi >= lq, qi - lq, qi), s_new, NEG_INF)
    m = jnp.maximum(jnp.max(s_past, axis=1, keepdims=True), jnp.max(s_new, axis=1, keepdims=True))
    p_past = jnp.exp2(s_past - m)
    p_new = jnp.exp2(s_new - m)
    denom = jnp.sum(p_past, axis=1, keepdims=True) + jnp.sum(p_new, axis=1, keepdims=True)
    pb_new = p_new.astype(BF16)
    acc = _dot_nt(p_past.astype(BF16), vp_ref[0].astype(BF16))
    acc_a = (acc[0:lq] + _dot(pb_new[0:lq], vn_ref[:, 0:LANES])) / denom[0:lq]
    acc_b = (acc[lq:2 * lq] + _dot(pb_new[lq:2 * lq], vn_ref[:, LANES:2 * LANES])) / denom[lq:2 * lq]
    o_ref[...] = jnp.where(lane < FOX_D, acc_a, acc_b).astype(BF16)


def _fox_cached(fq, k_past, v_past, k_new, v_new, c_all, batch, lq):
    past = k_past.shape[2]
    new_rows = lambda w: pl.BlockSpec((lq, w), lambda b, h: (b, h))
    cache = pl.BlockSpec((1, LANES, past), lambda b, h: (b, h, 0))
    return pl.pallas_call(
        functools.partial(_fox_cached_body, past=past, lq=lq),
        grid=(batch, FOX_H // 2),
        in_specs=[new_rows(LANES), cache, cache, new_rows(LANES), new_rows(2 * LANES),
                  pl.BlockSpec((1, 1, 2, c_all.shape[-1]), lambda b, h: (b, h, 0, 0))],
        out_specs=new_rows(LANES),
        out_shape=jax.ShapeDtypeStruct((batch * lq, FOX_W), BF16),
        compiler_params=_params(("parallel", "parallel")),
        name="fox_attention_cached",
    )(fq, k_past, v_past, k_new, v_new, c_all)


def _memory_kv_body(x_ref, g_ref, w_ref, k_ref, v_ref, kb_ref, vb_ref):
    y = _dot(_rms(x_ref[...], g_ref[0]).astype(BF16), w_ref[0])
    k = y[:, 0:D_MODEL]
    v = y[:, D_MODEL:2 * D_MODEL]
    k_ref[0] = k
    v_ref[0] = v
    kb_ref[0] = k.astype(BF16)
    vb_ref[0] = v.astype(BF16)


def _memory_kv(x, g, w, tm):
    n = x.shape[0]
    layers = w.shape[0]
    out = pl.BlockSpec((1, tm, D_MODEL), lambda l, i: (l, i, 0))
    shape = lambda dt: jax.ShapeDtypeStruct((layers, n, D_MODEL), dt)
    return pl.pallas_call(
        _memory_kv_body,
        grid=(layers, n // tm),
        in_specs=[pl.BlockSpec((tm, D_MODEL), lambda l, i: (i, 0)),
                  pl.BlockSpec((1, 1, D_MODEL), lambda l, i: (l, 0, 0)),
                  pl.BlockSpec((1, D_MODEL, 2 * D_MODEL), lambda l, i: (l, 0, 0))],
        out_specs=[out] * 4,
        out_shape=[shape(F32), shape(F32), shape(BF16), shape(BF16)],
        compiler_params=_params(("parallel", "parallel")),
        name="memory_kv",
    )(x, g, w)


def _post_body(*refs, has_mix, parts, seq_rows, mem_len):
    if has_mix:
        (x_ref, ro_ref, fo_ref, wo_ref, nw_ref, wxq_ref, wxo_ref, mk_ref, mv_ref, wup_ref, wdn_ref,
         o_ref) = refs
    else:
        x_ref, nw_ref, wxq_ref, wxo_ref, mk_ref, mv_ref, wup_ref, wdn_ref, o_ref = refs
    rows = [slice(r0, r0 + nrows) for r0, nrows in parts]
    xs = [x_ref[rs, :] for rs in rows]
    if has_mix:
        mixes = [_dot(ro_ref[rs, :], wo_ref[0:RET_W, :]) + _dot(fo_ref[rs, :], wo_ref[RET_W:RET_W + FOX_W, :])
                 for rs in rows]
        xs = [x + _rms(mix, nw_ref[1:2, :]) for x, mix in zip(xs, mixes)]

    qs = [_dot(_rms(x, nw_ref[2:3, :]).astype(BF16), wxq_ref[...]) for x in xs]
    ctx = []
    for (r0, nrows), q in zip(parts, qs):
        groups = []
        for g0 in range(0, nrows, min(seq_rows, nrows)):
            gs = slice(g0, g0 + min(seq_rows, nrows))
            m0 = ((r0 + g0) // seq_rows) * mem_len
            heads = []
            for hh in range(X_H):
                sl = slice(hh * X_D, (hh + 1) * X_D)
                s = _dot_nt(q[gs, sl].astype(BF16), mk_ref[m0:m0 + mem_len, sl]) * (X_D ** -0.5)
                p = jnp.exp(s - jnp.max(s, axis=-1, keepdims=True))
                o = _dot(p.astype(BF16), mv_ref[m0:m0 + mem_len, sl]) / jnp.sum(p, axis=-1, keepdims=True)
                heads.append(o.astype(BF16))
            groups.append(jnp.concatenate(heads, axis=1))
        ctx.append(groups[0] if len(groups) == 1 else jnp.concatenate(groups, axis=0))
    atts = [_dot(c, wxo_ref[...]) for c in ctx]
    xs = [x + _rms(att, nw_ref[3:4, :]) for x, att in zip(xs, atts)]

    hs = [_rms(x, nw_ref[4:5, :]).astype(BF16) for x in xs]
    accs = [None] * len(parts)
    for c in range(D_FF // D_MODEL):
        sl = slice(c * D_MODEL, (c + 1) * D_MODEL)
        for i, h in enumerate(hs):
            r = jnp.maximum(_dot(h, wup_ref[:, sl]), 0.0)
            part = _dot((r * r).astype(BF16), wdn_ref[sl, :])
            accs[i] = part if accs[i] is None else accs[i] + part
    for rs, x, acc in zip(rows, xs, accs):
        o_ref[rs, :] = x + _rms(acc, nw_ref[5:6, :])


def _post(x, mix_in, nw, wxq, wxo, mk, mv, wup, wdn, seq, tm):
    n = x.shape[0]
    mem_len = mk.shape[0] // (n // seq)
    row = lambda w: pl.BlockSpec((tm, w), lambda i: (i, 0))
    if tm <= seq:
        per_batch = seq // tm
        mem = pl.BlockSpec((mem_len, D_MODEL), lambda i: (i // per_batch, 0))
        seq_rows = tm
        nparts = POST_PARTS if tm % (POST_PARTS * 16) == 0 else 1
        parts = tuple((p * (tm // nparts), tm // nparts) for p in range(nparts))
    else:
        mem = pl.BlockSpec(((tm // seq) * mem_len, D_MODEL), lambda i: (i, 0))
        seq_rows = seq
        parts = ((0, tm),)
    args = [x]
    specs = [row(D_MODEL)]
    if mix_in is not None:
        ro, fo, wo = mix_in
        args += [ro, fo, wo]
        specs += [row(RET_W), row(FOX_W), _const_spec(wo.shape)]
    args += [nw, wxq, wxo, mk, mv, wup, wdn]
    specs += [_const_spec(nw.shape), _const_spec(wxq.shape), _const_spec(wxo.shape), mem, mem,
              _const_spec(wup.shape), _const_spec(wdn.shape)]
    return pl.pallas_call(
        functools.partial(_post_body, has_mix=mix_in is not None, parts=parts, seq_rows=seq_rows, mem_len=mem_len),
        grid=(n // tm,),
        in_specs=specs,
        out_specs=row(D_MODEL),
        out_shape=jax.ShapeDtypeStruct((n, D_MODEL), F32),
        compiler_params=_params(("parallel",)),
        name="xattn_mlp_mix" if mix_in is not None else "xattn_mlp",
    )(*args)


def _s5prep_body(lre_ref, lim_ref, ldt_ref, lre_rep_ref, lim_rep_ref, bre_ref, bim_ref,
                 are_ref, aim_ref, bbre_ref, bbim_ref):
    dt = jnp.exp(ldt_ref[...])

    def zoh(lre, lim):
        mag = jnp.exp(lre * dt)
        a_re = mag * jnp.cos(lim * dt)
        a_im = mag * jnp.sin(lim * dt)
        den = lre * lre + lim * lim
        num_re = a_re - 1.0
        f_re = (num_re * lre + a_im * lim) / den
        f_im = (a_im * lre - num_re * lim) / den
        return a_re, a_im, f_re, f_im

    a_re, a_im, _, _ = zoh(lre_ref[...], lim_ref[...])
    are_ref[...] = a_re
    aim_ref[...] = a_im
    _, _, f_re, f_im = zoh(lre_rep_ref[...], lim_rep_ref[...])
    b_re = bre_ref[...]
    b_im = bim_ref[...]
    bbre_ref[...] = f_re * b_re - f_im * b_im
    bbim_ref[...] = f_re * b_im + f_im * b_re


def _s5_discretize(lam_re, lam_im, log_dt, b_re, b_im):
    rep = lambda a: jnp.repeat(a, S5_GROUP, axis=1)
    wide = jax.ShapeDtypeStruct((S5_G, S5_P * S5_GROUP), F32)
    small = jax.ShapeDtypeStruct((S5_G, S5_P), F32)
    a_re, a_im, bb_re, bb_im = pl.pallas_call(
        _s5prep_body, out_shape=[small, small, wide, wide], name="s5_discretize",
    )(lam_re, lam_im, log_dt.reshape(S5_G, 1), rep(lam_re), rep(lam_im),
      b_re.reshape(S5_G, -1), b_im.reshape(S5_G, -1))
    a = jnp.stack([a_re.reshape(-1), a_im.reshape(-1)])
    gpc = S5_LANE_CHUNK // S5_P
    nch = S5_G // gpc
    eye = jnp.eye(gpc, dtype=F32)

    def in_blocks(bb):
        t = bb.reshape(nch, gpc, S5_P, S5_GROUP).transpose(0, 1, 3, 2)
        return (t[:, :, :, None, :] * eye[None, :, None, :, None]).reshape(nch, gpc * S5_GROUP, gpc * S5_P)

    bmat = jnp.concatenate([in_blocks(bb_re), in_blocks(bb_im)], axis=-1).astype(BF16)
    return a, bmat


def _s5_out_matrices(c_re, c_im):
    gpc = S5_LANE_CHUNK // S5_P
    nch = S5_G // gpc
    eye = jnp.eye(gpc, dtype=F32)

    def out_blocks(c):
        t = c.reshape(nch, gpc, S5_GROUP, S5_P).transpose(0, 1, 3, 2)
        return (t[:, :, :, None, :] * eye[None, :, None, :, None]).reshape(nch, gpc * S5_P, gpc * S5_GROUP)

    return jnp.concatenate([out_blocks(c_re), -out_blocks(c_im)], axis=1).astype(BF16)


def _s5_body(x_ref, nw_ref, perm_ref, win_ref, bmat_ref, a_ref, cmat_ref, d_ref, wglu_ref, st0_ref,
             o_ref, st_ref, u_scr, bu_scr, y_scr, *, ts):
    nb = x_ref.shape[0]
    rows = nb * ts
    lc = S5_LANE_CHUNK
    gw = (lc // S5_P) * S5_GROUP

    @pl.when(pl.program_id(0) == 0)
    def _():
        st_ref[...] = st0_ref[...]

    x = x_ref[...].reshape(rows, D_MODEL)
    h = _dot(perm_ref[0], _rms(x, nw_ref[0:1, :]).astype(BF16)).astype(BF16)
    u_scr[...] = _dot(h, win_ref[...])
    nv = lc // LANES
    for j in range(S5_STATE // lc):
        uj = u_scr[:, j * gw:(j + 1) * gw]
        bu = _dot(uj.astype(BF16), bmat_ref[j])
        for c in range(2 * nv):
            bu_scr[c] = bu[:, c * LANES:(c + 1) * LANES]
        a_re = [jnp.broadcast_to(a_ref[0:1, j * lc + c * LANES:j * lc + (c + 1) * LANES], (nb, LANES))
                for c in range(nv)]
        a_im = [jnp.broadcast_to(a_ref[1:2, j * lc + c * LANES:j * lc + (c + 1) * LANES], (nb, LANES))
                for c in range(nv)]

        def step(t, carry):
            idx = pl.ds(pl.multiple_of(t * nb, nb), nb)
            new = []
            for c in range(nv):
                s_re, s_im = carry[c], carry[nv + c]
                new.append(a_re[c] * s_re - a_im[c] * s_im + bu_scr[c, idx, :])
            for c in range(nv):
                s_re, s_im = carry[c], carry[nv + c]
                new.append(a_re[c] * s_im + a_im[c] * s_re + bu_scr[nv + c, idx, :])
            for c in range(2 * nv):
                bu_scr[c, idx, :] = new[c]
            return tuple(new)

        init = tuple(st_ref[:, j * lc + c * LANES:j * lc + (c + 1) * LANES] for c in range(nv)) + tuple(
            st_ref[:, S5_STATE + j * lc + c * LANES:S5_STATE + j * lc + (c + 1) * LANES] for c in range(nv))
        fin = lax.fori_loop(0, ts, step, init, unroll=True)
        for c in range(nv):
            st_ref[:, j * lc + c * LANES:j * lc + (c + 1) * LANES] = fin[c]
            st_ref[:, S5_STATE + j * lc + c * LANES:S5_STATE + j * lc + (c + 1) * LANES] = fin[nv + c]
        xs = jnp.concatenate([bu_scr[c] for c in range(2 * nv)], axis=1).astype(BF16)
        y_scr[:, j * gw:(j + 1) * gw] = _dot(xs, cmat_ref[j]) + d_ref[0:1, j * gw:(j + 1) * gw] * uj
    y = _dot(perm_ref[1], y_scr[...].astype(BF16)).astype(BF16)
    g = _dot(y, wglu_ref[...])
    out = g[:, 0:D_MODEL] * jax.nn.sigmoid(g[:, D_MODEL:2 * D_MODEL])
    o_ref[...] = (x + _rms(out, nw_ref[1:2, :])).reshape(nb, ts, D_MODEL)


def _s5_layer(x3, nw, win, bmat, a, cmat, d, wglu, st0, ts):
    nb, seq, _ = x3.shape
    rows = nb * ts
    blk = pl.BlockSpec((nb, ts, D_MODEL), lambda t: (0, t, 0))
    r = np.arange(rows)
    fwd = np.zeros((rows, rows), np.float32)
    fwd[(r % ts) * nb + r // ts, r] = 1.0
    perm = jnp.asarray(np.stack([fwd, fwd.T]), BF16)
    consts = [nw, perm, win, bmat, a, cmat, d, wglu, st0]
    return pl.pallas_call(
        functools.partial(_s5_body, ts=ts),
        grid=(seq // ts,),
        in_specs=[blk] + [_const_spec(c.shape) for c in consts],
        out_specs=[blk, pl.BlockSpec(st0.shape, lambda t: (0, 0))],
        out_shape=[jax.ShapeDtypeStruct(x3.shape, F32), jax.ShapeDtypeStruct(st0.shape, F32)],
        scratch_shapes=[pltpu.VMEM((rows, D_MODEL), F32),
                        pltpu.VMEM((2 * S5_LANE_CHUNK // LANES, rows, LANES), F32),
                        pltpu.VMEM((rows, D_MODEL), F32)],
        compiler_params=_params(("arbitrary",)),
        name="s5_layer",
    )(x3, *consts)


def _rotary_tables(pos, rows):
    half = RET_D // 2
    inv = ROPE_BASE ** (-jnp.arange(half, dtype=F32) / half)
    ang = pos.astype(F32)[:, None] * inv[None, :]
    cos = jnp.cos(ang)
    sin = jnp.sin(ang)
    cos_t = jnp.concatenate([cos, cos], axis=1)
    sin_t = jnp.concatenate([-sin, sin], axis=1)
    reps = max(1, rows // pos.shape[0])
    return jnp.tile(cos_t, (reps, 1)), jnp.tile(sin_t, (reps, 1))


def _prepare_weights(norm_w, w_in_ab, b_fox_f, w_out_ab, w_in_c, s5_lambda_re, s5_lambda_im, s5_log_dt,
                     s5_b_re, s5_b_im, s5_c_re, s5_c_im, s5_d, w_glu, w_xq, w_xo, w_up, w_down):
    w = {}
    wab = w_in_ab[0].astype(BF16)
    w["wr"] = wab[:, 0:4 * RET_W]
    w["wf"] = wab[:, 4 * RET_W:4 * RET_W + 3 * FOX_W]
    w["wl"] = jnp.pad(wab[:, 4 * RET_W + 3 * FOX_W:], ((0, 0), (0, LANES - FOX_H)))
    w["bl"] = jnp.pad(b_fox_f[0].astype(F32), (0, LANES - FOX_H)).reshape(1, LANES)
    w["wo"] = w_out_ab[0].astype(BF16)
    w["nw"] = norm_w.astype(F32)
    w["win_c"] = w_in_c[0].astype(BF16)
    w["a"], w["bmat"] = _s5_discretize(s5_lambda_re[0], s5_lambda_im[0], s5_log_dt[0], s5_b_re[0], s5_b_im[0])
    w["cmat"] = _s5_out_matrices(s5_c_re[0].astype(F32), s5_c_im[0].astype(F32))
    w["d"] = s5_d[0].astype(F32).reshape(1, D_MODEL)
    w["wglu"] = w_glu[0].astype(BF16)
    for name, arr in (("wxq", w_xq), ("wxo", w_xo), ("wup", w_up), ("wdn", w_down)):
        w[name] = arr.astype(BF16)
    return w


def _run_group(x3, pos, w, gn_w, mem_kv, ret_s0, fox_past, s5_st0):
    batch, seq, _ = x3.shape
    n = batch * seq
    tm = min(ROW_TILE, n)
    post_tile = lambda t: t if seq >= t else min(n, (t // seq) * seq)
    x = x3.reshape(n, D_MODEL)

    cos_t, sin_t = _rotary_tables(pos, tm)
    chunk = CHUNK if fox_past is None else seq
    gn = gn_w.reshape(RET_H, 1, RET_D)
    outs = _inproj(x, w["nw"][0, 0:1], w["wr"], w["wf"], w["wl"], w["bl"], cos_t, sin_t, tm, seq, gn, ret_s0, chunk)
    fq, fkb, fvp, fk, fv, lf, qn2, kn2 = outs[-8:]
    if tm <= seq:
        ro, ret_state = outs[0:2]
        lf_t = lf
        fk = jnp.transpose(fk, (0, 3, 1, 2))
        fv = jnp.transpose(fv, (0, 3, 1, 2))
        lf3 = jnp.transpose(lf, (0, 2, 1))
    else:
        lf3 = lf.reshape(batch, seq, FOX_H)
        lf_t = jnp.transpose(lf3, (0, 2, 1))
    if fox_past is not None:
        lf_t = jnp.concatenate([jnp.transpose(fox_past[2].astype(F32), (0, 2, 1)), lf_t], axis=2)
    lk = lf_t.shape[2]
    pad = -lk % CUMSUM_CHUNK
    lf_rows = jnp.pad(lf_t.reshape(batch * FOX_H, lk), ((0, 0), (0, pad)))
    c_all = _cumsum_lanes(lf_rows).reshape(batch, FOX_H // 2, 2, lk + pad)
    if fox_past is None:
        assert tm == FOX_TILE
        first_tile = _fox_first_tiles(qn2, kn2, c_all, batch, seq, FOX_TILE)
        fo = _fox(first_tile, fq, fkb.reshape(batch, seq, FOX_W), fvp.reshape(batch, seq, FOX_H * LANES), c_all,
                  batch, seq, FOX_TILE)
    else:
        past = fox_past[0].shape[1]
        cache_t = lambda a: jnp.transpose(a, (0, 2, 3, 1)).reshape(batch, FOX_W, past)
        fo = _fox_cached(fq, cache_t(fox_past[0]), cache_t(fox_past[1]), fkb, fvp, c_all, batch, seq)

    if tm > seq:
        ro, ret_state = _retention(*outs[0:4], gn, ret_s0, batch, seq, chunk)

    x = _post(x, (ro, fo, w["wo"]), w["nw"][0], w["wxq"][0], w["wxo"][0], mem_kv[0][0], mem_kv[0][1],
              w["wup"][0], w["wdn"][0], seq, post_tile(POST_TILE))

    ts = min(S5_TS, seq)
    x3n, s5_state = _s5_layer(x.reshape(batch, seq, D_MODEL), w["nw"][1], w["win_c"], w["bmat"], w["a"],
                              w["cmat"], w["d"], w["wglu"], s5_st0, ts)
    x = _post(x3n.reshape(n, D_MODEL), None, w["nw"][1], w["wxq"][1], w["wxo"][1], mem_kv[1][0], mem_kv[1][1],
              w["wup"][1], w["wdn"][1], seq, post_tile(POST_TILE))

    new = dict(ret=ret_state[None], fox_k=fk.reshape(1, batch, seq, FOX_H, FOX_D),
               fox_v=fv.reshape(1, batch, seq, FOX_H, FOX_D), fox_logf=lf3[None],
               s5_re=s5_state[:, 0:S5_STATE].reshape(1, batch, S5_G, S5_P),
               s5_im=s5_state[:, S5_STATE:].reshape(1, batch, S5_G, S5_P))
    return x.reshape(batch, seq, D_MODEL), new


def kernel(x_prompt, x_sample, cache_ret_state, cache_fox_k, cache_fox_v, cache_fox_logf, state_s5_re, state_s5_im, cache_mem_k, cache_mem_v, mem_prompt, norm_w, w_in_ab, b_fox_f, ret_gn_w, w_out_ab, w_in_c, s5_lambda_re, s5_lambda_im, s5_log_dt, s5_b_re, s5_b_im, s5_c_re, s5_c_im, s5_d, w_glu, mem_norm_w, w_xq, w_xk, w_xv, w_xo, w_up, w_down):
    w = _prepare_weights(norm_w, w_in_ab, b_fox_f, w_out_ab, w_in_c, s5_lambda_re, s5_lambda_im, s5_log_dt,
                         s5_b_re, s5_b_im, s5_c_re, s5_c_im, s5_d, w_glu, w_xq, w_xo, w_up, w_down)
    batch, seq, _ = x_prompt.shape
    dec_batch, dec_seq, _ = x_sample.shape
    mem_len = mem_prompt.shape[1]
    depth = norm_w.shape[0]
    gn_w = ret_gn_w[0].astype(F32)

    mem_rows = mem_prompt.reshape(batch * mem_len, D_MODEL)
    wkv = jnp.concatenate([w_xk, w_xv], axis=2).astype(BF16)
    mem_k, mem_v, mem_kb, mem_vb = _memory_kv(mem_rows, mem_norm_w.astype(F32).reshape(depth, 1, D_MODEL), wkv,
                                              min(ROW_TILE, batch * mem_len))
    mem_kv_prompt = [(mem_kb[layer], mem_vb[layer]) for layer in range(depth)]

    pos_prompt = jnp.arange(seq)
    y_prompt, np_ = _run_group(
        x_prompt, pos_prompt, w, gn_w, mem_kv_prompt,
        jnp.zeros((batch, RET_H, RET_D, RET_D), F32), None, jnp.zeros((batch, 2 * S5_STATE), F32))

    past = cache_fox_k.shape[2]
    pos_sample = past + jnp.arange(dec_seq)
    mem_kv_sample = [(cache_mem_k[layer].reshape(dec_batch * mem_len, D_MODEL).astype(BF16),
                      cache_mem_v[layer].reshape(dec_batch * mem_len, D_MODEL).astype(BF16))
                     for layer in range(depth)]
    s5_st0 = jnp.concatenate([state_s5_re[0].reshape(dec_batch, S5_STATE),
                              state_s5_im[0].reshape(dec_batch, S5_STATE)], axis=1).astype(F32)
    y_sample, ns_ = _run_group(
        x_sample, pos_sample, w, gn_w, mem_kv_sample, cache_ret_state[0].astype(F32),
        (cache_fox_k[0], cache_fox_v[0], cache_fox_logf[0]), s5_st0)

    return (y_prompt, y_sample,
            np_["ret"], ns_["ret"],
            np_["fox_k"], np_["fox_v"], np_["fox_logf"],
            ns_["fox_k"], ns_["fox_v"], ns_["fox_logf"],
            np_["s5_re"], np_["s5_im"], ns_["s5_re"], ns_["s5_im"],
            mem_k.reshape(depth, batch, mem_len, X_H, X_D), mem_v.reshape(depth, batch, mem_len, X_H, X_D))
```

```python
import functools

import numpy as np
import jax
import jax.numpy as jnp
from jax import lax
from jax.experimental import pallas as pl
from jax.experimental.pallas import tpu as pltpu

F32 = jnp.float32
BF16 = jnp.bfloat16

D_MODEL = 1024
RET_H = 4
RET_D = 128
RET_W = RET_H * RET_D
CHUNK = 64
ROPE_BASE = 10000.0
FOX_H = 8
FOX_D = 64
FOX_W = FOX_H * FOX_D
S5_GROUP = 16
S5_G = D_MODEL // S5_GROUP
S5_P = 64
S5_STATE = S5_G * S5_P
X_H = 4
X_D = D_MODEL // X_H
D_FF = 4 * D_MODEL
EPS = 1e-6
NEG_INF = -1e30
LOG2E = float(np.log2(np.e))

LANES = 128
VMEM_LIMIT = 60 * 1024 * 1024

ROW_TILE = 512
POST_TILE = 1024
POST_PARTS = 2
RET_BLOCK = 256
FOX_TILE = 512
FOX_UNROLL = 4
FOX_DROP_LOG2 = 152.0
FOX_NORM_SLACK = 1.02
S5_TS = 64
S5_LANE_CHUNK = 512
CUMSUM_CHUNK = 256


def _dot(a, b):
    return jnp.dot(a, b, preferred_element_type=F32)


def _dot_nt(a, b):
    return lax.dot_general(a, b, (((1,), (1,)), ((), ())), preferred_element_type=F32)


def _dot_tn(a, b):
    return lax.dot_general(a, b, (((0,), (0,)), ((), ())), preferred_element_type=F32)


def _rms(x, w):
    return x * lax.rsqrt(jnp.mean(x * x, axis=-1, keepdims=True) + EPS) * w


def _const_spec(shape):
    nd = len(shape)
    return pl.BlockSpec(shape, lambda *_: (0,) * nd, pipeline_mode=pl.Buffered(1))


def _params(sem):
    return pltpu.CompilerParams(dimension_semantics=sem, vmem_limit_bytes=VMEM_LIMIT)


def _retention_block(qs, ks, vs, gs, dm_ref, wq_ref, wk_ref, dec_ref, gn_ref, s_ref):
    heads = range(RET_H)
    states = [s_ref[0, hh] for hh in heads]
    scores = [(_dot_nt(qs[hh], ks[hh]) * dm_ref[hh]).astype(BF16) for hh in heads]
    inter = [_dot(qs[hh], states[hh].astype(BF16)) * wq_ref[hh] for hh in heads]
    outs = [_dot(scores[hh], vs[hh]) + inter[hh] for hh in heads]
    kws = [(ks[hh].astype(F32) * wk_ref[hh]).astype(BF16) for hh in heads]
    for hh in heads:
        s_ref[0, hh] = states[hh] * dec_ref[hh] + _dot_tn(kws[hh], vs[hh])
    res = []
    for hh in heads:
        o = outs[hh]
        mu = jnp.mean(o, axis=-1, keepdims=True)
        oc = o - mu
        var = jnp.mean(oc * oc, axis=-1, keepdims=True)
        g = gs[hh]
        res.append((oc * lax.rsqrt(var + EPS) * gn_ref[hh] * (g * jax.nn.sigmoid(g))).astype(BF16))
    return res


def _retention_consts(tb, chunk):
    assert chunk <= tb and tb % chunk == 0
    lg = np.log(1.0 - 2.0 ** (-5.0 - np.arange(RET_H, dtype=np.float64)))
    idx = np.arange(tb, dtype=np.float64)
    visible = (idx[None, :] // chunk) <= (idx[:, None] // chunk)
    dm = np.exp(lg[:, None, None] * np.abs(idx[:, None] - idx[None, :])) * visible[None]
    wq = np.broadcast_to(np.exp(lg[:, None] * (idx[None, :] + 1.0))[:, :, None], (RET_H, tb, RET_D))
    wk = np.broadcast_to(np.exp(lg[:, None] * (tb - 1.0 - idx[None, :]))[:, :, None], (RET_H, tb, RET_D))
    dec = np.broadcast_to(np.exp(lg * tb)[:, None, None], (RET_H, 1, RET_D))
    return [jnp.asarray(a, F32) for a in (dm, wq, wk, dec)]


def _inproj_body(*refs, head_major, per_batch):
    x_ref, g_ref, wr_ref, wf_ref, wl_ref, bl_ref, cos_ref, sin_ref, hsum_ref = refs[0:9]
    if head_major:
        dm_ref, wq_ref, wk_ref, dec_ref, gn_ref, s0_ref = refs[9:15]
        ro_ref, s_ref = refs[15:17]
    else:
        rq_ref, rk_ref, rv_ref, rg_ref = refs[9:13]
    fq_ref, fkb_ref, fvp_ref, fk_ref, fv_ref, lf_ref, qn_ref, kn_ref = refs[-8:]

    def put_kv(dst_ref, z):
        if head_major:
            dst_ref[0] = z.T.reshape(FOX_H, FOX_D, z.shape[0])
        else:
            dst_ref[...] = z

    h = _rms(x_ref[...], g_ref[...]).astype(BF16)
    cos = cos_ref[...]
    sin = sin_ref[...]

    sls = [slice(hh * RET_D, (hh + 1) * RET_D) for hh in range(RET_H)]

    def rotary(z, scale):
        res = []
        for sl in sls:
            zz = z[:, sl]
            r = zz * cos + pltpu.roll(zz, RET_D // 2, 1) * sin
            res.append((r if scale is None else r * scale).astype(BF16))
        return res

    rqs = rotary(_dot(h, wr_ref[:, 0:RET_W]), None)
    rks = rotary(_dot(h, wr_ref[:, RET_W:2 * RET_W]), RET_D ** -0.5)
    zv = _dot(h, wr_ref[:, 2 * RET_W:3 * RET_W])
    zg = _dot(h, wr_ref[:, 3 * RET_W:4 * RET_W])
    def put_max_sq_norm(dst_ref, zb):
        sq = _dot(zb * zb, hsum_ref[...])
        span = sq.shape[0] // dst_ref.shape[0]
        for st in range(dst_ref.shape[0]):
            dst_ref[st] = jnp.max(sq[st * span:(st + 1) * span, :], axis=0, keepdims=True)

    def project_q():
        fqb = (_dot(h, wf_ref[:, 0:FOX_W]) * (LOG2E * FOX_D ** -0.5)).astype(BF16)
        fq_ref[...] = fqb
        put_max_sq_norm(qn_ref, fqb)

    def project_k():
        fk = _dot(h, wf_ref[:, FOX_W:2 * FOX_W])
        put_kv(fk_ref, fk)
        fkb = fk.astype(BF16)
        fkb_ref[...] = fkb
        put_max_sq_norm(kn_ref, fkb)

    def project_v():
        fv = _dot(h, wf_ref[:, 2 * FOX_W:3 * FOX_W])
        put_kv(fv_ref, fv)
        lane = lax.broadcasted_iota(jnp.int32, (fv.shape[0], LANES), 1)
        for pr in range(FOX_H // 2):
            vp = fv[:, pr * LANES:(pr + 1) * LANES]
            fvp_ref[:, (2 * pr) * LANES:(2 * pr + 1) * LANES] = jnp.where(lane < FOX_D, vp, 1.0).astype(BF16)
            fvp_ref[:, (2 * pr + 1) * LANES:(2 * pr + 2) * LANES] = jnp.where(lane >= FOX_D, vp, 1.0).astype(BF16)

    def project_logf():
        zl = _dot(h, wl_ref[...]) + bl_ref[...]
        lf = jnp.minimum(zl, 0.0) - jnp.log1p(jnp.exp(-jnp.abs(zl)))
        if head_major:
            lf_ref[0] = lf.T[0:FOX_H, :]
        else:
            lf_ref[...] = lf[:, 0:FOX_H]

    projections = [project_q, project_k, project_v, project_logf]
    if head_major:
        @pl.when(pl.program_id(0) % per_batch == 0)
        def _():
            s_ref[...] = s0_ref[...]

        def retention_rows(r0):
            rows = slice(r0, r0 + tb)
            outs = _retention_block([a[rows] for a in rqs], [a[rows] for a in rks],
                                    [zv[rows, sl].astype(BF16) for sl in sls], [zg[rows, sl] for sl in sls],
                                    dm_ref, wq_ref, wk_ref, dec_ref, gn_ref, s_ref)
            for sl, o in zip(sls, outs):
                ro_ref[rows, sl] = o

        tb = dm_ref.shape[1]
        starts = list(range(0, h.shape[0], tb))
        per = -(-len(projections) // len(starts))
        for i, r0 in enumerate(starts):
            retention_rows(r0)
            for proj in projections[i * per:(i + 1) * per]:
                proj()
    else:
        for sl, a, b in zip(sls, rqs, rks):
            rq_ref[:, sl] = a
            rk_ref[:, sl] = b
        rv_ref[...] = zv.astype(BF16)
        rg_ref[...] = zg
        for proj in projections:
            proj()


def _inproj(x, g, wr, wf, wl, bl, cos_t, sin_t, tm, seq, gn, s0, chunk):
    n = x.shape[0]
    nt = cos_t.shape[0] // tm
    head_major = tm <= seq
    row = lambda w: pl.BlockSpec((tm, w), lambda i: (i, 0))
    tab = pl.BlockSpec((tm, LANES), lambda i: (i % nt, 0))
    shp = lambda w, dt: jax.ShapeDtypeStruct((n, w), dt)
    hsum = jnp.asarray(np.arange(FOX_W)[:, None] // FOX_D == np.arange(LANES)[None, :], BF16)
    stats = max(1, tm // FOX_TILE)
    stat_spec = pl.BlockSpec((stats, 1, LANES), lambda i: (i, 0, 0))
    stat_shape = jax.ShapeDtypeStruct((n // tm * stats, 1, LANES), F32)
    if head_major:
        per_batch = seq // tm
        kv_spec = pl.BlockSpec((1, FOX_H, FOX_D, tm), lambda i: (i // per_batch, 0, 0, i % per_batch))
        kv_shape = jax.ShapeDtypeStruct((n // seq, FOX_H, FOX_D, seq), F32)
        lf_spec = pl.BlockSpec((1, FOX_H, tm), lambda i: (i // per_batch, 0, i % per_batch))
        lf_shape = jax.ShapeDtypeStruct((n // seq, FOX_H, seq), F32)
        consts = _retention_consts(min(RET_BLOCK, tm), chunk)
        st_spec = pl.BlockSpec((1, RET_H, RET_D, RET_D), lambda i: (i // per_batch, 0, 0, 0))
        extra_in = consts + [gn, s0]
        extra_in_specs = [_const_spec(c.shape) for c in consts] + [_const_spec(gn.shape), st_spec]
        ret_specs = [row(RET_W), st_spec]
        ret_shapes = [shp(RET_W, BF16), jax.ShapeDtypeStruct(s0.shape, F32)]
    else:
        per_batch = 1
        kv_spec, kv_shape = row(FOX_W), shp(FOX_W, F32)
        lf_spec, lf_shape = row(FOX_H), shp(FOX_H, F32)
        extra_in, extra_in_specs = [], []
        ret_specs = [row(RET_W)] * 4
        ret_shapes = [shp(RET_W, BF16), shp(RET_W, BF16), shp(RET_W, BF16), shp(RET_W, F32)]
    return pl.pallas_call(
        functools.partial(_inproj_body, head_major=head_major, per_batch=per_batch),
        grid=(n // tm,),
        in_specs=[row(D_MODEL), _const_spec(g.shape), _const_spec(wr.shape), _const_spec(wf.shape),
                  _const_spec(wl.shape), _const_spec(bl.shape), tab, tab, _const_spec(hsum.shape)] + extra_in_specs,
        out_specs=ret_specs + [row(FOX_W), row(FOX_W), row(FOX_H * LANES), kv_spec, kv_spec, lf_spec,
                               stat_spec, stat_spec],
        out_shape=ret_shapes + [shp(FOX_W, BF16), shp(FOX_W, BF16), shp(FOX_H * LANES, BF16), kv_shape,
                                kv_shape, lf_shape, stat_shape, stat_shape],
        compiler_params=_params(("arbitrary",)),
        name="inproj_ab_retention" if head_major else "inproj_ab",
    )(x, g, wr, wf, wl, bl, cos_t, sin_t, hsum, *extra_in)


def _cumsum_body(x_ref, u_ref, o_ref):
    rows, total = x_ref.shape
    u = u_ref[...]

    def chunk(i, carry):
        off = pl.multiple_of(i * CUMSUM_CHUNK, CUMSUM_CHUNK)
        xc = x_ref[:, pl.ds(off, CUMSUM_CHUNK)]
        hi = xc.astype(BF16)
        r1 = xc - hi.astype(F32)
        mid = r1.astype(BF16)
        lo = (r1 - mid.astype(F32)).astype(BF16)
        y = _dot(hi, u) + _dot(mid, u) + _dot(lo, u) + carry
        o_ref[:, pl.ds(off, CUMSUM_CHUNK)] = y * LOG2E
        return y[:, CUMSUM_CHUNK - 1:CUMSUM_CHUNK]

    lax.fori_loop(0, total // CUMSUM_CHUNK, chunk, jnp.zeros((rows, 1), F32))


def _cumsum_lanes(x):
    idx = np.arange(CUMSUM_CHUNK)
    u = jnp.asarray(idx[:, None] <= idx[None, :], BF16)
    return pl.pallas_call(
        _cumsum_body,
        out_shape=jax.ShapeDtypeStruct(x.shape, F32),
        compiler_params=pltpu.CompilerParams(vmem_limit_bytes=VMEM_LIMIT),
        name="logf_cumsum",
    )(x, u)


def _ret_body(q_ref, k_ref, v_ref, rg_ref, dm_ref, wq_ref, wk_ref, dec_ref, gn_ref, s0_ref,
              ro_ref, s_ref):
    @pl.when(pl.program_id(1) == 0)
    def _():
        s_ref[...] = s0_ref[...]

    sls = [slice(hh * RET_D, (hh + 1) * RET_D) for hh in range(RET_H)]
    outs = _retention_block([q_ref[:, sl] for sl in sls], [k_ref[:, sl] for sl in sls],
                            [v_ref[:, sl] for sl in sls], [rg_ref[:, sl] for sl in sls],
                            dm_ref, wq_ref, wk_ref, dec_ref, gn_ref, s_ref)
    for sl, o in zip(sls, outs):
        ro_ref[:, sl] = o


def _retention(rq, rk, rv, rg, gn, s0, batch, seq, chunk):
    tb = min(RET_BLOCK, seq)
    nb = seq // tb
    consts = _retention_consts(tb, chunk)
    row = pl.BlockSpec((tb, RET_W), lambda b, t: (b * nb + t, 0))
    st = pl.BlockSpec((1, RET_H, RET_D, RET_D), lambda b, t: (b, 0, 0, 0))
    return pl.pallas_call(
        _ret_body,
        grid=(batch, nb),
        in_specs=[row, row, row, row] + [_const_spec(c.shape) for c in consts] + [_const_spec(gn.shape), st],
        out_specs=[row, st],
        out_shape=[jax.ShapeDtypeStruct((batch * seq, RET_W), BF16),
                   jax.ShapeDtypeStruct((batch, RET_H, RET_D, RET_D), F32)],
        compiler_params=_params(("parallel", "arbitrary")),
        name="retention",
    )(rq, rk, rv, rg, *consts, gn, s0)


def _fox_body(start_ref, q_ref, k_ref, v_ref, c_ref, mask_ref, o_ref, qs_scr, ql_scr, m_scr, acc_scr, *, tq, tk):
    n_full = pl.program_id(2)
    pair = pl.program_id(0) * pl.num_programs(1) + pl.program_id(1)
    first_a = start_ref[(2 * pair) * pl.num_programs(2) + n_full]
    first_b = start_ref[(2 * pair + 1) * pl.num_programs(2) + n_full]
    solo_first = jnp.minimum(first_a, first_b)
    first = jnp.maximum(first_a, first_b)
    solo_head = (first_b < first_a).astype(jnp.int32)
    q = q_ref[...]
    lane = lax.broadcasted_iota(jnp.int32, q.shape, 1)
    zero = jnp.zeros_like(q)
    qs_scr[0:tq, :] = jnp.where(lane < FOX_D, q, zero)
    qs_scr[tq:2 * tq, :] = jnp.where(lane >= FOX_D, q, zero)
    ql_scr[0:tq // 2, :] = qs_scr[tq // 2:tq, :]
    ql_scr[tq // 2:tq, :] = qs_scr[tq + tq // 2:2 * tq, :]
    m_scr[...] = jnp.full(m_scr.shape, -jnp.inf, F32)
    acc_scr[...] = jnp.zeros(acc_scr.shape, F32)

    def stacked_tile(q_scr, q_rows, state_rows, off, width, mask):
        s = _dot_nt(q_scr[...], k_ref[0, pl.ds(off, width), :])
        shs = [s[q_rows[hh], :] - c_ref[0, 0, hh:hh + 1, pl.ds(off, width)] for hh in range(2)]
        if mask is not None:
            shs = [sh + mask for sh in shs]
        chunks = [[sh[:, c * LANES:(c + 1) * LANES] for c in range(width // LANES)] for sh in shs]
        m_prevs = [m_scr[rows, :] for rows in state_rows]
        m_news = [jnp.maximum(m_prevs[hh], jnp.max(functools.reduce(jnp.maximum, chunks[hh]), axis=1, keepdims=True))
                  for hh in range(2)]
        ps = [jnp.concatenate([jnp.exp2(ch - m_news[hh]) for ch in chunks[hh]], axis=1).astype(BF16)
              for hh in range(2)]
        pvs = [_dot(ps[hh], v_ref[0, pl.ds(off, width), hh * LANES:(hh + 1) * LANES]) for hh in range(2)]
        for hh in range(2):
            rows = state_rows[hh]
            acc_scr[rows, :] = jnp.exp2(m_prevs[hh] - m_news[hh]) * acc_scr[rows, :] + pvs[hh]
            m_scr[rows, :] = m_news[hh]

    head_rows = [slice(hh * tq, (hh + 1) * tq) for hh in range(2)]

    def tile(kj):
        stacked_tile(qs_scr, head_rows, head_rows, pl.multiple_of(kj * tk, tk), tk, None)

    def diagonal_tile():
        hq = tq // 2
        off = pl.multiple_of(n_full * tk, tk)
        stacked_tile(qs_scr, head_rows, head_rows, off, hq, mask_ref[...])
        lower = [slice(hh * tq + hq, (hh + 1) * tq) for hh in range(2)]
        stacked_tile(ql_scr, [slice(hh * hq, (hh + 1) * hq) for hh in range(2)], lower,
                     pl.multiple_of(off + hq, hq), hq, mask_ref[0:hq, :])

    def solo_tile(kj):
        off = pl.multiple_of(kj * tk, tk)
        rows = pl.ds(pl.multiple_of(solo_head * tq, tq), tq)
        sh = (_dot_nt(qs_scr[rows, :], k_ref[0, pl.ds(off, tk), :])
              - c_ref[0, 0, pl.ds(solo_head, 1), pl.ds(off, tk)])
        chunks = [sh[:, c * LANES:(c + 1) * LANES] for c in range(tk // LANES)]
        m_prev = m_scr[rows, :]
        m_new = jnp.maximum(m_prev, jnp.max(functools.reduce(jnp.maximum, chunks), axis=1, keepdims=True))
        p = jnp.concatenate([jnp.exp2(ch - m_new) for ch in chunks], axis=1).astype(BF16)
        pv = _dot(p, v_ref[0, pl.ds(off, tk), :])
        pv = jnp.where(solo_head == 0, pv[:, 0:LANES], pv[:, LANES:2 * LANES])
        acc_scr[rows, :] = jnp.exp2(m_prev - m_new) * acc_scr[rows, :] + pv
        m_scr[rows, :] = m_new

    def run_tiles(start, count, visit):
        def group(i, carry):
            for j in range(FOX_UNROLL):
                visit(start + FOX_UNROLL * i + j)
            return carry

        lax.fori_loop(0, count // FOX_UNROLL, group, 0)
        span = FOX_UNROLL // 2
        while span >= 1:
            @pl.when((count // span) % 2 == 1)
            def _(span=span):
                base = start + (count // (2 * span)) * (2 * span)
                for j in range(span):
                    visit(base + j)
            span //= 2

    run_tiles(solo_first, first - solo_first, solo_tile)
    run_tiles(first, n_full - first, tile)
    diagonal_tile()

    acc_a = acc_scr[0:tq, :]
    acc_b = acc_scr[tq:2 * tq, :]
    half = LANES // 2
    o_ref[...] = jnp.where(lane < FOX_D, acc_a / pltpu.roll(acc_a, half, 1),
                           acc_b / pltpu.roll(acc_b, half, 1)).astype(BF16)


def _fox_first_tiles(qn2, kn2, c_all, batch, seq, tile_len):
    nq = seq // tile_len
    qn = jnp.sqrt(qn2[:, 0, 0:FOX_H]).reshape(batch, nq, FOX_H).transpose(0, 2, 1)
    kn = jnp.sqrt(kn2[:, 0, 0:FOX_H]).reshape(batch, nq, FOX_H).transpose(0, 2, 1)
    c = c_all[..., 0:seq].reshape(batch, FOX_H, nq, tile_len)
    c_first, c_last = c[..., 0], c[..., tile_len - 1]
    dots = FOX_NORM_SLACK * (qn[..., :, None] * kn[..., None, :] + (qn * kn)[..., :, None])
    gap = dots + c_first[..., :, None] - c_last[..., None, :]
    before = jnp.arange(nq)[None, :] < jnp.arange(nq)[:, None]
    droppable = (gap < -FOX_DROP_LOG2) & before
    first = jnp.sum(jnp.cumprod(droppable.astype(jnp.int32), axis=-1), axis=-1)
    return first.reshape(-1).astype(jnp.int32)


def _fox(first_tile, fq, k_all, v_all, c_all, batch, seq, tile_len):
    tq = tk = tile_len
    nq = seq // tq
    half = tile_len // 2
    mask = jnp.asarray(np.where(np.arange(half)[None, :] <= np.arange(tile_len)[:, None], 0.0, NEG_INF), F32)
    body = functools.partial(_fox_body, tq=tq, tk=tk)
    grid_spec = pltpu.PrefetchScalarGridSpec(
        num_scalar_prefetch=1,
        grid=(batch, FOX_H // 2, nq),
        in_specs=[pl.BlockSpec((tq, LANES), lambda b, h, i, _: (b * nq + i, h)),
                  pl.BlockSpec((1, seq, LANES), lambda b, h, i, _: (b, 0, h)),
                  pl.BlockSpec((1, seq, 2 * LANES), lambda b, h, i, _: (b, 0, h)),
                  pl.BlockSpec((1, 1, 2, seq), lambda b, h, i, _: (b, h, 0, 0)),
                  _const_spec(mask.shape)],
        out_specs=pl.BlockSpec((tq, LANES), lambda b, h, i, _: (b * nq + i, h)),
        scratch_shapes=[pltpu.VMEM((2 * tq, LANES), BF16), pltpu.VMEM((tq, LANES), BF16),
                        pltpu.VMEM((2 * tq, LANES), F32), pltpu.VMEM((2 * tq, LANES), F32)])
    return pl.pallas_call(
        body,
        grid_spec=grid_spec,
        out_shape=jax.ShapeDtypeStruct((batch * seq, FOX_W), BF16),
        compiler_params=_params(("parallel", "parallel", "arbitrary")),
        name="fox_attention",
    )(first_tile, fq, k_all, v_all, c_all, mask)


def _fox_cached_body(q_ref, kp_ref, vp_ref, kn_ref, vn_ref, c_ref, o_ref, *, past, lq):
    q = q_ref[...]
    lane = lax.broadcasted_iota(jnp.int32, q.shape, 1)
    zero = jnp.zeros_like(q)
    qs = jnp.concatenate([jnp.where(lane < FOX_D, q, zero), jnp.where(lane >= FOX_D, q, zero)], axis=0)
    head_rows = lambda a, b, width: jnp.concatenate(
        [jnp.broadcast_to(a, (lq, width)), jnp.broadcast_to(b, (lq, width))], axis=0)
    s_past = _dot(qs, kp_ref[0].astype(BF16)) - head_rows(c_ref[0, 0, 0:1, 0:past], c_ref[0, 0, 1:2, 0:past], past)
    s_new = _dot_nt(qs, kn_ref[...]) - head_rows(c_ref[0, 0, 0:1, past:past + lq],
                                                 c_ref[0, 0, 1:2, past:past + lq], lq)
    qi = lax.broadcasted_iota(jnp.int32, s_new.shape, 0)
    ki = lax.broadcasted_iota(jnp.int32, s_new.shape, 1)
    s_new = jnp.where(ki <= jnp.where(qi >= lq, qi - lq, qi), s_new, NEG_INF)
    m = jnp.maximum(jnp.max(s_past, axis=1, keepdims=True), jnp.max(s_new, axis=1, keepdims=True))
    p_past = jnp.exp2(s_past - m)
    p_new = jnp.exp2(s_new - m)
    denom = jnp.sum(p_past, axis=1, keepdims=True) + jnp.sum(p_new, axis=1, keepdims=True)
    pb_new = p_new.astype(BF16)
    acc = _dot_nt(p_past.astype(BF16), vp_ref[0].astype(BF16))
    acc_a = (acc[0:lq] + _dot(pb_new[0:lq], vn_ref[:, 0:LANES])) / denom[0:lq]
    acc_b = (acc[lq:2 * lq] + _dot(pb_new[lq:2 * lq], vn_ref[:, LANES:2 * LANES])) / denom[lq:2 * lq]
    o_ref[...] = jnp.where(lane < FOX_D, acc_a, acc_b).astype(BF16)


def _fox_cached(fq, k_past, v_past, k_new, v_new, c_all, batch, lq):
    past = k_past.shape[2]
    new_rows = lambda w: pl.BlockSpec((lq, w), lambda b, h: (b, h))
    cache = pl.BlockSpec((1, LANES, past), lambda b, h: (b, h, 0))
    return pl.pallas_call(
        functools.partial(_fox_cached_body, past=past, lq=lq),
        grid=(batch, FOX_H // 2),
        in_specs=[new_rows(LANES), cache, cache, new_rows(LANES), new_rows(2 * LANES),
                  pl.BlockSpec((1, 1, 2, c_all.shape[-1]), lambda b, h: (b, h, 0, 0))],
        out_specs=new_rows(LANES),
        out_shape=jax.ShapeDtypeStruct((batch * lq, FOX_W), BF16),
        compiler_params=_params(("parallel", "parallel")),
        name="fox_attention_cached",
    )(fq, k_past, v_past, k_new, v_new, c_all)


def _memory_kv_body(x_ref, g_ref, w_ref, k_ref, v_ref, kb_ref, vb_ref):
    y = _dot(_rms(x_ref[...], g_ref[0]).astype(BF16), w_ref[0])
    k = y[:, 0:D_MODEL]
    v = y[:, D_MODEL:2 * D_MODEL]
    k_ref[0] = k
    v_ref[0] = v
    kb_ref[0] = k.astype(BF16)
    vb_ref[0] = v.astype(BF16)


def _memory_kv(x, g, w, tm):
    n = x.shape[0]
    layers = w.shape[0]
    out = pl.BlockSpec((1, tm, D_MODEL), lambda l, i: (l, i, 0))
    shape = lambda dt: jax.ShapeDtypeStruct((layers, n, D_MODEL), dt)
    return pl.pallas_call(
        _memory_kv_body,
        grid=(layers, n // tm),
        in_specs=[pl.BlockSpec((tm, D_MODEL), lambda l, i: (i, 0)),
                  pl.BlockSpec((1, 1, D_MODEL), lambda l, i: (l, 0, 0)),
                  pl.BlockSpec((1, D_MODEL, 2 * D_MODEL), lambda l, i: (l, 0, 0))],
        out_specs=[out] * 4,
        out_shape=[shape(F32), shape(F32), shape(BF16), shape(BF16)],
        compiler_params=_params(("parallel", "parallel")),
        name="memory_kv",
    )(x, g, w)


def _post_body(*refs, has_mix, parts, seq_rows, mem_len):
    if has_mix:
        (x_ref, ro_ref, fo_ref, wo_ref, nw_ref, wxq_ref, wxo_ref, mk_ref, mv_ref, wup_ref, wdn_ref,
         o_ref) = refs
    else:
        x_ref, nw_ref, wxq_ref, wxo_ref, mk_ref, mv_ref, wup_ref, wdn_ref, o_ref = refs
    rows = [slice(r0, r0 + nrows) for r0, nrows in parts]
    xs = [x_ref[rs, :] for rs in rows]
    if has_mix:
        mixes = [_dot(ro_ref[rs, :], wo_ref[0:RET_W, :]) + _dot(fo_ref[rs, :], wo_ref[RET_W:RET_W + FOX_W, :])
                 for rs in rows]
        xs = [x + _rms(mix, nw_ref[1:2, :]) for x, mix in zip(xs, mixes)]

    qs = [_dot(_rms(x, nw_ref[2:3, :]).astype(BF16), wxq_ref[...]) for x in xs]
    ctx = []
    for (r0, nrows), q in zip(parts, qs):
        groups = []
        for g0 in range(0, nrows, min(seq_rows, nrows)):
            gs = slice(g0, g0 + min(seq_rows, nrows))
            m0 = ((r0 + g0) // seq_rows) * mem_len
            heads = []
            for hh in range(X_H):
                sl = slice(hh * X_D, (hh + 1) * X_D)
                s = _dot_nt(q[gs, sl].astype(BF16), mk_ref[m0:m0 + mem_len, sl]) * (X_D ** -0.5)
                p = jnp.exp(s - jnp.max(s, axis=-1, keepdims=True))
                o = _dot(p.astype(BF16), mv_ref[m0:m0 + mem_len, sl]) / jnp.sum(p, axis=-1, keepdims=True)
                heads.append(o.astype(BF16))
            groups.append(jnp.concatenate(heads, axis=1))
        ctx.append(groups[0] if len(groups) == 1 else jnp.concatenate(groups, axis=0))
    atts = [_dot(c, wxo_ref[...]) for c in ctx]
    xs = [x + _rms(att, nw_ref[3:4, :]) for x, att in zip(xs, atts)]

    hs = [_rms(x, nw_ref[4:5, :]).astype(BF16) for x in xs]
    accs = [None] * len(parts)
    for c in range(D_FF // D_MODEL):
        sl = slice(c * D_MODEL, (c + 1) * D_MODEL)
        for i, h in enumerate(hs):
            r = jnp.maximum(_dot(h, wup_ref[:, sl]), 0.0)
            part = _dot((r * r).astype(BF16), wdn_ref[sl, :])
            accs[i] = part if accs[i] is None else accs[i] + part
    for rs, x, acc in zip(rows, xs, accs):
        o_ref[rs, :] = x + _rms(acc, nw_ref[5:6, :])


def _post(x, mix_in, nw, wxq, wxo, mk, mv, wup, wdn, seq, tm):
    n = x.shape[0]
    mem_len = mk.shape[0] // (n // seq)
    row = lambda w: pl.BlockSpec((tm, w), lambda i: (i, 0))
    if tm <= seq:
        per_batch = seq // tm
        mem = pl.BlockSpec((mem_len, D_MODEL), lambda i: (i // per_batch, 0))
        seq_rows = tm
        nparts = POST_PARTS if tm % (POST_PARTS * 16) == 0 else 1
        parts = tuple((p * (tm // nparts), tm // nparts) for p in range(nparts))
    else:
        mem = pl.BlockSpec(((tm // seq) * mem_len, D_MODEL), lambda i: (i, 0))
        seq_rows = seq
        parts = ((0, tm),)
    args = [x]
    specs = [row(D_MODEL)]
    if mix_in is not None:
        ro, fo, wo = mix_in
        args += [ro, fo, wo]
        specs += [row(RET_W), row(FOX_W), _const_spec(wo.shape)]
    args += [nw, wxq, wxo, mk, mv, wup, wdn]
    specs += [_const_spec(nw.shape), _const_spec(wxq.shape), _const_spec(wxo.shape), mem, mem,
              _const_spec(wup.shape), _const_spec(wdn.shape)]
    return pl.pallas_call(
        functools.partial(_post_body, has_mix=mix_in is not None, parts=parts, seq_rows=seq_rows, mem_len=mem_len),
        grid=(n // tm,),
        in_specs=specs,
        out_specs=row(D_MODEL),
        out_shape=jax.ShapeDtypeStruct((n, D_MODEL), F32),
        compiler_params=_params(("parallel",)),
        name="xattn_mlp_mix" if mix_in is not None else "xattn_mlp",
    )(*args)


def _s5prep_body(lre_ref, lim_ref, ldt_ref, lre_rep_ref, lim_rep_ref, bre_ref, bim_ref,
                 are_ref, aim_ref, bbre_ref, bbim_ref):
    dt = jnp.exp(ldt_ref[...])

    def zoh(lre, lim):
        mag = jnp.exp(lre * dt)
        a_re = mag * jnp.cos(lim * dt)
        a_im = mag * jnp.sin(lim * dt)
        den = lre * lre + lim * lim
        num_re = a_re - 1.0
        f_re = (num_re * lre + a_im * lim) / den
        f_im = (a_im * lre - num_re * lim) / den
        return a_re, a_im, f_re, f_im

    a_re, a_im, _, _ = zoh(lre_ref[...], lim_ref[...])
    are_ref[...] = a_re
    aim_ref[...] = a_im
    _, _, f_re, f_im = zoh(lre_rep_ref[...], lim_rep_ref[...])
    b_re = bre_ref[...]
    b_im = bim_ref[...]
    bbre_ref[...] = f_re * b_re - f_im * b_im
    bbim_ref[...] = f_re * b_im + f_im * b_re


def _s5_discretize(lam_re, lam_im, log_dt, b_re, b_im):
    rep = lambda a: jnp.repeat(a, S5_GROUP, axis=1)
    wide = jax.ShapeDtypeStruct((S5_G, S5_P * S5_GROUP), F32)
    small = jax.ShapeDtypeStruct((S5_G, S5_P), F32)
    a_re, a_im, bb_re, bb_im = pl.pallas_call(
        _s5prep_body, out_shape=[small, small, wide, wide], name="s5_discretize",
    )(lam_re, lam_im, log_dt.reshape(S5_G, 1), rep(lam_re), rep(lam_im),
      b_re.reshape(S5_G, -1), b_im.reshape(S5_G, -1))
    a = jnp.stack([a_re.reshape(-1), a_im.reshape(-1)])
    gpc = S5_LANE_CHUNK // S5_P
    nch = S5_G // gpc
    eye = jnp.eye(gpc, dtype=F32)

    def in_blocks(bb):
        t = bb.reshape(nch, gpc, S5_P, S5_GROUP).transpose(0, 1, 3, 2)
        return (t[:, :, :, None, :] * eye[None, :, None, :, None]).reshape(nch, gpc * S5_GROUP, gpc * S5_P)

    bmat = jnp.concatenate([in_blocks(bb_re), in_blocks(bb_im)], axis=-1).astype(BF16)
    return a, bmat


def _s5_out_matrices(c_re, c_im):
    gpc = S5_LANE_CHUNK // S5_P
    nch = S5_G // gpc
    eye = jnp.eye(gpc, dtype=F32)

    def out_blocks(c):
        t = c.reshape(nch, gpc, S5_GROUP, S5_P).transpose(0, 1, 3, 2)
        return (t[:, :, :, None, :] * eye[None, :, None, :, None]).reshape(nch, gpc * S5_P, gpc * S5_GROUP)

    return jnp.concatenate([out_blocks(c_re), -out_blocks(c_im)], axis=1).astype(BF16)


def _s5_body(x_ref, nw_ref, perm_ref, win_ref, bmat_ref, a_ref, cmat_ref, d_ref, wglu_ref, st0_ref,
             o_ref, st_ref, u_scr, bu_scr, y_scr, *, ts):
    nb = x_ref.shape[0]
    rows = nb * ts
    lc = S5_LANE_CHUNK
    gw = (lc // S5_P) * S5_GROUP

    @pl.when(pl.program_id(0) == 0)
    def _():
        st_ref[...] = st0_ref[...]

    x = x_ref[...].reshape(rows, D_MODEL)
    h = _dot(perm_ref[0], _rms(x, nw_ref[0:1, :]).astype(BF16)).astype(BF16)
    u_scr[...] = _dot(h, win_ref[...])
    nv = lc // LANES
    for j in range(S5_STATE // lc):
        uj = u_scr[:, j * gw:(j + 1) * gw]
        bu = _dot(uj.astype(BF16), bmat_ref[j])
        for c in range(2 * nv):
            bu_scr[c] = bu[:, c * LANES:(c + 1) * LANES]
        a_re = [jnp.broadcast_to(a_ref[0:1, j * lc + c * LANES:j * lc + (c + 1) * LANES], (nb, LANES))
                for c in range(nv)]
        a_im = [jnp.broadcast_to(a_ref[1:2, j * lc + c * LANES:j * lc + (c + 1) * LANES], (nb, LANES))
                for c in range(nv)]

        def step(t, carry):
            idx = pl.ds(pl.multiple_of(t * nb, nb), nb)
            new = []
            for c in range(nv):
                s_re, s_im = carry[c], carry[nv + c]
                new.append(a_re[c] * s_re - a_im[c] * s_im + bu_scr[c, idx, :])
            for c in range(nv):
                s_re, s_im = carry[c], carry[nv + c]
                new.append(a_re[c] * s_im + a_im[c] * s_re + bu_scr[nv + c, idx, :])
            for c in range(2 * nv):
                bu_scr[c, idx, :] = new[c]
            return tuple(new)

        init = tuple(st_ref[:, j * lc + c * LANES:j * lc + (c + 1) * LANES] for c in range(nv)) + tuple(
            st_ref[:, S5_STATE + j * lc + c * LANES:S5_STATE + j * lc + (c + 1) * LANES] for c in range(nv))
        fin = lax.fori_loop(0, ts, step, init, unroll=True)
        for c in range(nv):
            st_ref[:, j * lc + c * LANES:j * lc + (c + 1) * LANES] = fin[c]
            st_ref[:, S5_STATE + j * lc + c * LANES:S5_STATE + j * lc + (c + 1) * LANES] = fin[nv + c]
        xs = jnp.concatenate([bu_scr[c] for c in range(2 * nv)], axis=1).astype(BF16)
        y_scr[:, j * gw:(j + 1) * gw] = _dot(xs, cmat_ref[j]) + d_ref[0:1, j * gw:(j + 1) * gw] * uj
    y = _dot(perm_ref[1], y_scr[...].astype(BF16)).astype(BF16)
    g = _dot(y, wglu_ref[...])
    out = g[:, 0:D_MODEL] * jax.nn.sigmoid(g[:, D_MODEL:2 * D_MODEL])
    o_ref[...] = (x + _rms(out, nw_ref[1:2, :])).reshape(nb, ts, D_MODEL)


def _s5_layer(x3, nw, win, bmat, a, cmat, d, wglu, st0, ts):
    nb, seq, _ = x3.shape
    rows = nb * ts
    blk = pl.BlockSpec((nb, ts, D_MODEL), lambda t: (0, t, 0))
    r = np.arange(rows)
    fwd = np.zeros((rows, rows), np.float32)
    fwd[(r % ts) * nb + r // ts, r] = 1.0
    perm = jnp.asarray(np.stack([fwd, fwd.T]), BF16)
    consts = [nw, perm, win, bmat, a, cmat, d, wglu, st0]
    return pl.pallas_call(
        functools.partial(_s5_body, ts=ts),
        grid=(seq // ts,),
        in_specs=[blk] + [_const_spec(c.shape) for c in consts],
        out_specs=[blk, pl.BlockSpec(st0.shape, lambda t: (0, 0))],
        out_shape=[jax.ShapeDtypeStruct(x3.shape, F32), jax.ShapeDtypeStruct(st0.shape, F32)],
        scratch_shapes=[pltpu.VMEM((rows, D_MODEL), F32),
                        pltpu.VMEM((2 * S5_LANE_CHUNK // LANES, rows, LANES), F32),
                        pltpu.VMEM((rows, D_MODEL), F32)],
        compiler_params=_params(("arbitrary",)),
        name="s5_layer",
    )(x3, *consts)


def _rotary_tables(pos, rows):
    half = RET_D // 2
    inv = ROPE_BASE ** (-jnp.arange(half, dtype=F32) / half)
    ang = pos.astype(F32)[:, None] * inv[None, :]
    cos = jnp.cos(ang)
    sin = jnp.sin(ang)
    cos_t = jnp.concatenate([cos, cos], axis=1)
    sin_t = jnp.concatenate([-sin, sin], axis=1)
    reps = max(1, rows // pos.shape[0])
    return jnp.tile(cos_t, (reps, 1)), jnp.tile(sin_t, (reps, 1))


def _prepare_weights(norm_w, w_in_ab, b_fox_f, w_out_ab, w_in_c, s5_lambda_re, s5_lambda_im, s5_log_dt,
                     s5_b_re, s5_b_im, s5_c_re, s5_c_im, s5_d, w_glu, w_xq, w_xo, w_up, w_down):
    w = {}
    wab = w_in_ab[0].astype(BF16)
    w["wr"] = wab[:, 0:4 * RET_W]
    w["wf"] = wab[:, 4 * RET_W:4 * RET_W + 3 * FOX_W]
    w["wl"] = jnp.pad(wab[:, 4 * RET_W + 3 * FOX_W:], ((0, 0), (0, LANES - FOX_H)))
    w["bl"] = jnp.pad(b_fox_f[0].astype(F32), (0, LANES - FOX_H)).reshape(1, LANES)
    w["wo"] = w_out_ab[0].astype(BF16)
    w["nw"] = norm_w.astype(F32)
    w["win_c"] = w_in_c[0].astype(BF16)
    w["a"], w["bmat"] = _s5_discretize(s5_lambda_re[0], s5_lambda_im[0], s5_log_dt[0], s5_b_re[0], s5_b_im[0])
    w["cmat"] = _s5_out_matrices(s5_c_re[0].astype(F32), s5_c_im[0].astype(F32))
    w["d"] = s5_d[0].astype(F32).reshape(1, D_MODEL)
    w["wglu"] = w_glu[0].astype(BF16)
    for name, arr in (("wxq", w_xq), ("wxo", w_xo), ("wup", w_up), ("wdn", w_down)):
        w[name] = arr.astype(BF16)
    return w


def _run_group(x3, pos, w, gn_w, mem_kv, ret_s0, fox_past, s5_st0):
    batch, seq, _ = x3.shape
    n = batch * seq
    tm = min(POST_TILE if seq >= POST_TILE else ROW_TILE, n)
    post_tile = lambda t: t if seq >= t else min(n, (t // seq) * seq)
    x = x3.reshape(n, D_MODEL)

    cos_t, sin_t = _rotary_tables(pos, tm)
    chunk = CHUNK if fox_past is None else seq
    gn = gn_w.reshape(RET_H, 1, RET_D)
    outs = _inproj(x, w["nw"][0, 0:1], w["wr"], w["wf"], w["wl"], w["bl"], cos_t, sin_t, tm, seq, gn, ret_s0, chunk)
    fq, fkb, fvp, fk, fv, lf, qn2, kn2 = outs[-8:]
    if tm <= seq:
        ro, ret_state = outs[0:2]
        lf_t = lf
        fk = jnp.transpose(fk, (0, 3, 1, 2))
        fv = jnp.transpose(fv, (0, 3, 1, 2))
        lf3 = jnp.transpose(lf, (0, 2, 1))
    else:
        lf3 = lf.reshape(batch, seq, FOX_H)
        lf_t = jnp.transpose(lf3, (0, 2, 1))
    if fox_past is not None:
        lf_t = jnp.concatenate([jnp.transpose(fox_past[2].astype(F32), (0, 2, 1)), lf_t], axis=2)
    lk = lf_t.shape[2]
    pad = -lk % CUMSUM_CHUNK
    lf_rows = jnp.pad(lf_t.reshape(batch * FOX_H, lk), ((0, 0), (0, pad)))
    c_all = _cumsum_lanes(lf_rows).reshape(batch, FOX_H // 2, 2, lk + pad)
    if fox_past is None:
        assert tm % FOX_TILE == 0
        first_tile = _fox_first_tiles(qn2, kn2, c_all, batch, seq, FOX_TILE)
        fo = _fox(first_tile, fq, fkb.reshape(batch, seq, FOX_W), fvp.reshape(batch, seq, FOX_H * LANES), c_all,
                  batch, seq, FOX_TILE)
    else:
        past = fox_past[0].shape[1]
        cache_t = lambda a: jnp.transpose(a, (0, 2, 3, 1)).reshape(batch, FOX_W, past)
        fo = _fox_cached(fq, cache_t(fox_past[0]), cache_t(fox_past[1]), fkb, fvp, c_all, batch, seq)

    if tm > seq:
        ro, ret_state = _retention(*outs[0:4], gn, ret_s0, batch, seq, chunk)

    x = _post(x, (ro, fo, w["wo"]), w["nw"][0], w["wxq"][0], w["wxo"][0], mem_kv[0][0], mem_kv[0][1],
              w["wup"][0], w["wdn"][0], seq, post_tile(POST_TILE))

    ts = min(S5_TS, seq)
    x3n, s5_state = _s5_layer(x.reshape(batch, seq, D_MODEL), w["nw"][1], w["win_c"], w["bmat"], w["a"],
                              w["cmat"], w["d"], w["wglu"], s5_st0, ts)
    x = _post(x3n.reshape(n, D_MODEL), None, w["nw"][1], w["wxq"][1], w["wxo"][1], mem_kv[1][0], mem_kv[1][1],
              w["wup"][1], w["wdn"][1], seq, post_tile(POST_TILE))

    new = dict(ret=ret_state[None], fox_k=fk.reshape(1, batch, seq, FOX_H, FOX_D),
               fox_v=fv.reshape(1, batch, seq, FOX_H, FOX_D), fox_logf=lf3[None],
               s5_re=s5_state[:, 0:S5_STATE].reshape(1, batch, S5_G, S5_P),
               s5_im=s5_state[:, S5_STATE:].reshape(1, batch, S5_G, S5_P))
    return x.reshape(batch, seq, D_MODEL), new


def kernel(x_prompt, x_sample, cache_ret_state, cache_fox_k, cache_fox_v, cache_fox_logf, state_s5_re, state_s5_im, cache_mem_k, cache_mem_v, mem_prompt, norm_w, w_in_ab, b_fox_f, ret_gn_w, w_out_ab, w_in_c, s5_lambda_re, s5_lambda_im, s5_log_dt, s5_b_re, s5_b_im, s5_c_re, s5_c_im, s5_d, w_glu, mem_norm_w, w_xq, w_xk, w_xv, w_xo, w_up, w_down):
    w = _prepare_weights(norm_w, w_in_ab, b_fox_f, w_out_ab, w_in_c, s5_lambda_re, s5_lambda_im, s5_log_dt,
                         s5_b_re, s5_b_im, s5_c_re, s5_c_im, s5_d, w_glu, w_xq, w_xo, w_up, w_down)
    batch, seq, _ = x_prompt.shape
    dec_batch, dec_seq, _ = x_sample.shape
    mem_len = mem_prompt.shape[1]
    depth = norm_w.shape[0]
    gn_w = ret_gn_w[0].astype(F32)

    mem_rows = mem_prompt.reshape(batch * mem_len, D_MODEL)
    wkv = jnp.concatenate([w_xk, w_xv], axis=2).astype(BF16)
    mem_k, mem_v, mem_kb, mem_vb = _memory_kv(mem_rows, mem_norm_w.astype(F32).reshape(depth, 1, D_MODEL), wkv,
                                              min(ROW_TILE, batch * mem_len))
    mem_kv_prompt = [(mem_kb[layer], mem_vb[layer]) for layer in range(depth)]

    pos_prompt = jnp.arange(seq)
    y_prompt, np_ = _run_group(
        x_prompt, pos_prompt, w, gn_w, mem_kv_prompt,
        jnp.zeros((batch, RET_H, RET_D, RET_D), F32), None, jnp.zeros((batch, 2 * S5_STATE), F32))

    past = cache_fox_k.shape[2]
    pos_sample = past + jnp.arange(dec_seq)
    mem_kv_sample = [(cache_mem_k[layer].reshape(dec_batch * mem_len, D_MODEL).astype(BF16),
                      cache_mem_v[layer].reshape(dec_batch * mem_len, D_MODEL).astype(BF16))
                     for layer in range(depth)]
    s5_st0 = jnp.concatenate([state_s5_re[0].reshape(dec_batch, S5_STATE),
                              state_s5_im[0].reshape(dec_batch, S5_STATE)], axis=1).astype(F32)
    y_sample, ns_ = _run_group(
        x_sample, pos_sample, w, gn_w, mem_kv_sample, cache_ret_state[0].astype(F32),
        (cache_fox_k[0], cache_fox_v[0], cache_fox_logf[0]), s5_st0)

    return (y_prompt, y_sample,
            np_["ret"], ns_["ret"],
            np_["fox_k"], np_["fox_v"], np_["fox_logf"],
            ns_["fox_k"], ns_["fox_v"], ns_["fox_logf"],
            np_["s5_re"], np_["s5_im"], ns_["s5_re"], ns_["s5_im"],
            mem_k.reshape(depth, batch, mem_len, X_H, X_D), mem_v.reshape(depth, batch, mem_len, X_H, X_D))
```
